```python
import jax
import jax.numpy as jnp
from jax import lax
import numpy as np

D_MODEL = 2048
BATCH = 4
SEQ = 2048
DEPTH = 2
DEC_BATCH = 8
DEC_SEQ = 1
PAST_LEN = 16384
PAGE_SIZE = 128

NSA_HEADS = 16
NSA_KV_HEADS = 4
NSA_GROUP = NSA_HEADS // NSA_KV_HEADS
NSA_HD = 64
NSA_BLOCK = 64
NSA_TOPK = 16
NSA_WINDOW = 512
NSA_CMP_HID = 128
NSA_Q_BLOCK = 32
GLA_HEADS = 4
GLA_DK = 64
GLA_DV = 128
GLA_GATE_RANK = 16
GLA_TAU = 16.0
GLA_CHUNK = 64
RW_HEADS = 8
RW_HD = 64
RW_DECAY_RANK = 32
RW_A_RANK = 32
RW_G_RANK = 96
RW_GN_EPS = 64e-5
D_FF = 4 * D_MODEL
NORM_EPS = 1e-6

NSA_Q_DIM = NSA_HEADS * NSA_HD
NSA_KV_DIM = NSA_KV_HEADS * NSA_HD
GLA_QK_DIM = GLA_HEADS * GLA_DK
GLA_V_DIM = GLA_HEADS * GLA_DV
RW_DIM = RW_HEADS * RW_HD
RW_SIZES = (RW_DIM, RW_DIM, RW_DIM, RW_DECAY_RANK, RW_A_RANK, RW_G_RANK)
RW_PROJ = 3 * RW_DIM + RW_DECAY_RANK + RW_A_RANK + RW_G_RANK
IN_SIZES = (NSA_Q_DIM, 2 * NSA_KV_DIM, 2 * NSA_KV_DIM, 2 * NSA_KV_DIM, 3 * NSA_HEADS,
            GLA_QK_DIM, GLA_QK_DIM, GLA_V_DIM, GLA_GATE_RANK, GLA_V_DIM, RW_PROJ, 3 * D_MODEL)
N_IN = NSA_Q_DIM + 6 * NSA_KV_DIM + 3 * NSA_HEADS + 2 * GLA_QK_DIM + 2 * GLA_V_DIM + GLA_GATE_RANK + RW_PROJ + 3 * D_MODEL

F32 = jnp.float32

kernel_name = "hybrid_nsa_gla_rwkv7_step"


def split_cols(z, sizes):
    cuts = np.cumsum(np.array(sizes))[:-1]
    return jnp.split(z, [int(c) for c in cuts], axis=-1)


def rms_norm(x, g, eps=NORM_EPS):
    xf = x.astype(F32)
    y = xf * lax.rsqrt(jnp.mean(xf * xf, axis=-1, keepdims=True) + eps)
    return (y * g.astype(F32)).astype(x.dtype)


def masked_softmax(s, mask):
    s = jnp.where(mask, s, -jnp.inf)
    m = jnp.max(s, axis=-1, keepdims=True)
    m = jnp.where(jnp.isfinite(m), m, 0.0)
    e = jnp.where(mask, jnp.exp(s - m), 0.0)
    return e / jnp.maximum(jnp.sum(e, axis=-1, keepdims=True), 1e-30)


def alibi_slopes():
    h = jnp.arange(1, NSA_HEADS + 1, dtype=F32)
    return jnp.exp2(-8.0 * h / NSA_HEADS).reshape(NSA_KV_HEADS, NSA_GROUP)


def pad_rows(rows, blk):
    pad = (-rows.shape[1]) % blk
    return jnp.pad(rows, [(0, 0), (0, pad)] + [(0, 0)] * (rows.ndim - 2))


def compress(rows, pos_emb, w1, w2):
    rows = pad_rows(rows, NSA_BLOCK)
    b, l, g, d = rows.shape
    nb = l // NSA_BLOCK
    z = rows.reshape(b, nb, NSA_BLOCK, g, d) + pos_emb[None, None, :, None, :]
    z = z.transpose(0, 1, 3, 2, 4).reshape(b, nb, g, NSA_BLOCK * d)
    return jax.nn.gelu(z @ w1) @ w2


def to_blocks(rows):
    rows = pad_rows(rows, NSA_BLOCK)
    b, l, g, d = rows.shape
    return rows.reshape(b, l // NSA_BLOCK, NSA_BLOCK, g, d).transpose(0, 3, 1, 2, 4).astype(F32)


def nsa_attend(q, q_pos, gates, k_cmp, v_cmp, k_blk, v_blk, k_win, v_win, win_pos, slopes):
    b = q.shape[0]
    scale = NSA_HD ** -0.5
    qf = q.astype(F32)
    sl = slopes[None, :, :, None, None]
    nb = k_cmp.shape[1]
    blk = jnp.arange(nb, dtype=jnp.int32)
    d_cmp = q_pos[:, None] - ((blk[None, :] + 1) * NSA_BLOCK - 1)
    s = jnp.einsum("bqgrd,bngd->bgrqn", qf, k_cmp.astype(F32)) * scale - sl * d_cmp.astype(F32)
    p_cmp = masked_softmax(s, d_cmp >= 0)
    o_cmp = jnp.einsum("bgrqn,bngd->bqgrd", p_cmp, v_cmp.astype(F32))
    cur = q_pos // NSA_BLOCK
    forced = (blk[None, :] == 0) | (blk[None, :] == cur[:, None]) | (blk[None, :] == cur[:, None] - 1)
    score = jnp.where(forced, jnp.inf, jnp.sum(p_cmp, axis=2))
    score = jnp.where(blk[None, :] <= cur[:, None], score, -jnp.inf)
    _, idx = lax.top_k(score, min(NSA_TOPK, nb))
    ok = idx <= cur[None, None, :, None]
    bi = jnp.arange(b)[:, None, None, None]
    gi = jnp.arange(NSA_KV_HEADS)[None, :, None, None]
    k_sel = k_blk[bi, gi, idx]
    v_sel = v_blk[bi, gi, idx]
    d_sel = q_pos[None, None, :, None, None] - (idx[..., None] * NSA_BLOCK + jnp.arange(NSA_BLOCK, dtype=jnp.int32))
    m_sel = ((d_sel >= 0) & ok[..., None])[:, :, None]
    s = (jnp.einsum("bqgrd,bgqkld->bgrqkl", qf, k_sel) * scale
         - slopes[None, :, :, None, None, None] * d_sel[:, :, None].astype(F32))
    shp = s.shape
    p = masked_softmax(s.reshape(shp[:4] + (-1,)), m_sel.reshape(m_sel.shape[:4] + (-1,))).reshape(shp)
    o_sel = jnp.einsum("bgrqkl,bgqkld->bqgrd", p, v_sel)
    d_win = q_pos[:, None] - win_pos[None, :]
    m_win = (d_win >= 0) & (d_win < NSA_WINDOW) & (win_pos >= 0)[None, :]
    s = jnp.einsum("bqgrd,bsgd->bgrqs", qf, k_win.astype(F32)) * scale - sl * d_win.astype(F32)
    p = masked_softmax(s, m_win)
    o_win = jnp.einsum("bgrqs,bsgd->bqgrd", p, v_win.astype(F32))
    g = gates.astype(F32)[..., None]
    o = g[:, :, 0] * o_cmp + g[:, :, 1] * o_sel + g[:, :, 2] * o_win
    return o.astype(q.dtype)


def nsa_prompt(q, gates, cmp_rows, sel_rows, win_rows, cmp_pos, cmp_w1, cmp_w2, kcmp_g, slopes):
    b, t = q.shape[:2]
    k_cmp = rms_norm(compress(cmp_rows[:, :, 0], cmp_pos[0], cmp_w1[0], cmp_w2[0]), kcmp_g)
    v_cmp = compress(cmp_rows[:, :, 1], cmp_pos[1], cmp_w1[1], cmp_w2[1])
    k_blk = to_blocks(sel_rows[:, :, 0])
    v_blk = to_blocks(sel_rows[:, :, 1])
    win_pad = jnp.pad(win_rows, ((0, 0), (NSA_WINDOW, 0), (0, 0), (0, 0), (0, 0)))
    nq = t // NSA_Q_BLOCK
    band = NSA_WINDOW + NSA_Q_BLOCK
    q_b = jnp.swapaxes(q.reshape((b, nq, NSA_Q_BLOCK) + q.shape[2:]), 0, 1)
    g_b = jnp.swapaxes(gates.reshape((b, nq, NSA_Q_BLOCK) + gates.shape[2:]), 0, 1)
    starts = jnp.arange(nq, dtype=jnp.int32) * NSA_Q_BLOCK

    def one_block(args):
        q_, g_, s0 = args
        kv = lax.dynamic_slice_in_dim(win_pad, s0, band, axis=1)
        win_pos = s0 - NSA_WINDOW + jnp.arange(band, dtype=jnp.int32)
        q_pos = s0 + jnp.arange(NSA_Q_BLOCK, dtype=jnp.int32)
        return nsa_attend(q_, q_pos, g_, k_cmp, v_cmp, k_blk, v_blk, kv[:, :, 0], kv[:, :, 1], win_pos, slopes)

    o = lax.map(one_block, (q_b, g_b, starts))
    return jnp.swapaxes(o, 0, 1).reshape(q.shape)


def nsa_sample(q, gates, cmp_rows, sel_rows, win_rows, cache_cmp, cache_sel, cache_win, page_table,
               cmp_pos, cmp_w1, cmp_w2, kcmp_g, slopes):
    nbatch, t = q.shape[:2]
    past = page_table.shape[1] * cache_cmp.shape[1]

    def with_past(cache, rows):
        old = cache[page_table].reshape((nbatch, past) + cache.shape[2:])
        return jnp.concatenate([old.astype(rows.dtype), rows], axis=1)

    cmp_all = with_past(cache_cmp, cmp_rows)
    sel_all = with_past(cache_sel, sel_rows)
    k_cmp = rms_norm(compress(cmp_all[:, :, 0], cmp_pos[0], cmp_w1[0], cmp_w2[0]), kcmp_g)
    v_cmp = compress(cmp_all[:, :, 1], cmp_pos[1], cmp_w1[1], cmp_w2[1])
    w_buf = cache_win.shape[1]
    win_all = jnp.concatenate([cache_win.astype(win_rows.dtype), win_rows], axis=1)
    win_pos = past - w_buf + jnp.arange(w_buf + t, dtype=jnp.int32)
    q_pos = past + jnp.arange(t, dtype=jnp.int32)
    return nsa_attend(q, q_pos, gates, k_cmp, v_cmp, to_blocks(sel_all[:, :, 0]), to_blocks(sel_all[:, :, 1]),
                      win_all[:, :, 0], win_all[:, :, 1], win_pos, slopes)


def gla_chunked(q, k, v, log_a, s0):
    b, t, h, dk = q.shape
    dv = v.shape[-1]
    c = min(GLA_CHUNK, t)
    q, k, v, log_a = (pad_rows(z, c) for z in (q, k, v, log_a))
    n = q.shape[1] // c
    chunks = lambda z: z.reshape(b, n, c, h, z.shape[-1]).transpose(1, 0, 3, 2, 4)
    causal = jnp.tril(jnp.ones((c, c), dtype=bool))

    def step(state, inp):
        q_c, k_c, v_c, a_c = inp
        cum = jnp.cumsum(a_c, axis=2)
        last = cum[:, :, -1:, :]
        q_dec = q_c * jnp.exp(cum)
        att = jnp.where(causal, jnp.einsum("bhtd,bhsd->bhts", q_dec, k_c * jnp.exp(-cum)), 0.0)
        o = jnp.einsum("bhts,bhsv->bhtv", att, v_c) + jnp.einsum("bhtd,bhdv->bhtv", q_dec, state)
        state = (state * jnp.exp(last)[:, :, 0, :, None]
                 + jnp.einsum("bhsd,bhsv->bhdv", k_c * jnp.exp(last - cum), v_c))
        return state, o

    s_t, o = lax.scan(step, s0, (chunks(q), chunks(k), chunks(v), chunks(log_a)))
    o = o.transpose(1, 0, 3, 2, 4).reshape(b, n * c, h, dv)[:, :t]
    return o, s_t


def gla_branch(q, k, v, a_lr, g_out, s0, a2, a_b, norm_g):
    b, t, _ = q.shape
    log_a = jax.nn.log_sigmoid(a_lr.astype(F32) @ a2 + a_b) / GLA_TAU
    hk = lambda z: z.astype(F32).reshape(b, t, GLA_HEADS, GLA_DK)
    o, s_t = gla_chunked(hk(q) * GLA_DK ** -0.5, hk(k), v.astype(F32).reshape(b, t, GLA_HEADS, GLA_DV),
                         hk(log_a), s0.astype(F32))
    o = rms_norm(o, norm_g).reshape(b, t, GLA_V_DIM)
    return o * jax.nn.silu(g_out.astype(F32)), s_t


def rwkv7_branch(rw, prev_row, s0, mu, w0, w2, a0, a2, g2, k_k, k_a, r_k, ln_g, ln_b):
    b, t, _ = rw.shape
    shifted = jnp.concatenate([prev_row[:, None, :].astype(rw.dtype), rw[:, :-1]], axis=1)
    xm = (rw + (shifted - rw) * mu).astype(F32)
    r, k, v, wl, al, gl = split_cols(xm, RW_SIZES)
    w = -jax.nn.softplus(-(w0 + jnp.tanh(wl) @ w2)) - 0.5
    decay = jnp.exp(-jnp.exp(w))
    a = jax.nn.sigmoid(a0 + al @ a2)
    g = jax.nn.sigmoid(gl) @ g2
    heads = lambda z: z.reshape(b, t, RW_HEADS, RW_HD)
    kk = heads(k * k_k)
    kk = kk / jnp.maximum(jnp.sqrt(jnp.sum(kk * kk, axis=-1, keepdims=True)), 1e-12)
    k = k * (1.0 + (a - 1.0) * k_a)
    rh, kh, vh = heads(r), heads(k), heads(v)
    tm = lambda z: jnp.swapaxes(z, 0, 1)

    def step(state, inp):
        r_t, dec_t, k_t, v_t, kk_t, kka_t = inp
        sa = jnp.einsum("bhvk,bhk->bhv", state, -kk_t)
        state = (state * dec_t[:, :, None, :] + sa[..., None] * kka_t[:, :, None, :]
                 + v_t[..., None] * k_t[:, :, None, :])
        return state, jnp.einsum("bhvk,bhk->bhv", state, r_t)

    s_t, o = lax.scan(step, s0.astype(F32),
                      (tm(rh), tm(heads(decay)), tm(kh), tm(vh), tm(kk), tm(kk * heads(a))))
    o = tm(o)
    mean = jnp.mean(o, axis=-1, keepdims=True)
    var = jnp.mean(jnp.square(o - mean), axis=-1, keepdims=True)
    o = ((o - mean) * lax.rsqrt(var + RW_GN_EPS)).reshape(b, t, RW_DIM) * ln_g + ln_b
    bonus = jnp.sum(rh * kh * r_k, axis=-1, keepdims=True) * vh
    return (o + bonus.reshape(b, t, RW_DIM)) * g, s_t


def trunk_layer(x, lp, nsa_fn, shift_prev, gla_s0, rw_s0):
    b, t, _ = x.shape
    hn = rms_norm(x, lp["norm1_g"])
    (nsa_q, nsa_cmp, nsa_sel, nsa_win, nsa_gate, gla_q, gla_k, gla_v, gla_a, gla_g, rw, merge) = split_cols(
        hn @ lp["w_in"], IN_SIZES)
    qk_g = lp["nsa_qk_g"]
    kv_shape = (b, t, 2, NSA_KV_HEADS, NSA_HD)
    q = rms_norm(nsa_q.reshape(b, t, NSA_KV_HEADS, NSA_GROUP, NSA_HD), qk_g[0])
    cmp_rows = nsa_cmp.reshape(kv_shape)
    sel = nsa_sel.reshape(kv_shape)
    sel_rows = jnp.stack([rms_norm(sel[:, :, 0], qk_g[2]), sel[:, :, 1]], axis=2)
    win = nsa_win.reshape(kv_shape)
    win_rows = jnp.stack([rms_norm(win[:, :, 0], qk_g[3]), win[:, :, 1]], axis=2)
    gates = jax.nn.sigmoid(nsa_gate.reshape(b, t, 3, NSA_KV_HEADS, NSA_GROUP))
    o_nsa = nsa_fn(q, gates, cmp_rows, sel_rows, win_rows).reshape(b, t, NSA_Q_DIM)
    o_gla, gla_st = gla_branch(gla_q, gla_k, gla_v, gla_a, gla_g, gla_s0, lp["gla_a2"], lp["gla_a_b"], lp["gla_norm_g"])
    o_rw, rw_st = rwkv7_branch(rw, shift_prev, rw_s0, lp["rw_mu"], lp["rw_w0"], lp["rw_w2"], lp["rw_a0"], lp["rw_a2"],
                               lp["rw_g2"], lp["rw_kk"], lp["rw_ka"], lp["rw_rk"], lp["rw_ln_g"], lp["rw_ln_b"])
    mg = jax.nn.sigmoid(merge.reshape(b, t, 3, D_MODEL).astype(F32))
    merged = (mg[:, :, 0] * (o_nsa @ lp["nsa_up"]) + mg[:, :, 1] * (o_gla @ lp["gla_up"])
              + mg[:, :, 2] * (o_rw @ lp["rw_up"]))
    x = x + (merged @ lp["w_out"]).astype(x.dtype)
    h2 = rms_norm(x, lp["norm2_g"])
    x = x + (jnp.square(jax.nn.relu(h2 @ lp["mlp_w1"])) @ lp["mlp_w2"]).astype(x.dtype)
    return x, cmp_rows, sel_rows, win_rows, gla_st, rw_st, rw[:, -1]


def setup_inputs(seed: int = 0) -> dict:
    key = jax.random.key(seed)
    keys = iter(jax.random.split(key, 64))

    def nrm(shape, scale=1.0):
        return scale * jax.random.normal(next(keys), shape, F32)

    def gain(shape):
        return 1.0 + nrm(shape, 0.05)

    n_pages = PAST_LEN // PAGE_SIZE
    n_pool = (5 * DEC_BATCH * n_pages) // 4
    w_buf = min(NSA_WINDOW, PAST_LEN)
    kv_row = (2, NSA_KV_HEADS, NSA_HD)
    page_table = jax.random.permutation(next(keys), n_pool)[: DEC_BATCH * n_pages]
    page_table = page_table.reshape(DEC_BATCH, n_pages).astype(jnp.int32)
    return {
        "x_prompt": nrm((BATCH, SEQ, D_MODEL)),
        "x_sample": nrm((DEC_BATCH, DEC_SEQ, D_MODEL)),
        "cache_cmp_kv": nrm((DEPTH, n_pool, PAGE_SIZE) + kv_row),
        "cache_sel_kv": nrm((DEPTH, n_pool, PAGE_SIZE) + kv_row),
        "cache_win_kv": nrm((DEPTH, DEC_BATCH, w_buf) + kv_row),
        "state_gla": nrm((DEPTH, DEC_BATCH, GLA_HEADS, GLA_DK, GLA_DV)),
        "state_rwkv": nrm((DEPTH, DEC_BATCH, RW_HEADS, RW_HD, RW_HD), 0.5),
        "state_rwkv_shift": nrm((DEPTH, DEC_BATCH, RW_PROJ)),
        "page_table": page_table,
        "norm1_g": gain((DEPTH, D_MODEL)),
        "w_in": nrm((DEPTH, D_MODEL, N_IN), D_MODEL ** -0.5),
        "nsa_qk_g": gain((DEPTH, 4, NSA_HD)),
        "cmp_pos": nrm((DEPTH, 2, NSA_BLOCK, NSA_HD), 0.2),
        "cmp_w1": nrm((DEPTH, 2, NSA_BLOCK * NSA_HD, NSA_CMP_HID), (NSA_BLOCK * NSA_HD) ** -0.5),
        "cmp_w2": nrm((DEPTH, 2, NSA_CMP_HID, NSA_HD), NSA_CMP_HID ** -0.5),
        "nsa_up": nrm((DEPTH, NSA_Q_DIM, D_MODEL), NSA_Q_DIM ** -0.5),
        "gla_a2": nrm((DEPTH, GLA_GATE_RANK, GLA_QK_DIM), GLA_GATE_RANK ** -0.5),
        "gla_a_b": nrm((DEPTH, GLA_QK_DIM), 0.1),
        "gla_norm_g": gain((DEPTH, GLA_DV)),
        "gla_up": nrm((DEPTH, GLA_V_DIM, D_MODEL), GLA_V_DIM ** -0.5),
        "rw_mu": jax.random.uniform(next(keys), (DEPTH, RW_PROJ), F32),
        "rw_w0": nrm((DEPTH, RW_DIM), 0.5),
        "rw_w2": nrm((DEPTH, RW_DECAY_RANK, RW_DIM), 0.5 * RW_DECAY_RANK ** -0.5),
        "rw_a0": nrm((DEPTH, RW_DIM), 0.1),
        "rw_a2": nrm((DEPTH, RW_A_RANK, RW_DIM), RW_A_RANK ** -0.5),
        "rw_g2": nrm((DEPTH, RW_G_RANK, RW_DIM), RW_G_RANK ** -0.5),
        "rw_kk": 0.85 + nrm((DEPTH, RW_DIM), 0.05),
        "rw_ka": gain((DEPTH, RW_DIM)),
        "rw_rk": nrm((DEPTH, RW_HEADS, RW_HD), 0.1),
        "rw_ln_g": gain((DEPTH, RW_DIM)),
        "rw_ln_b": nrm((DEPTH, RW_DIM), 0.01),
        "rw_up": nrm((DEPTH, RW_DIM, D_MODEL), RW_DIM ** -0.5),
        "w_out": nrm((DEPTH, D_MODEL, D_MODEL), D_MODEL ** -0.5),
        "norm2_g": gain((DEPTH, D_MODEL)),
        "mlp_w1": nrm((DEPTH, D_MODEL, D_FF), D_MODEL ** -0.5),
        "mlp_w2": nrm((DEPTH, D_FF, D_MODEL), D_FF ** -0.5),
    }


def reference(x_prompt, x_sample, cache_cmp_kv, cache_sel_kv, cache_win_kv, state_gla, state_rwkv, state_rwkv_shift,
              page_table, norm1_g, w_in, nsa_qk_g, cmp_pos, cmp_w1, cmp_w2, nsa_up, gla_a2, gla_a_b, gla_norm_g, gla_up,
              rw_mu, rw_w0, rw_w2, rw_a0, rw_a2, rw_g2, rw_kk, rw_ka, rw_rk, rw_ln_g, rw_ln_b, rw_up, w_out, norm2_g,
              mlp_w1, mlp_w2):
    slopes = alibi_slopes()
    b, t, _ = x_prompt.shape
    w_keep = min(NSA_WINDOW, t)
    gla0 = jnp.zeros((b, GLA_HEADS, GLA_DK, GLA_DV), F32)
    rw0 = jnp.zeros((b, RW_HEADS, RW_HD, RW_HD), F32)
    shift0 = jnp.zeros((b, RW_PROJ), x_prompt.dtype)
    xp, xs = x_prompt, x_sample
    p_cmp, p_sel, p_win, p_gla, p_rw, p_sh = [], [], [], [], [], []
    s_cmp, s_sel, s_win, s_gla, s_rw, s_sh = [], [], [], [], [], []
    for l in range(DEPTH):
        lp = {"norm1_g": norm1_g[l], "w_in": w_in[l], "nsa_qk_g": nsa_qk_g[l], "nsa_up": nsa_up[l],
              "gla_a2": gla_a2[l], "gla_a_b": gla_a_b[l], "gla_norm_g": gla_norm_g[l], "gla_up": gla_up[l],
              "rw_mu": rw_mu[l], "rw_w0": rw_w0[l], "rw_w2": rw_w2[l], "rw_a0": rw_a0[l], "rw_a2": rw_a2[l],
              "rw_g2": rw_g2[l], "rw_kk": rw_kk[l], "rw_ka": rw_ka[l], "rw_rk": rw_rk[l], "rw_ln_g": rw_ln_g[l],
              "rw_ln_b": rw_ln_b[l], "rw_up": rw_up[l], "w_out": w_out[l], "norm2_g": norm2_g[l],
              "mlp_w1": mlp_w1[l], "mlp_w2": mlp_w2[l]}

        def nsa_p(q, g, c, s, w, l=l):
            return nsa_prompt(q, g, c, s, w, cmp_pos[l], cmp_w1[l], cmp_w2[l], nsa_qk_g[l, 1], slopes)

        def nsa_s(q, g, c, s, w, l=l):
            return nsa_sample(q, g, c, s, w, cache_cmp_kv[l], cache_sel_kv[l], cache_win_kv[l], page_table,
                              cmp_pos[l], cmp_w1[l], cmp_w2[l], nsa_qk_g[l, 1], slopes)

        xp, c, s, w, g_st, r_st, sh = trunk_layer(xp, lp, nsa_p, shift0, gla0, rw0)
        p_cmp.append(c)
        p_sel.append(s)
        p_win.append(w[:, t - w_keep:])
        p_gla.append(g_st.astype(state_gla.dtype))
        p_rw.append(r_st.astype(state_rwkv.dtype))
        p_sh.append(sh.astype(state_rwkv_shift.dtype))
        xs, c, s, w, g_st, r_st, sh = trunk_layer(xs, lp, nsa_s, state_rwkv_shift[l], state_gla[l], state_rwkv[l])
        s_cmp.append(c)
        s_sel.append(s)
        s_win.append(w)
        s_gla.append(g_st.astype(state_gla.dtype))
        s_rw.append(r_st.astype(state_rwkv.dtype))
        s_sh.append(sh.astype(state_rwkv_shift.dtype))
    return (xp, xs,
            jnp.stack(p_cmp), jnp.stack(p_sel), jnp.stack(p_win), jnp.stack(p_gla), jnp.stack(p_rw), jnp.stack(p_sh),
            jnp.stack(s_cmp), jnp.stack(s_sel), jnp.stack(s_win), jnp.stack(s_gla), jnp.stack(s_rw), jnp.stack(s_sh))
```

```python
import functools

import numpy as np
import jax
import jax.numpy as jnp
from jax import lax
from jax.experimental import pallas as pl
from jax.experimental.pallas import tpu as pltpu

F32 = jnp.float32
BF16 = jnp.bfloat16
HIGHEST = lax.Precision.HIGHEST

D_MODEL = 2048
DEPTH = 2
PAGE_SIZE = 128
NSA_HEADS = 16
NSA_KV_HEADS = 4
NSA_GROUP = NSA_HEADS // NSA_KV_HEADS
NSA_HD = 64
NSA_BLOCK = 64
NSA_TOPK = 16
NSA_WINDOW = 512
NSA_CMP_HID = 128
GLA_HEADS = 4
GLA_DK = 64
GLA_DV = 128
GLA_GATE_RANK = 16
GLA_TAU = 16.0
GLA_CHUNK = 64
RW_HEADS = 8
RW_HD = 64
RW_DECAY_RANK = 32
RW_A_RANK = 32
RW_G_RANK = 96
RW_GN_EPS = 64e-5
D_FF = 4 * D_MODEL
NORM_EPS = 1e-6

NSA_Q_DIM = NSA_HEADS * NSA_HD
NSA_KV_DIM = NSA_KV_HEADS * NSA_HD
GLA_QK_DIM = GLA_HEADS * GLA_DK
GLA_V_DIM = GLA_HEADS * GLA_DV
RW_DIM = RW_HEADS * RW_HD
RW_SIZES = (RW_DIM, RW_DIM, RW_DIM, RW_DECAY_RANK, RW_A_RANK, RW_G_RANK)
RW_PROJ = sum(RW_SIZES)
IN_SIZES = (NSA_Q_DIM, 2 * NSA_KV_DIM, 2 * NSA_KV_DIM, 2 * NSA_KV_DIM, 3 * NSA_HEADS,
            GLA_QK_DIM, GLA_QK_DIM, GLA_V_DIM, GLA_GATE_RANK, GLA_V_DIM, RW_PROJ, 3 * D_MODEL)

LANES = 128
SUBLANES_BF16 = 16
VMEM_LIMIT = 56 * 1024 * 1024

GATE_W = LANES
NSA_W = NSA_Q_DIM + 6 * NSA_KV_DIM + GATE_W
GLA_A_W = LANES
GLA_W = 2 * GLA_QK_DIM + 2 * GLA_V_DIM + GLA_A_W
RW_LR_W = LANES
RW_W = 3 * RW_DIM + 3 * RW_LR_W
MERGE_W = 3 * D_MODEL


def _params(sem):
    return pltpu.CompilerParams(dimension_semantics=sem, vmem_limit_bytes=VMEM_LIMIT)


def _largest_tile(n, cap):
    best = None
    for t in range(LANES, min(n, cap) + 1, LANES):
        if n % t == 0:
            best = t
    return best if best is not None else n


def _sigmoid(x):
    return 1.0 / (1.0 + jnp.exp(-x))


def _softplus(y):
    return jnp.maximum(y, 0.0) + jnp.log(1.0 + jnp.exp(-jnp.abs(y)))


def _masked_softmax(s, mask):
    s = jnp.where(mask, s, -jnp.inf)
    m = jnp.max(s, axis=-1, keepdims=True)
    m = jnp.where(m > -jnp.inf, m, 0.0)
    e = jnp.where(mask, jnp.exp(s - m), 0.0)
    return e / jnp.maximum(jnp.sum(e, axis=-1, keepdims=True), 1e-30)


def _dot(a, b, precision=None):
    return jnp.dot(a, b, preferred_element_type=F32, precision=precision)


def _dot_nt(a, b, precision=None):
    return lax.dot_general(a, b, (((1,), (1,)), ((), ())), preferred_element_type=F32, precision=precision)


def _dot_tn(a, b, precision=None):
    return lax.dot_general(a, b, (((0,), (0,)), ((), ())), preferred_element_type=F32, precision=precision)


def _rmsnorm_kernel(x_ref, g_ref, o_ref):
    x = x_ref[...]
    y = x * lax.rsqrt(jnp.mean(x * x, axis=-1, keepdims=True) + NORM_EPS) * g_ref[...]
    o_ref[...] = y.astype(o_ref.dtype)


def rmsnorm_rows(x, g, tm):
    m, d = x.shape
    return pl.pallas_call(
        _rmsnorm_kernel,
        grid=(m // tm,),
        in_specs=[pl.BlockSpec((tm, d), lambda i: (i, 0)), pl.BlockSpec((1, d), lambda i: (0, 0))],
        out_specs=pl.BlockSpec((tm, d), lambda i: (i, 0)),
        out_shape=jax.ShapeDtypeStruct((m, d), BF16),
        compiler_params=_params(("parallel",)),
        name="rmsnorm_rows",
    )(x, g.reshape(1, d))


def _mm_kernel(*refs, nk, mode):
    if mode == "residual":
        a_ref, w_ref, r_ref, o_ref = refs[:4]
        scratch = refs[4:]
    else:
        a_ref, w_ref, o_ref = refs[:3]
        scratch = refs[3:]
    part = _dot(a_ref[...], w_ref[...])

    def finish(acc):
        if mode == "relu2":
            acc = jnp.square(jnp.maximum(acc, 0.0))
        elif mode == "residual":
            acc = r_ref[...] + acc
        o_ref[...] = acc.astype(o_ref.dtype)

    if nk == 1:
        finish(part)
    else:
        acc_ref = scratch[0]
        k = pl.program_id(2)

        @pl.when(k == 0)
        def _():
            acc_ref[...] = part

        @pl.when(k > 0)
        def _():
            acc_ref[...] += part

        @pl.when(k == nk - 1)
        def _():
            finish(acc_ref[...])


def matmul(a, w, mode="plain", residual=None, tm=512):
    m, kdim = a.shape
    n = w.shape[1]
    tm = min(tm, m)
    tk = min(kdim, 2048)
    nk = kdim // tk
    tn = _largest_tile(n, 1024 if (mode == "residual" or nk > 1) else 2048)
    in_specs = [pl.BlockSpec((tm, tk), lambda i, j, k: (i, k)), pl.BlockSpec((tk, tn), lambda i, j, k: (k, j))]
    args = [a, w]
    if mode == "residual":
        in_specs.append(pl.BlockSpec((tm, tn), lambda i, j, k: (i, j)))
        args.append(residual)
    out_dtype = BF16 if mode == "relu2" else F32
    return pl.pallas_call(
        functools.partial(_mm_kernel, nk=nk, mode=mode),
        grid=(m // tm, n // tn, nk),
        in_specs=in_specs,
        out_specs=pl.BlockSpec((tm, tn), lambda i, j, k: (i, j)),
        out_shape=jax.ShapeDtypeStruct((m, n), out_dtype),
        scratch_shapes=[pltpu.VMEM((tm, tn), F32)] if nk > 1 else [],
        compiler_params=_params(("parallel", "parallel", "arbitrary")),
        name="matmul_" + mode,
    )(*args)


def _merge_kernel(on_ref, og_ref, or_ref, wn_ref, wg_ref, wr_ref, m0_ref, m1_ref, m2_ref, o_ref):
    acc = _sigmoid(m0_ref[...]) * _dot(on_ref[...].astype(BF16), wn_ref[...])
    acc += _sigmoid(m1_ref[...]) * _dot(og_ref[...].astype(BF16), wg_ref[...])
    acc += _sigmoid(m2_ref[...]) * _dot(or_ref[...].astype(BF16), wr_ref[...])
    o_ref[...] = acc.astype(o_ref.dtype)


def merge_branches(o_nsa, o_gla, o_rw, w_nsa, w_gla, w_rw, z_merge, tm=512, tn=1024):
    m = o_nsa.shape[0]
    tm = min(tm, m)
    nj = D_MODEL // tn
    row = lambda kd: pl.BlockSpec((tm, kd), lambda i, j: (i, 0))
    wsp = lambda kd: pl.BlockSpec((kd, tn), lambda i, j: (0, j))
    gate = lambda b: pl.BlockSpec((tm, tn), lambda i, j: (i, j + b * nj))
    return pl.pallas_call(
        _merge_kernel,
        grid=(m // tm, nj),
        in_specs=[row(NSA_Q_DIM), row(GLA_V_DIM), row(RW_DIM), wsp(NSA_Q_DIM), wsp(GLA_V_DIM), wsp(RW_DIM),
                  gate(0), gate(1), gate(2)],
        out_specs=pl.BlockSpec((tm, tn), lambda i, j: (i, j)),
        out_shape=jax.ShapeDtypeStruct((m, D_MODEL), BF16),
        compiler_params=_params(("parallel", "parallel")),
        name="merge_branches",
    )(o_nsa, o_gla, o_rw, w_nsa, w_gla, w_rw, z_merge, z_merge, z_merge)


def _head_rmsnorm(x, g):
    return x * lax.rsqrt(jnp.mean(x * x, axis=-1, keepdims=True) + NORM_EPS) * g


def _nsa_prep_kernel(z_ref, g_ref, qn_ref, cmp_ref, sel_ref, win_ref, selb_ref, winb_ref, gates_ref):
    hd = NSA_HD
    g_q, g_sel, g_win = g_ref[0:1, :], g_ref[2:3, :], g_ref[3:4, :]
    scale = NSA_HD ** -0.5
    for h in range(NSA_HEADS):
        q = _head_rmsnorm(z_ref[:, h * hd:(h + 1) * hd], g_q)
        qn_ref[:, h * hd:(h + 1) * hd] = (q * scale).astype(BF16)
    c0 = NSA_Q_DIM
    cmp_ref[...] = z_ref[:, c0:c0 + 2 * NSA_KV_DIM]
    for (off, gain, o_ref, ob_ref) in ((c0 + 2 * NSA_KV_DIM, g_sel, sel_ref, selb_ref),
                                       (c0 + 4 * NSA_KV_DIM, g_win, win_ref, winb_ref)):
        for h in range(NSA_KV_HEADS):
            k = _head_rmsnorm(z_ref[:, off + h * hd:off + (h + 1) * hd], gain)
            o_ref[:, h * hd:(h + 1) * hd] = k
            ob_ref[:, h * hd:(h + 1) * hd] = k.astype(BF16)
        v = z_ref[:, off + NSA_KV_DIM:off + 2 * NSA_KV_DIM]
        o_ref[:, NSA_KV_DIM:] = v
        ob_ref[:, NSA_KV_DIM:] = v.astype(BF16)
    gates_ref[...] = _sigmoid(z_ref[:, c0 + 6 * NSA_KV_DIM:])


def nsa_prep(z_nsa, qk_g, tm=256):
    m = z_nsa.shape[0]
    tm = min(tm, m)
    kvw = 2 * NSA_KV_DIM
    row = lambda w: pl.BlockSpec((tm, w), lambda i: (i, 0))
    return pl.pallas_call(
        _nsa_prep_kernel,
        grid=(m // tm,),
        in_specs=[row(NSA_W), pl.BlockSpec((4, NSA_HD), lambda i: (0, 0))],
        out_specs=[row(NSA_Q_DIM), row(kvw), row(kvw), row(kvw), row(kvw), row(kvw), row(GATE_W)],
        out_shape=[jax.ShapeDtypeStruct((m, NSA_Q_DIM), BF16),
                   jax.ShapeDtypeStruct((m, kvw), F32), jax.ShapeDtypeStruct((m, kvw), F32),
                   jax.ShapeDtypeStruct((m, kvw), F32), jax.ShapeDtypeStruct((m, kvw), BF16),
                   jax.ShapeDtypeStruct((m, kvw), BF16), jax.ShapeDtypeStruct((m, GATE_W), F32)],
        compiler_params=_params(("parallel",)),
        name="nsa_prep",
    )(z_nsa, qk_g)


QUARTERS = 2 * NSA_KV_DIM // LANES


def _compress_compute(x_ref, nb, pos_ref, w1_ref, w2_ref, kg_ref, kc_ref, vc_ref, acc_ref):
    acc_ref[...] = jnp.zeros_like(acc_ref)
    row_stride = NSA_BLOCK * QUARTERS

    def body(l, carry):
        for q in range(QUARTERS):
            kv = q // 2
            x = x_ref[pl.ds(l * QUARTERS + q, nb, stride=row_stride), :] + pos_ref[kv, pl.ds(l, 1), :]
            for half in range(2):
                acc_ref[q * 2 + half] += _dot(x[:, half * NSA_HD:(half + 1) * NSA_HD].astype(BF16), w1_ref[kv, l])
        return carry

    lax.fori_loop(0, NSA_BLOCK, body, 0)
    for q in range(QUARTERS):
        kv = q // 2
        for half in range(2):
            g = (q % 2) * 2 + half
            hid = jax.nn.gelu(acc_ref[q * 2 + half], approximate=True)
            s = _dot(hid.astype(BF16), w2_ref[kv])
            if kv == 0:
                kc_ref[:, g * NSA_HD:(g + 1) * NSA_HD] = _head_rmsnorm(s, kg_ref[...])
            else:
                vc_ref[:, g * NSA_HD:(g + 1) * NSA_HD] = s


def _compress_kernel(x_ref, pos_ref, w1_ref, w2_ref, kg_ref, kc_ref, vc_ref, acc_ref, *, nb):
    _compress_compute(x_ref, nb, pos_ref, w1_ref, w2_ref, kg_ref, kc_ref, vc_ref, acc_ref)


def _cmp_weight_specs(nidx):
    zero = lambda n: (lambda *idx: (0,) * n)
    return [pl.BlockSpec((2, NSA_BLOCK, LANES), zero(3)),
            pl.BlockSpec((2, NSA_BLOCK, NSA_HD, NSA_CMP_HID), zero(4)),
            pl.BlockSpec((2, NSA_CMP_HID, NSA_HD), zero(3)),
            pl.BlockSpec((1, NSA_HD), zero(2))]


def compress_rows(rows, cw, nb):
    nblk = rows.shape[0] // NSA_BLOCK
    nb = min(nb, nblk)
    x4 = rows.reshape(nblk * NSA_BLOCK * QUARTERS, LANES)
    out = jax.ShapeDtypeStruct((nblk, NSA_KV_DIM), F32)
    return pl.pallas_call(
        functools.partial(_compress_kernel, nb=nb),
        grid=(nblk // nb,),
        in_specs=[pl.BlockSpec((nb * NSA_BLOCK * QUARTERS, LANES), lambda i: (i, 0))] + _cmp_weight_specs(1),
        out_specs=[pl.BlockSpec((nb, NSA_KV_DIM), lambda i: (i, 0))] * 2,
        out_shape=[out, out],
        scratch_shapes=[pltpu.VMEM((2 * QUARTERS, nb, NSA_CMP_HID), F32)],
        compiler_params=_params(("parallel",)),
        name="compress_rows",
    )(x4, cw["pos2"], cw["w1"], cw["w2"], cw["kg"])


PAGES_PER_STEP = 32
BLOCKS_PER_PAGE = PAGE_SIZE // NSA_BLOCK


def _compress_pages_kernel(pt_ref, *refs, npg):
    page_refs = refs[:npg]
    pos_ref, w1_ref, w2_ref, kg_ref, kc_ref, vc_ref, x_ref, acc_ref = refs[npg:]
    rows = PAGE_SIZE * QUARTERS
    for p in range(npg):
        x_ref[p * rows:(p + 1) * rows, :] = page_refs[p][0, 0]
    _compress_compute(x_ref, npg * BLOCKS_PER_PAGE, pos_ref, w1_ref, w2_ref, kg_ref, kc_ref.at[0], vc_ref.at[0], acc_ref)


def compress_pages(cache, layer, page_table, cw):
    nbatch, n_pages = page_table.shape
    npg = min(PAGES_PER_STEP, n_pages)
    nb = npg * BLOCKS_PER_PAGE
    cache4 = cache.reshape(cache.shape[0], cache.shape[1], PAGE_SIZE * QUARTERS, LANES)
    page_spec = lambda p: pl.BlockSpec((1, 1, PAGE_SIZE * QUARTERS, LANES),
                                       lambda b, c, pt: (layer, pt[b, c * npg + p], 0, 0))
    wspecs = _cmp_weight_specs(3)
    out = jax.ShapeDtypeStruct((nbatch, n_pages * BLOCKS_PER_PAGE, NSA_KV_DIM), F32)
    grid_spec = pltpu.PrefetchScalarGridSpec(
        num_scalar_prefetch=1,
        grid=(nbatch, n_pages // npg),
        in_specs=[page_spec(p) for p in range(npg)] + wspecs,
        out_specs=[pl.BlockSpec((1, nb, NSA_KV_DIM), lambda b, c, pt: (b, c, 0))] * 2,
        scratch_shapes=[pltpu.VMEM((npg * PAGE_SIZE * QUARTERS, LANES), F32),
                        pltpu.VMEM((2 * QUARTERS, nb, NSA_CMP_HID), F32)],
    )
    return pl.pallas_call(
        functools.partial(_compress_pages_kernel, npg=npg),
        grid_spec=grid_spec,
        out_shape=[out, out],
        compiler_params=_params(("parallel", "arbitrary")),
        name="compress_pages",
    )(page_table, *([cache4] * npg), cw["pos2"], cw["w1"], cw["w2"], cw["kg"])


def _select_blocks(score, blk, cur, nblk):
    forced = (blk == 0) | (blk == cur) | (blk == cur - 1)
    score = jnp.where(forced, jnp.inf, score)
    score = jnp.where(blk <= cur, score, -jnp.inf)
    rank = jnp.zeros(score.shape, F32)
    for i in range(nblk):
        col = score[:, i:i + 1]
        ahead = (col > score) | ((col == score) & (blk > i))
        rank = rank + jnp.where(ahead, 1.0, 0.0)
    return (rank < float(min(NSA_TOPK, nblk))) & (blk <= cur)


def _alibi_slope(head):
    return 2.0 ** (-8.0 * (head + 1) / NSA_HEADS)


def _nsa_prompt_kernel(q_ref, gates_ref, selb_ref, winb_ref, kc_ref, vc_ref, o_ref, *, tq, t_len, band):
    qi = pl.program_id(1)
    q0 = qi * tq
    nblk = t_len // NSA_BLOCK
    hd = NSA_HD
    q_pos = q0 + lax.broadcasted_iota(jnp.int32, (tq, 1), 0)
    cur = q_pos // NSA_BLOCK
    blk = lax.broadcasted_iota(jnp.int32, (1, nblk), 1)
    d_cmp = q_pos - ((blk + 1) * NSA_BLOCK - 1)
    key_pos = lax.broadcasted_iota(jnp.int32, (1, t_len), 1)
    d_sel = q_pos - key_pos
    expand = (lax.broadcasted_iota(jnp.int32, (nblk, t_len), 1) // NSA_BLOCK
              == lax.broadcasted_iota(jnp.int32, (nblk, t_len), 0)).astype(BF16)
    w0 = jnp.maximum(q0 - NSA_WINDOW, 0)
    w0 = pl.multiple_of(w0, tq)
    win_pos = w0 + lax.broadcasted_iota(jnp.int32, (1, band), 1)
    d_win = q_pos - win_pos
    m_win = (d_win >= 0) & (d_win < NSA_WINDOW)
    d_cmp_f, d_sel_f, d_win_f = d_cmp.astype(F32), d_sel.astype(F32), d_win.astype(F32)
    for g in range(NSA_KV_HEADS):
        kc = kc_ref[0, :, g * hd:(g + 1) * hd].astype(BF16)
        vc = vc_ref[0, :, g * hd:(g + 1) * hd].astype(BF16)
        k_sel = selb_ref[0, :, g * hd:(g + 1) * hd]
        v_sel = selb_ref[0, :, NSA_KV_DIM + g * hd:NSA_KV_DIM + (g + 1) * hd]
        k_win = winb_ref[0, pl.ds(w0, band), g * hd:(g + 1) * hd]
        v_win = winb_ref[0, pl.ds(w0, band), NSA_KV_DIM + g * hd:NSA_KV_DIM + (g + 1) * hd]
        qs, slopes, o_cmp = [], [], []
        score = jnp.zeros((tq, nblk), F32)
        for r in range(NSA_GROUP):
            h = g * NSA_GROUP + r
            q = q_ref[0, :, h * hd:(h + 1) * hd]
            slope = _alibi_slope(h)
            p = _masked_softmax(_dot_nt(q, kc) - slope * d_cmp_f, d_cmp >= 0)
            score = score + p
            o_cmp.append(_dot(p.astype(BF16), vc))
            qs.append(q)
            slopes.append(slope)
        sel = _select_blocks(score, blk, cur, nblk)
        sel_keys = _dot(jnp.where(sel, 1.0, 0.0).astype(BF16), expand) > 0.5
        m_sel = sel_keys & (d_sel >= 0)
        for r in range(NSA_GROUP):
            h = g * NSA_GROUP + r
            p = _masked_softmax(_dot_nt(qs[r], k_sel) - slopes[r] * d_sel_f, m_sel)
            o_sel = _dot(p.astype(BF16), v_sel)
            p = _masked_softmax(_dot_nt(qs[r], k_win) - slopes[r] * d_win_f, m_win)
            o_win = _dot(p.astype(BF16), v_win)
            gate = lambda b: gates_ref[0, :, b * NSA_HEADS + h:b * NSA_HEADS + h + 1]
            o = gate(0) * o_cmp[r] + gate(1) * o_sel + gate(2) * o_win
            o_ref[0, :, h * hd:(h + 1) * hd] = o.astype(o_ref.dtype)


def nsa_prompt_attention(qn, gates, sel_b, win_b, k_cmp, v_cmp, tq=128):
    b, t, _ = qn.shape
    tq = min(tq, t)
    band = min(NSA_WINDOW + tq, t)
    nblk = t // NSA_BLOCK
    full = lambda w, n: pl.BlockSpec((1, n, w), lambda i, j: (i, 0, 0))
    tile = lambda w: pl.BlockSpec((1, tq, w), lambda i, j: (i, j, 0))
    return pl.pallas_call(
        functools.partial(_nsa_prompt_kernel, tq=tq, t_len=t, band=band),
        grid=(b, t // tq),
        in_specs=[tile(NSA_Q_DIM), tile(GATE_W), full(2 * NSA_KV_DIM, t), full(2 * NSA_KV_DIM, t),
                  full(NSA_KV_DIM, nblk), full(NSA_KV_DIM, nblk)],
        out_specs=tile(NSA_Q_DIM),
        out_shape=jax.ShapeDtypeStruct((b, t, NSA_Q_DIM), BF16),
        compiler_params=_params(("parallel", "arbitrary")),
        name="nsa_prompt_attention",
    )(qn, gates, sel_b, win_b, k_cmp, v_cmp)


def _head_group_mask(nrows):
    row = lax.broadcasted_iota(jnp.int32, (nrows, 1), 0)
    return [(row // NSA_GROUP) == g for g in range(NSA_KV_HEADS)]


def _slope_column(nrows, head0=0):
    row = lax.broadcasted_iota(jnp.int32, (nrows, 1), 0)
    col = jnp.zeros((nrows, 1), F32)
    for h in range(NSA_HEADS):
        col = jnp.where(row + head0 == h, 2.0 ** (-8.0 * (h + 1) / NSA_HEADS), col)
    return col


def _nsa_sample1_kernel(q_ref, kc_ref, vc_ref, wcache_ref, wnew_ref, ocmp_ref, owin_ref, idx_ref, *,
                        nblk, nblk_pad, past, w_buf):
    hd = NSA_HD
    q = q_ref[0]
    groups = _head_group_mask(NSA_HEADS)
    slope = _slope_column(NSA_HEADS)
    q_pos = past
    cur = q_pos // NSA_BLOCK
    blk = lax.broadcasted_iota(jnp.int32, (1, nblk_pad), 1)
    d_cmp = q_pos - ((blk + 1) * NSA_BLOCK - 1)
    s = jnp.zeros((NSA_HEADS, nblk_pad), F32)
    for g in range(NSA_KV_HEADS):
        s = s + jnp.where(groups[g], _dot_nt(q, kc_ref[0, :, g * hd:(g + 1) * hd]), 0.0)
    p = _masked_softmax(s - slope * d_cmp.astype(F32), (d_cmp >= 0) & (blk < nblk))
    o = jnp.zeros((NSA_HEADS, hd), F32)
    for g in range(NSA_KV_HEADS):
        o = o + jnp.where(groups[g], _dot(p, vc_ref[0, :, g * hd:(g + 1) * hd]), 0.0)
    ocmp_ref[0] = o
    gsel = (lax.broadcasted_iota(jnp.int32, (8, NSA_HEADS), 1) // NSA_GROUP
            == lax.broadcasted_iota(jnp.int32, (8, NSA_HEADS), 0)).astype(F32)
    score = _dot(gsel, p, HIGHEST)
    forced = (blk == 0) | (blk == cur) | (blk == cur - 1)
    score = jnp.where(forced, jnp.inf, score)
    score = jnp.where((blk <= cur) & (blk < nblk), score, -jnp.inf)
    blk_f = blk.astype(F32)
    lane_k = lax.broadcasted_iota(jnp.int32, (1, NSA_TOPK), 1)
    idx = jnp.zeros((8, NSA_TOPK), F32)
    for k in range(NSA_TOPK):
        m = jnp.max(score, axis=-1, keepdims=True)
        first = jnp.min(jnp.where(score == m, blk_f, 1e9), axis=-1, keepdims=True)
        idx = jnp.where(lane_k == k, first, idx)
        score = jnp.where(blk_f == first, -jnp.inf, score)
    idx_ref[0] = idx.astype(jnp.int32)
    i_buf = lax.broadcasted_iota(jnp.int32, (1, w_buf), 1)
    win_pos = past - w_buf + i_buf
    d_win = q_pos - win_pos
    m_win = (d_win >= 0) & (d_win < NSA_WINDOW) & (win_pos >= 0)
    s = jnp.zeros((NSA_HEADS, w_buf), F32)
    k_new = jnp.zeros((NSA_HEADS, hd), F32)
    v_new = jnp.zeros((NSA_HEADS, hd), F32)
    for g in range(NSA_KV_HEADS):
        s = s + jnp.where(groups[g], _dot_nt(q, wcache_ref[0, 0, :, g * hd:(g + 1) * hd]), 0.0)
        k_new = k_new + jnp.where(groups[g], wnew_ref[0, :, g * hd:(g + 1) * hd], 0.0)
        v_new = v_new + jnp.where(groups[g], wnew_ref[0, :, NSA_KV_DIM + g * hd:NSA_KV_DIM + (g + 1) * hd], 0.0)
    s = jnp.where(m_win, s - slope * d_win.astype(F32), -jnp.inf)
    s_new = jnp.sum(q * k_new, axis=-1, keepdims=True)
    m = jnp.maximum(jnp.max(s, axis=-1, keepdims=True), s_new)
    e = jnp.where(m_win, jnp.exp(s - m), 0.0)
    e_new = jnp.exp(s_new - m)
    denom = jnp.maximum(jnp.sum(e, axis=-1, keepdims=True) + e_new, 1e-30)
    o = e_new * v_new
    for g in range(NSA_KV_HEADS):
        o = o + jnp.where(groups[g], _dot(e, wcache_ref[0, 0, :, NSA_KV_DIM + g * hd:NSA_KV_DIM + (g + 1) * hd]), 0.0)
    owin_ref[0] = o / denom


def nsa_sample_attention1(q, k_cmp, v_cmp, cache_win, layer, win_new, nblk, past):
    nbatch = q.shape[0]
    nblk_pad = k_cmp.shape[1]
    w_buf = cache_win.shape[2]
    hq = pl.BlockSpec((1, NSA_HEADS, NSA_HD), lambda b: (b, 0, 0))
    return pl.pallas_call(
        functools.partial(_nsa_sample1_kernel, nblk=nblk, nblk_pad=nblk_pad, past=past, w_buf=w_buf),
        grid=(nbatch,),
        in_specs=[hq, pl.BlockSpec((1, nblk_pad, NSA_KV_DIM), lambda b: (b, 0, 0)),
                  pl.BlockSpec((1, nblk_pad, NSA_KV_DIM), lambda b: (b, 0, 0)),
                  pl.BlockSpec((1, 1, w_buf, 2 * NSA_KV_DIM), lambda b: (layer, b, 0, 0)),
                  pl.BlockSpec((1, 1, 2 * NSA_KV_DIM), lambda b: (b, 0, 0))],
        out_specs=[hq, hq, pl.BlockSpec((1, 8, NSA_TOPK), lambda b: (b, 0, 0))],
        out_shape=[jax.ShapeDtypeStruct((nbatch, NSA_HEADS, NSA_HD), F32),
                   jax.ShapeDtypeStruct((nbatch, NSA_HEADS, NSA_HD), F32),
                   jax.ShapeDtypeStruct((nbatch, 8, NSA_TOPK), jnp.int32)],
        compiler_params=_params(("parallel",)),
        name="nsa_sample_attention1",
    )(q, k_cmp, v_cmp, cache_win, win_new)


def _nsa_sample2_kernel(idx_ref, pt_ref, q_ref, b0_ref, b1_ref, b2_ref, b3_ref, new_ref, ocmp_ref, owin_ref, gates_ref,
                        o_ref, m_scr, l_scr, acc_scr, *, past, n_cache_blk):
    hd = NSA_HD
    b = pl.program_id(0)
    k = pl.program_id(1)
    blocks = (b0_ref, b1_ref, b2_ref, b3_ref)

    @pl.when(k == 0)
    def _():
        m_scr[...] = jnp.full(m_scr.shape, -jnp.inf, F32)
        l_scr[...] = jnp.zeros(l_scr.shape, F32)
        acc_scr[...] = jnp.zeros(acc_scr.shape, F32)

    q_pos = past
    lane = lax.broadcasted_iota(jnp.int32, (1, NSA_BLOCK), 1)
    for g in range(NSA_KV_HEADS):
        j = idx_ref[(b * 8 + g) * NSA_TOPK + k]
        slope = _slope_column(8, head0=g * NSA_GROUP)
        q = q_ref[0, g]
        kb = blocks[g][0, 0, :, g * hd:(g + 1) * hd]
        vb = blocks[g][0, 0, :, NSA_KV_DIM + g * hd:NSA_KV_DIM + (g + 1) * hd]
        d = q_pos - (j * NSA_BLOCK + lane)
        valid = (d >= 0) & (j < n_cache_blk)
        s = jnp.where(valid, _dot_nt(q, kb) - slope * d.astype(F32), -jnp.inf)
        k_new = new_ref[0, :, g * hd:(g + 1) * hd]
        v_new = new_ref[0, :, NSA_KV_DIM + g * hd:NSA_KV_DIM + (g + 1) * hd]
        new_ok = j == q_pos // NSA_BLOCK
        s_new = jnp.where(new_ok, jnp.sum(q * k_new, axis=-1, keepdims=True), -jnp.inf)
        m_old = m_scr[g]
        m_new = jnp.maximum(jnp.maximum(m_old, jnp.max(s, axis=-1, keepdims=True)), s_new)
        m_safe = jnp.where(m_new > -jnp.inf, m_new, 0.0)
        alpha = jnp.exp(m_old - m_safe)
        p = jnp.where(valid, jnp.exp(s - m_safe), 0.0)
        p_new = jnp.where(new_ok, jnp.exp(s_new - m_safe), 0.0)
        l_scr[g] = alpha * l_scr[g] + jnp.sum(p, axis=-1, keepdims=True) + p_new
        acc_scr[g] = alpha * acc_scr[g] + _dot(p, vb) + p_new * v_new
        m_scr[g] = m_new

    @pl.when(k == NSA_TOPK - 1)
    def _():
        for g in range(NSA_KV_HEADS):
            o_sel = acc_scr[g] / jnp.maximum(l_scr[g], 1e-30)
            gt = gates_ref[0, g]
            o_ref[0, g] = gt[:, 0:1] * ocmp_ref[0, g] + gt[:, 1:2] * o_sel + gt[:, 2:3] * owin_ref[0, g]


def nsa_sample_attention2(idx, page_table, q8, cache_sel, layer, sel_new, o_cmp8, o_win8, gates8, past):
    nbatch = q8.shape[0]
    n_cache_blk = past // NSA_BLOCK
    cache_h = cache_sel.reshape(cache_sel.shape[0], cache_sel.shape[1] * BLOCKS_PER_PAGE, NSA_BLOCK, 2 * NSA_KV_DIM)

    def blk_map(g):
        def index_map(b, k, idx_ref, pt_ref):
            j = jnp.minimum(idx_ref[(b * 8 + g) * NSA_TOPK + k], n_cache_blk - 1)
            return (layer, pt_ref[b, j // BLOCKS_PER_PAGE] * BLOCKS_PER_PAGE + j % BLOCKS_PER_PAGE, 0, 0)
        return index_map

    g4 = lambda w: pl.BlockSpec((1, NSA_KV_HEADS, 8, w), lambda b, k, i, p: (b, 0, 0, 0))
    grid_spec = pltpu.PrefetchScalarGridSpec(
        num_scalar_prefetch=2,
        grid=(nbatch, NSA_TOPK),
        in_specs=[g4(NSA_HD)] + [pl.BlockSpec((1, 1, NSA_BLOCK, 2 * NSA_KV_DIM), blk_map(g)) for g in range(NSA_KV_HEADS)]
                 + [pl.BlockSpec((1, 1, 2 * NSA_KV_DIM), lambda b, k, i, p: (b, 0, 0)), g4(NSA_HD), g4(NSA_HD), g4(LANES)],
        out_specs=g4(NSA_HD),
        scratch_shapes=[pltpu.VMEM((NSA_KV_HEADS, 8, 1), F32), pltpu.VMEM((NSA_KV_HEADS, 8, 1), F32),
                        pltpu.VMEM((NSA_KV_HEADS, 8, NSA_HD), F32)],
    )
    return pl.pallas_call(
        functools.partial(_nsa_sample2_kernel, past=past, n_cache_blk=n_cache_blk),
        grid_spec=grid_spec,
        out_shape=jax.ShapeDtypeStruct((nbatch, NSA_KV_HEADS, 8, NSA_HD), F32),
        compiler_params=_params(("parallel", "arbitrary")),
        name="nsa_sample_attention2",
    )(idx, page_table, q8, cache_h, cache_h, cache_h, cache_h, sel_new, o_cmp8, o_win8, gates8)


GLA_Q0, GLA_K0, GLA_V0 = 0, GLA_QK_DIM, 2 * GLA_QK_DIM
GLA_A0 = GLA_V0 + GLA_V_DIM
GLA_G0 = GLA_A0 + GLA_A_W


def _gla_kernel(z_ref, s0_ref, a2_ref, ab_ref, ng_ref, o_ref, st_ref, s_scr, *, tt, t_valid, n_t):
    ti = pl.program_id(1)
    c = GLA_CHUNK

    @pl.when(ti == 0)
    def _():
        for h in range(GLA_HEADS):
            s_scr[h] = s0_ref[0, h].T

    a_lr = z_ref[0, :, GLA_A0:GLA_A0 + GLA_A_W].astype(BF16)
    pre = _dot(a_lr, a2_ref[...]) + ab_ref[...]
    log_a = (jnp.minimum(pre, 0.0) - jnp.log(1.0 + jnp.exp(-jnp.abs(pre)))) / GLA_TAU
    row = ti * tt + lax.broadcasted_iota(jnp.int32, (tt, 1), 0)
    log_a = jnp.where(row < t_valid, log_a, 0.0)
    tri = lax.broadcasted_iota(jnp.int32, (c, c), 0) >= lax.broadcasted_iota(jnp.int32, (c, c), 1)
    tri_f = tri.astype(F32)
    for ci in range(tt // c):
        rows = slice(ci * c, (ci + 1) * c)
        for h in range(GLA_HEADS):
            ks = slice(h * GLA_DK, (h + 1) * GLA_DK)
            vs = slice(h * GLA_DV, (h + 1) * GLA_DV)
            cum = _dot(tri_f, log_a[rows, ks], HIGHEST)
            last = cum[c - 1:c, :]
            q_c = z_ref[0, rows, GLA_Q0 + h * GLA_DK:GLA_Q0 + (h + 1) * GLA_DK] * (GLA_DK ** -0.5)
            k_c = z_ref[0, rows, GLA_K0 + h * GLA_DK:GLA_K0 + (h + 1) * GLA_DK]
            v_c = z_ref[0, rows, GLA_V0 + h * GLA_DV:GLA_V0 + (h + 1) * GLA_DV].astype(BF16)
            q_dec = (q_c * jnp.exp(cum)).astype(BF16)
            k_inv = (k_c * jnp.exp(-cum)).astype(BF16)
            k_dec = (k_c * jnp.exp(last - cum)).astype(BF16)
            att = jnp.where(tri, _dot_nt(q_dec, k_inv), 0.0)
            st = s_scr[h]
            o = _dot(att.astype(BF16), v_c) + _dot_nt(q_dec, st.astype(BF16))
            s_scr[h] = st * jnp.exp(last) + _dot_tn(v_c, k_dec)
            o = o * lax.rsqrt(jnp.mean(o * o, axis=-1, keepdims=True) + NORM_EPS) * ng_ref[...]
            gate = z_ref[0, rows, GLA_G0 + h * GLA_DV:GLA_G0 + (h + 1) * GLA_DV]
            o_ref[0, rows, vs] = (o * (gate * _sigmoid(gate))).astype(o_ref.dtype)

    @pl.when(ti == n_t - 1)
    def _():
        for h in range(GLA_HEADS):
            st_ref[0, h] = s_scr[h].T


def gla_mixer(z_gla, s0, a2p, ab, ng, t_valid):
    b, t, _ = z_gla.shape
    tt = min(t, 512)
    n_t = t // tt
    return pl.pallas_call(
        functools.partial(_gla_kernel, tt=tt, t_valid=t_valid, n_t=n_t),
        grid=(b, n_t),
        in_specs=[pl.BlockSpec((1, tt, GLA_W), lambda i, j: (i, j, 0)),
                  pl.BlockSpec((1, GLA_HEADS, GLA_DK, GLA_DV), lambda i, j: (i, 0, 0, 0)),
                  pl.BlockSpec((GLA_A_W, GLA_QK_DIM), lambda i, j: (0, 0)),
                  pl.BlockSpec((1, GLA_QK_DIM), lambda i, j: (0, 0)),
                  pl.BlockSpec((1, GLA_DV), lambda i, j: (0, 0))],
        out_specs=[pl.BlockSpec((1, tt, GLA_V_DIM), lambda i, j: (i, j, 0)),
                   pl.BlockSpec((1, GLA_HEADS, GLA_DK, GLA_DV), lambda i, j: (i, 0, 0, 0))],
        out_shape=[jax.ShapeDtypeStruct((b, t, GLA_V_DIM), BF16),
                   jax.ShapeDtypeStruct((b, GLA_HEADS, GLA_DK, GLA_DV), F32)],
        scratch_shapes=[pltpu.VMEM((GLA_HEADS, GLA_DV, GLA_DK), F32)],
        compiler_params=_params(("parallel", "arbitrary")),
        name="gla_mixer",
    )(z_gla, s0, a2p, ab, ng)


RW_R0, RW_K0, RW_V0 = 0, RW_DIM, 2 * RW_DIM
RW_WL0 = 3 * RW_DIM
RW_AL0 = RW_WL0 + RW_LR_W
RW_GL0 = RW_AL0 + RW_LR_W


def _rw_prep_kernel(z_ref, sh_ref, mu_ref, w0_ref, w2_ref, a0_ref, a2_ref, g2_ref, kk_ref, ka_ref, rk_ref, seg_ref,
                    r_ref, dec_ref, k_ref, v_ref, nkk_ref, kka_ref, g_ref, bonus_ref):
    z = z_ref[...]
    xm = z + (sh_ref[...] - z) * mu_ref[...]
    r = xm[:, RW_R0:RW_R0 + RW_DIM]
    k = xm[:, RW_K0:RW_K0 + RW_DIM]
    v = xm[:, RW_V0:RW_V0 + RW_DIM]
    wl = xm[:, RW_WL0:RW_WL0 + RW_LR_W]
    al = xm[:, RW_AL0:RW_AL0 + RW_LR_W]
    gl = xm[:, RW_GL0:RW_GL0 + RW_LR_W]
    w = -_softplus(-(w0_ref[...] + _dot(jnp.tanh(wl).astype(BF16), w2_ref[...]))) - 0.5
    a = _sigmoid(a0_ref[...] + _dot(al.astype(BF16), a2_ref[...]))
    kk = k * kk_ref[...]
    norm = jnp.sqrt(_dot(kk * kk, seg_ref[...], HIGHEST))
    kk = kk / jnp.maximum(norm, 1e-12)
    k2 = k * (1.0 + (a - 1.0) * ka_ref[...])
    r_ref[...] = r
    dec_ref[...] = jnp.exp(-jnp.exp(w))
    k_ref[...] = k2
    v_ref[...] = v
    nkk_ref[...] = -kk
    kka_ref[...] = kk * a
    g_ref[...] = _dot(_sigmoid(gl).astype(BF16), g2_ref[...])
    bonus_ref[...] = _dot(r * k2 * rk_ref[...], seg_ref[...], HIGHEST) * v


def rw_prep(z_rw, shifted, rp, tm=512):
    m = z_rw.shape[0]
    tm = min(tm, m)
    row = lambda w: pl.BlockSpec((tm, w), lambda i: (i, 0))
    const = lambda r, w: pl.BlockSpec((r, w), lambda i: (0, 0))
    out = jax.ShapeDtypeStruct((m, RW_DIM), F32)
    return pl.pallas_call(
        _rw_prep_kernel,
        grid=(m // tm,),
        in_specs=[row(RW_W), row(RW_W), const(1, RW_W), const(1, RW_DIM), const(RW_LR_W, RW_DIM), const(1, RW_DIM),
                  const(RW_LR_W, RW_DIM), const(RW_LR_W, RW_DIM), const(1, RW_DIM), const(1, RW_DIM), const(1, RW_DIM),
                  const(RW_DIM, RW_DIM)],
        out_specs=[row(RW_DIM)] * 8,
        out_shape=[out] * 8,
        compiler_params=_params(("parallel",)),
        name="rw_prep",
    )(z_rw, shifted, rp["mu"], rp["w0"], rp["w2"], rp["a0"], rp["a2"], rp["g2"], rp["kk"], rp["ka"], rp["rk"], rp["seg"])


RW_SUB = 8


def _rw_scan_kernel(r_ref, dec_ref, k_ref, v_ref, nkk_ref, kka_ref, g_ref, bonus_ref, lng_ref, lnb_ref, s0_ref,
                    o_ref, st_ref, s_scr, *, tt, n_t):
    ti = pl.program_id(1)

    @pl.when(ti == 0)
    def _():
        s_scr[...] = s0_ref[0]

    lane = lax.broadcasted_iota(jnp.int32, (1, RW_SUB), 1)

    def step(i, carry):
        rows = pl.ds(pl.multiple_of(i * RW_SUB, RW_SUB), RW_SUB)
        r8, dec8, k8, v8 = r_ref[0, rows, :], dec_ref[0, rows, :], k_ref[0, rows, :], v_ref[0, rows, :]
        nkk8, kka8, g8, bonus8 = nkk_ref[0, rows, :], kka_ref[0, rows, :], g_ref[0, rows, :], bonus_ref[0, rows, :]
        for h in range(RW_HEADS):
            hs = slice(h * RW_HD, (h + 1) * RW_HD)
            s = s_scr[h]
            v_t = v8[:, hs].T
            o_t = jnp.zeros((RW_HD, RW_SUB), F32)
            for j in range(RW_SUB):
                sa = jnp.sum(s * nkk8[j:j + 1, hs], axis=1, keepdims=True)
                s = s * dec8[j:j + 1, hs] + sa * kka8[j:j + 1, hs] + v_t[:, j:j + 1] * k8[j:j + 1, hs]
                o_col = jnp.sum(s * r8[j:j + 1, hs], axis=1, keepdims=True)
                o_t = jnp.where(lane == j, o_col, o_t)
            s_scr[h] = s
            o8 = o_t.T
            mean = jnp.mean(o8, axis=-1, keepdims=True)
            cen = o8 - mean
            var = jnp.mean(cen * cen, axis=-1, keepdims=True)
            y = cen * lax.rsqrt(var + RW_GN_EPS) * lng_ref[:, hs] + lnb_ref[:, hs]
            o_ref[0, rows, hs] = (y + bonus8[:, hs]) * g8[:, hs]
        return carry

    lax.fori_loop(0, tt // RW_SUB, step, 0)

    @pl.when(ti == n_t - 1)
    def _():
        st_ref[0] = s_scr[...]


def rw_scan(r, dec, k, v, nkk, kka, g, bonus, ln_g, ln_b, s0):
    b, t, _ = r.shape
    tt = min(t, 256)
    n_t = t // tt
    tile = pl.BlockSpec((1, tt, RW_DIM), lambda i, j: (i, j, 0))
    vec = pl.BlockSpec((1, RW_DIM), lambda i, j: (0, 0))
    st = pl.BlockSpec((1, RW_HEADS, RW_HD, RW_HD), lambda i, j: (i, 0, 0, 0))
    return pl.pallas_call(
        functools.partial(_rw_scan_kernel, tt=tt, n_t=n_t),
        grid=(b, n_t),
        in_specs=[tile] * 8 + [vec, vec, st],
        out_specs=[tile, st],
        out_shape=[jax.ShapeDtypeStruct((b, t, RW_DIM), F32), jax.ShapeDtypeStruct(s0.shape, F32)],
        scratch_shapes=[pltpu.VMEM((RW_HEADS, RW_HD, RW_HD), F32)],
        compiler_params=_params(("parallel", "arbitrary")),
        name="rw_scan",
    )(r, dec, k, v, nkk, kka, g, bonus, ln_g, ln_b, s0)


def _pad_last(w, width):
    return jnp.pad(w, [(0, 0)] * (w.ndim - 1) + [(0, width - w.shape[-1])])


def _pad_rows(w, rows):
    return jnp.pad(w, [(0, rows - w.shape[0])] + [(0, 0)] * (w.ndim - 1))


def _split_last(z, sizes):
    return jnp.split(z, [int(c) for c in np.cumsum(np.array(sizes))[:-1]], axis=-1)


def _rw_padded(cols):
    r, k, v, wl, al, gl = _split_last(cols, RW_SIZES)
    return jnp.concatenate([r, k, v, _pad_last(wl, RW_LR_W), _pad_last(al, RW_LR_W), _pad_last(gl, RW_LR_W)], axis=-1)


def _rw_unpadded(cols):
    return jnp.concatenate([cols[..., :RW_WL0], cols[..., RW_WL0:RW_WL0 + RW_DECAY_RANK],
                            cols[..., RW_AL0:RW_AL0 + RW_A_RANK], cols[..., RW_GL0:RW_GL0 + RW_G_RANK]], axis=-1)


def _layer_params(l, p):
    q, cmp_, sel, win, gate, gq, gk, gv, ga, gg, rw, mg = _split_last(p["w_in"][l], IN_SIZES)
    seg = np.kron(np.eye(RW_HEADS, dtype=np.float32), np.ones((RW_HD, RW_HD), np.float32))
    row = lambda v: v.reshape(1, -1)
    return {
        "norm1_g": p["norm1_g"][l], "norm2_g": p["norm2_g"][l],
        "w_nsa": jnp.concatenate([q, cmp_, sel, win, _pad_last(gate, GATE_W)], axis=1).astype(BF16),
        "w_gla": jnp.concatenate([gq, gk, gv, _pad_last(ga, GLA_A_W), gg], axis=1).astype(BF16),
        "w_rw": _rw_padded(rw).astype(BF16),
        "w_mg": mg.astype(BF16),
        "qk_g": p["nsa_qk_g"][l],
        "cw": {"pos2": jnp.concatenate([p["cmp_pos"][l]] * 2, axis=-1),
               "w1": p["cmp_w1"][l].reshape(2, NSA_BLOCK, NSA_HD, NSA_CMP_HID).astype(BF16),
               "w2": p["cmp_w2"][l].astype(BF16),
               "kg": row(p["nsa_qk_g"][l, 1])},
        "gla_a2": _pad_rows(p["gla_a2"][l], GLA_A_W).astype(BF16),
        "gla_ab": row(p["gla_a_b"][l]), "gla_ng": row(p["gla_norm_g"][l]),
        "rp": {"mu": row(_rw_padded(p["rw_mu"][l])), "w0": row(p["rw_w0"][l]),
               "w2": _pad_rows(p["rw_w2"][l], RW_LR_W).astype(BF16), "a0": row(p["rw_a0"][l]),
               "a2": _pad_rows(p["rw_a2"][l], RW_LR_W).astype(BF16),
               "g2": _pad_rows(p["rw_g2"][l], RW_LR_W).astype(BF16),
               "kk": row(p["rw_kk"][l]), "ka": row(p["rw_ka"][l]), "rk": row(p["rw_rk"][l]), "seg": jnp.asarray(seg)},
        "ln_g": row(p["rw_ln_g"][l]), "ln_b": row(p["rw_ln_b"][l]),
        "nsa_up": p["nsa_up"][l].astype(BF16), "gla_up": p["gla_up"][l].astype(BF16), "rw_up": p["rw_up"][l].astype(BF16),
        "w_out": p["w_out"][l].astype(BF16), "mlp_w1": p["mlp_w1"][l].astype(BF16), "mlp_w2": p["mlp_w2"][l].astype(BF16),
    }


def _project_in(x2, lp, tm):
    hn = rmsnorm_rows(x2, lp["norm1_g"], tm)
    return tuple(matmul(hn, lp[w], tm=tm) for w in ("w_nsa", "w_gla", "w_rw", "w_mg"))


def _project_out(x2, o_nsa, o_gla, o_rw, z_mg, lp, tm):
    merged = merge_branches(o_nsa, o_gla, o_rw, lp["nsa_up"], lp["gla_up"], lp["rw_up"], z_mg, tm=tm)
    x1 = matmul(merged, lp["w_out"], "residual", x2, tm=tm)
    h2 = rmsnorm_rows(x1, lp["norm2_g"], tm)
    u = matmul(h2, lp["mlp_w1"], "relu2", tm=tm)
    return matmul(u, lp["mlp_w2"], "residual", x1, tm=tm)


def _kv_rows(rows, b, t):
    return rows.reshape(b, t, 2, NSA_KV_HEADS, NSA_HD)


def _prompt_layer(x, lp):
    b, t, d = x.shape
    m = b * t
    tm = 512
    x2 = x.reshape(m, d)
    z_nsa, z_gla, z_rw, z_mg = _project_in(x2, lp, tm)
    qn, cmp_rows, sel_rows, win_rows, sel_b, win_b, gates = nsa_prep(z_nsa, lp["qk_g"])
    nblk = t // NSA_BLOCK
    k_cmp, v_cmp = compress_rows(cmp_rows, lp["cw"], nb=32)
    per_b = lambda a: a.reshape(b, t, a.shape[-1])
    o_nsa = nsa_prompt_attention(per_b(qn), per_b(gates), per_b(sel_b), per_b(win_b),
                                 k_cmp.reshape(b, nblk, NSA_KV_DIM), v_cmp.reshape(b, nblk, NSA_KV_DIM))
    o_gla, gla_st = gla_mixer(per_b(z_gla), jnp.zeros((b, GLA_HEADS, GLA_DK, GLA_DV), F32),
                              lp["gla_a2"], lp["gla_ab"], lp["gla_ng"], t_valid=t)
    z_rw3 = per_b(z_rw)
    shifted = jnp.concatenate([jnp.zeros((b, 1, RW_W), F32), z_rw3[:, :-1]], axis=1).reshape(m, RW_W)
    prep = rw_prep(z_rw, shifted, lp["rp"])
    o_rw, rw_st = rw_scan(*(per_b(a) for a in prep), lp["ln_g"], lp["ln_b"],
                          jnp.zeros((b, RW_HEADS, RW_HD, RW_HD), F32))
    y = _project_out(x2, o_nsa.reshape(m, -1), o_gla.reshape(m, -1), o_rw.reshape(m, -1), z_mg, lp, tm)
    w_keep = min(NSA_WINDOW, t)
    return (y.reshape(b, t, d), _kv_rows(cmp_rows, b, t), _kv_rows(sel_rows, b, t), _kv_rows(win_rows, b, t)[:, t - w_keep:],
            gla_st, rw_st, _rw_unpadded(z_rw3[:, -1]))


SAMPLE_ROWS = SUBLANES_BF16


def _sample_layer(x2, lp, layer, caches, page_table, nbatch):
    cache_cmp, cache_sel, cache_win, state_gla, state_rwkv, state_shift = caches
    tm = SAMPLE_ROWS
    past = page_table.shape[1] * PAGE_SIZE
    nblk = -(-(past + 1) // NSA_BLOCK)
    z_nsa, z_gla, z_rw, z_mg = _project_in(x2, lp, tm)
    qn, cmp_rows, sel_rows, win_rows, _, _, gates = nsa_prep(z_nsa, lp["qk_g"])
    kc_past, vc_past = compress_pages(cache_cmp, layer, page_table, lp["cw"])
    last_blk = jnp.pad(cmp_rows[:nbatch, None, :], ((0, 0), (0, NSA_BLOCK - 1), (0, 0))).reshape(nbatch * NSA_BLOCK, -1)
    kc_last, vc_last = compress_rows(last_blk, lp["cw"], nb=nbatch)
    nblk_pad = -(-nblk // 8) * 8
    with_last = lambda past_s, last_s: jnp.pad(jnp.concatenate([past_s, last_s[:, None, :]], axis=1),
                                               ((0, 0), (0, nblk_pad - nblk), (0, 0)))
    q16 = qn[:nbatch].astype(F32).reshape(nbatch, NSA_HEADS, NSA_HD)
    o_cmp, o_win, idx = nsa_sample_attention1(
        q16, with_last(kc_past, kc_last), with_last(vc_past, vc_last),
        cache_win.reshape(cache_win.shape[0], nbatch, cache_win.shape[2], 2 * NSA_KV_DIM), layer,
        win_rows[:nbatch, None, :], nblk, past)
    grp8 = lambda a: jnp.pad(a.reshape(nbatch, NSA_KV_HEADS, NSA_GROUP, a.shape[-1]),
                             ((0, 0), (0, 0), (0, 8 - NSA_GROUP), (0, 0)))
    gates3 = gates[:nbatch, :3 * NSA_HEADS].reshape(nbatch, 3, NSA_HEADS).transpose(0, 2, 1)
    o8 = nsa_sample_attention2(idx.reshape(-1), page_table, grp8(q16), cache_sel, layer, sel_rows[:nbatch, None, :],
                               grp8(o_cmp), grp8(o_win), grp8(_pad_last(gates3, LANES)), past)
    o_nsa = _pad_rows(o8[:, :, :NSA_GROUP].reshape(nbatch, NSA_Q_DIM), tm)
    z_gla_c = jnp.pad(z_gla[:nbatch, None, :], ((0, 0), (0, GLA_CHUNK - 1), (0, 0)))
    o_gla, gla_st = gla_mixer(z_gla_c, state_gla[layer], lp["gla_a2"], lp["gla_ab"], lp["gla_ng"], t_valid=1)
    o_gla = _pad_rows(o_gla[:, 0], tm)
    shifted = _pad_rows(_rw_padded(state_shift[layer]), tm)
    prep = rw_prep(z_rw, shifted, lp["rp"])
    pad_t = lambda a, v: jnp.pad(a[:nbatch, None, :], ((0, 0), (0, RW_SUB - 1), (0, 0)), constant_values=v)
    prep = [pad_t(a, 1.0 if i == 1 else 0.0) for i, a in enumerate(prep)]
    o_rw, rw_st = rw_scan(*prep, lp["ln_g"], lp["ln_b"], state_rwkv[layer])
    o_rw = _pad_rows(o_rw[:, 0], tm)
    y = _project_out(x2, o_nsa, o_gla, o_rw, z_mg, lp, tm)
    return (y, _kv_rows(cmp_rows[:nbatch], nbatch, 1), _kv_rows(sel_rows[:nbatch], nbatch, 1),
            _kv_rows(win_rows[:nbatch], nbatch, 1), gla_st, rw_st, _rw_unpadded(z_rw[:nbatch]))


def kernel(x_prompt, x_sample, cache_cmp_kv, cache_sel_kv, cache_win_kv, state_gla, state_rwkv, state_rwkv_shift,
           page_table, norm1_g, w_in, nsa_qk_g, cmp_pos, cmp_w1, cmp_w2, nsa_up, gla_a2, gla_a_b, gla_norm_g, gla_up,
           rw_mu, rw_w0, rw_w2, rw_a0, rw_a2, rw_g2, rw_kk, rw_ka, rw_rk, rw_ln_g, rw_ln_b, rw_up, w_out, norm2_g,
           mlp_w1, mlp_w2):
    p = dict(norm1_g=norm1_g, w_in=w_in, nsa_qk_g=nsa_qk_g, cmp_pos=cmp_pos, cmp_w1=cmp_w1, cmp_w2=cmp_w2, nsa_up=nsa_up,
             gla_a2=gla_a2, gla_a_b=gla_a_b, gla_norm_g=gla_norm_g, gla_up=gla_up, rw_mu=rw_mu, rw_w0=rw_w0, rw_w2=rw_w2,
             rw_a0=rw_a0, rw_a2=rw_a2, rw_g2=rw_g2, rw_kk=rw_kk, rw_ka=rw_ka, rw_rk=rw_rk, rw_ln_g=rw_ln_g,
             rw_ln_b=rw_ln_b, rw_up=rw_up, w_out=w_out, norm2_g=norm2_g, mlp_w1=mlp_w1, mlp_w2=mlp_w2)
    nbatch = x_sample.shape[0]
    caches = (cache_cmp_kv, cache_sel_kv, cache_win_kv, state_gla, state_rwkv, state_rwkv_shift)
    xp = x_prompt
    xs = _pad_rows(x_sample.reshape(nbatch, D_MODEL), SAMPLE_ROWS)
    outs_p, outs_s = [], []
    for l in range(DEPTH):
        lp = _layer_params(l, p)
        xp, *rest_p = _prompt_layer(xp, lp)
        xs, *rest_s = _sample_layer(xs, lp, l, caches, page_table, nbatch)
        outs_p.append(rest_p)
        outs_s.append(rest_s)
    stack = lambda outs, i: jnp.stack([o[i] for o in outs])
    return ((xp, xs[:nbatch].reshape(x_sample.shape))
            + tuple(stack(outs_p, i) for i in range(6)) + tuple(stack(outs_s, i) for i in range(6)))
```

```python
import functools

import numpy as np
import jax
import jax.numpy as jnp
from jax import lax
from jax.experimental import pallas as pl
from jax.experimental.pallas import tpu as pltpu

F32 = jnp.float32
BF16 = jnp.bfloat16
HIGHEST = lax.Precision.HIGHEST

D_MODEL = 2048
DEPTH = 2
PAGE_SIZE = 128
NSA_HEADS = 16
NSA_KV_HEADS = 4
NSA_GROUP = NSA_HEADS // NSA_KV_HEADS
NSA_HD = 64
NSA_BLOCK = 64
NSA_TOPK = 16
NSA_WINDOW = 512
NSA_CMP_HID = 128
GLA_HEADS = 4
GLA_DK = 64
GLA_DV = 128
GLA_GATE_RANK = 16
GLA_TAU = 16.0
GLA_CHUNK = 64
RW_HEADS = 8
RW_HD = 64
RW_DECAY_RANK = 32
RW_A_RANK = 32
RW_G_RANK = 96
RW_GN_EPS = 64e-5
D_FF = 4 * D_MODEL
NORM_EPS = 1e-6

NSA_Q_DIM = NSA_HEADS * NSA_HD
NSA_KV_DIM = NSA_KV_HEADS * NSA_HD
GLA_QK_DIM = GLA_HEADS * GLA_DK
GLA_V_DIM = GLA_HEADS * GLA_DV
RW_DIM = RW_HEADS * RW_HD
RW_SIZES = (RW_DIM, RW_DIM, RW_DIM, RW_DECAY_RANK, RW_A_RANK, RW_G_RANK)
RW_PROJ = sum(RW_SIZES)
IN_SIZES = (NSA_Q_DIM, 2 * NSA_KV_DIM, 2 * NSA_KV_DIM, 2 * NSA_KV_DIM, 3 * NSA_HEADS,
            GLA_QK_DIM, GLA_QK_DIM, GLA_V_DIM, GLA_GATE_RANK, GLA_V_DIM, RW_PROJ, 3 * D_MODEL)

LANES = 128
SUBLANES_BF16 = 16
VMEM_LIMIT = 56 * 1024 * 1024

GATE_W = LANES
NSA_W = NSA_Q_DIM + 6 * NSA_KV_DIM + GATE_W
GLA_A_W = LANES
GLA_W = 2 * GLA_QK_DIM + 2 * GLA_V_DIM + GLA_A_W
RW_LR_W = LANES
RW_W = 3 * RW_DIM + 3 * RW_LR_W
MERGE_W = 3 * D_MODEL


def _params(sem):
    return pltpu.CompilerParams(dimension_semantics=sem, vmem_limit_bytes=VMEM_LIMIT)


def _largest_tile(n, cap):
    best = None
    for t in range(LANES, min(n, cap) + 1, LANES):
        if n % t == 0:
            best = t
    return best if best is not None else n


def _sigmoid(x):
    return 1.0 / (1.0 + jnp.exp(-x))


def _softplus(y):
    return jnp.maximum(y, 0.0) + jnp.log(1.0 + jnp.exp(-jnp.abs(y)))


def _masked_softmax(s, mask):
    s = jnp.where(mask, s, -jnp.inf)
    m = jnp.max(s, axis=-1, keepdims=True)
    m = jnp.where(m > -jnp.inf, m, 0.0)
    e = jnp.where(mask, jnp.exp(s - m), 0.0)
    return e / jnp.maximum(jnp.sum(e, axis=-1, keepdims=True), 1e-30)


def _dot(a, b, precision=None):
    return jnp.dot(a, b, preferred_element_type=F32, precision=precision)


def _dot_nt(a, b, precision=None):
    return lax.dot_general(a, b, (((1,), (1,)), ((), ())), preferred_element_type=F32, precision=precision)


def _dot_tn(a, b, precision=None):
    return lax.dot_general(a, b, (((0,), (0,)), ((), ())), preferred_element_type=F32, precision=precision)


def _rmsnorm_kernel(x_ref, g_ref, o_ref):
    x = x_ref[...]
    y = x * lax.rsqrt(jnp.mean(x * x, axis=-1, keepdims=True) + NORM_EPS) * g_ref[...]
    o_ref[...] = y.astype(o_ref.dtype)


def rmsnorm_rows(x, g, tm):
    m, d = x.shape
    return pl.pallas_call(
        _rmsnorm_kernel,
        grid=(m // tm,),
        in_specs=[pl.BlockSpec((tm, d), lambda i: (i, 0)), pl.BlockSpec((1, d), lambda i: (0, 0))],
        out_specs=pl.BlockSpec((tm, d), lambda i: (i, 0)),
        out_shape=jax.ShapeDtypeStruct((m, d), BF16),
        compiler_params=_params(("parallel",)),
        name="rmsnorm_rows",
    )(x, g.reshape(1, d))


def _mm_kernel(*refs, nk, mode):
    if mode == "residual":
        a_ref, w_ref, r_ref, o_ref = refs[:4]
        scratch = refs[4:]
    else:
        a_ref, w_ref, o_ref = refs[:3]
        scratch = refs[3:]
    part = _dot(a_ref[...], w_ref[...])

    def finish(acc):
        if mode == "relu2":
            acc = jnp.square(jnp.maximum(acc, 0.0))
        elif mode == "residual":
            acc = r_ref[...] + acc
        o_ref[...] = acc.astype(o_ref.dtype)

    if nk == 1:
        finish(part)
    else:
        acc_ref = scratch[0]
        k = pl.program_id(2)

        @pl.when(k == 0)
        def _():
            acc_ref[...] = part

        @pl.when(k > 0)
        def _():
            acc_ref[...] += part

        @pl.when(k == nk - 1)
        def _():
            finish(acc_ref[...])


def matmul(a, w, mode="plain", residual=None, tm=512):
    m, kdim = a.shape
    n = w.shape[1]
    tm = min(tm, m)
    tk = min(kdim, 2048)
    nk = kdim // tk
    tn = _largest_tile(n, 1024 if (mode == "residual" or nk > 1) else 2048)
    in_specs = [pl.BlockSpec((tm, tk), lambda i, j, k: (i, k)), pl.BlockSpec((tk, tn), lambda i, j, k: (k, j))]
    args = [a, w]
    if mode == "residual":
        in_specs.append(pl.BlockSpec((tm, tn), lambda i, j, k: (i, j)))
        args.append(residual)
    out_dtype = BF16 if mode == "relu2" else F32
    return pl.pallas_call(
        functools.partial(_mm_kernel, nk=nk, mode=mode),
        grid=(m // tm, n // tn, nk),
        in_specs=in_specs,
        out_specs=pl.BlockSpec((tm, tn), lambda i, j, k: (i, j)),
        out_shape=jax.ShapeDtypeStruct((m, n), out_dtype),
        scratch_shapes=[pltpu.VMEM((tm, tn), F32)] if nk > 1 else [],
        compiler_params=_params(("parallel", "parallel", "arbitrary")),
        name="matmul_" + mode,
    )(*args)


def _merge_kernel(on_ref, og_ref, or_ref, wn_ref, wg_ref, wr_ref, m0_ref, m1_ref, m2_ref, o_ref):
    acc = _sigmoid(m0_ref[...]) * _dot(on_ref[...].astype(BF16), wn_ref[...])
    acc += _sigmoid(m1_ref[...]) * _dot(og_ref[...].astype(BF16), wg_ref[...])
    acc += _sigmoid(m2_ref[...]) * _dot(or_ref[...].astype(BF16), wr_ref[...])
    o_ref[...] = acc.astype(o_ref.dtype)


def merge_branches(o_nsa, o_gla, o_rw, w_nsa, w_gla, w_rw, z_merge, tm=512, tn=1024):
    m = o_nsa.shape[0]
    tm = min(tm, m)
    nj = D_MODEL // tn
    row = lambda kd: pl.BlockSpec((tm, kd), lambda i, j: (i, 0))
    wsp = lambda kd: pl.BlockSpec((kd, tn), lambda i, j: (0, j))
    gate = lambda b: pl.BlockSpec((tm, tn), lambda i, j: (i, j + b * nj))
    return pl.pallas_call(
        _merge_kernel,
        grid=(m // tm, nj),
        in_specs=[row(NSA_Q_DIM), row(GLA_V_DIM), row(RW_DIM), wsp(NSA_Q_DIM), wsp(GLA_V_DIM), wsp(RW_DIM),
                  gate(0), gate(1), gate(2)],
        out_specs=pl.BlockSpec((tm, tn), lambda i, j: (i, j)),
        out_shape=jax.ShapeDtypeStruct((m, D_MODEL), BF16),
        compiler_params=_params(("parallel", "parallel")),
        name="merge_branches",
    )(o_nsa, o_gla, o_rw, w_nsa, w_gla, w_rw, z_merge, z_merge, z_merge)


def _head_rmsnorm(x, g):
    return x * lax.rsqrt(jnp.mean(x * x, axis=-1, keepdims=True) + NORM_EPS) * g


def _nsa_prep_kernel(z_ref, g_ref, qn_ref, cmp_ref, sel_ref, win_ref, selb_ref, winb_ref, gates_ref):
    hd = NSA_HD
    g_q, g_sel, g_win = g_ref[0:1, :], g_ref[2:3, :], g_ref[3:4, :]
    scale = NSA_HD ** -0.5
    for h in range(NSA_HEADS):
        q = _head_rmsnorm(z_ref[:, h * hd:(h + 1) * hd], g_q)
        qn_ref[:, h * hd:(h + 1) * hd] = (q * scale).astype(BF16)
    c0 = NSA_Q_DIM
    cmp_ref[...] = z_ref[:, c0:c0 + 2 * NSA_KV_DIM]
    for (off, gain, o_ref, ob_ref) in ((c0 + 2 * NSA_KV_DIM, g_sel, sel_ref, selb_ref),
                                       (c0 + 4 * NSA_KV_DIM, g_win, win_ref, winb_ref)):
        for h in range(NSA_KV_HEADS):
            k = _head_rmsnorm(z_ref[:, off + h * hd:off + (h + 1) * hd], gain)
            o_ref[:, h * hd:(h + 1) * hd] = k
            ob_ref[:, h * hd:(h + 1) * hd] = k.astype(BF16)
        v = z_ref[:, off + NSA_KV_DIM:off + 2 * NSA_KV_DIM]
        o_ref[:, NSA_KV_DIM:] = v
        ob_ref[:, NSA_KV_DIM:] = v.astype(BF16)
    gates_ref[...] = _sigmoid(z_ref[:, c0 + 6 * NSA_KV_DIM:])


def nsa_prep(z_nsa, qk_g, tm=256):
    m = z_nsa.shape[0]
    tm = min(tm, m)
    kvw = 2 * NSA_KV_DIM
    row = lambda w: pl.BlockSpec((tm, w), lambda i: (i, 0))
    return pl.pallas_call(
        _nsa_prep_kernel,
        grid=(m // tm,),
        in_specs=[row(NSA_W), pl.BlockSpec((4, NSA_HD), lambda i: (0, 0))],
        out_specs=[row(NSA_Q_DIM), row(kvw), row(kvw), row(kvw), row(kvw), row(kvw), row(GATE_W)],
        out_shape=[jax.ShapeDtypeStruct((m, NSA_Q_DIM), BF16),
                   jax.ShapeDtypeStruct((m, kvw), F32), jax.ShapeDtypeStruct((m, kvw), F32),
                   jax.ShapeDtypeStruct((m, kvw), F32), jax.ShapeDtypeStruct((m, kvw), BF16),
                   jax.ShapeDtypeStruct((m, kvw), BF16), jax.ShapeDtypeStruct((m, GATE_W), F32)],
        compiler_params=_params(("parallel",)),
        name="nsa_prep",
    )(z_nsa, qk_g)


QUARTERS = 2 * NSA_KV_DIM // LANES


def _compress_compute(x_ref, nb, pos_ref, w1_ref, w2_ref, kg_ref, kc_ref, vc_ref, acc_ref):
    acc_ref[...] = jnp.zeros_like(acc_ref)
    row_stride = NSA_BLOCK * QUARTERS

    def body(l, carry):
        for q in range(QUARTERS):
            kv = q // 2
            x = x_ref[pl.ds(l * QUARTERS + q, nb, stride=row_stride), :] + pos_ref[kv, pl.ds(l, 1), :]
            for half in range(2):
                acc_ref[q * 2 + half] += _dot(x[:, half * NSA_HD:(half + 1) * NSA_HD].astype(BF16), w1_ref[kv, l])
        return carry

    lax.fori_loop(0, NSA_BLOCK, body, 0)
    for q in range(QUARTERS):
        kv = q // 2
        for half in range(2):
            g = (q % 2) * 2 + half
            hid = jax.nn.gelu(acc_ref[q * 2 + half], approximate=True)
            s = _dot(hid.astype(BF16), w2_ref[kv])
            if kv == 0:
                kc_ref[:, g * NSA_HD:(g + 1) * NSA_HD] = _head_rmsnorm(s, kg_ref[...])
            else:
                vc_ref[:, g * NSA_HD:(g + 1) * NSA_HD] = s


def _compress_kernel(x_ref, pos_ref, w1_ref, w2_ref, kg_ref, kc_ref, vc_ref, acc_ref, *, nb):
    _compress_compute(x_ref, nb, pos_ref, w1_ref, w2_ref, kg_ref, kc_ref, vc_ref, acc_ref)


def _cmp_weight_specs(nidx):
    zero = lambda n: (lambda *idx: (0,) * n)
    return [pl.BlockSpec((2, NSA_BLOCK, LANES), zero(3)),
            pl.BlockSpec((2, NSA_BLOCK, NSA_HD, NSA_CMP_HID), zero(4)),
            pl.BlockSpec((2, NSA_CMP_HID, NSA_HD), zero(3)),
            pl.BlockSpec((1, NSA_HD), zero(2))]


def compress_rows(rows, cw, nb):
    nblk = rows.shape[0] // NSA_BLOCK
    nb = min(nb, nblk)
    x4 = rows.reshape(nblk * NSA_BLOCK * QUARTERS, LANES)
    out = jax.ShapeDtypeStruct((nblk, NSA_KV_DIM), F32)
    return pl.pallas_call(
        functools.partial(_compress_kernel, nb=nb),
        grid=(nblk // nb,),
        in_specs=[pl.BlockSpec((nb * NSA_BLOCK * QUARTERS, LANES), lambda i: (i, 0))] + _cmp_weight_specs(1),
        out_specs=[pl.BlockSpec((nb, NSA_KV_DIM), lambda i: (i, 0))] * 2,
        out_shape=[out, out],
        scratch_shapes=[pltpu.VMEM((2 * QUARTERS, nb, NSA_CMP_HID), F32)],
        compiler_params=_params(("parallel",)),
        name="compress_rows",
    )(x4, cw["pos2"], cw["w1"], cw["w2"], cw["kg"])


PAGES_PER_STEP = 32
BLOCKS_PER_PAGE = PAGE_SIZE // NSA_BLOCK


def _compress_pages_kernel(pt_ref, *refs, npg):
    page_refs = refs[:npg]
    pos_ref, w1_ref, w2_ref, kg_ref, kc_ref, vc_ref, x_ref, acc_ref = refs[npg:]
    rows = PAGE_SIZE * QUARTERS
    for p in range(npg):
        x_ref[p * rows:(p + 1) * rows, :] = page_refs[p][0, 0]
    _compress_compute(x_ref, npg * BLOCKS_PER_PAGE, pos_ref, w1_ref, w2_ref, kg_ref, kc_ref.at[0], vc_ref.at[0], acc_ref)


def compress_pages(cache, layer, page_table, cw):
    nbatch, n_pages = page_table.shape
    npg = min(PAGES_PER_STEP, n_pages)
    nb = npg * BLOCKS_PER_PAGE
    cache4 = cache.reshape(cache.shape[0], cache.shape[1], PAGE_SIZE * QUARTERS, LANES)
    page_spec = lambda p: pl.BlockSpec((1, 1, PAGE_SIZE * QUARTERS, LANES),
                                       lambda b, c, pt: (layer, pt[b, c * npg + p], 0, 0))
    wspecs = _cmp_weight_specs(3)
    out = jax.ShapeDtypeStruct((nbatch, n_pages * BLOCKS_PER_PAGE, NSA_KV_DIM), F32)
    grid_spec = pltpu.PrefetchScalarGridSpec(
        num_scalar_prefetch=1,
        grid=(nbatch, n_pages // npg),
        in_specs=[page_spec(p) for p in range(npg)] + wspecs,
        out_specs=[pl.BlockSpec((1, nb, NSA_KV_DIM), lambda b, c, pt: (b, c, 0))] * 2,
        scratch_shapes=[pltpu.VMEM((npg * PAGE_SIZE * QUARTERS, LANES), F32),
                        pltpu.VMEM((2 * QUARTERS, nb, NSA_CMP_HID), F32)],
    )
    return pl.pallas_call(
        functools.partial(_compress_pages_kernel, npg=npg),
        grid_spec=grid_spec,
        out_shape=[out, out],
        compiler_params=_params(("parallel", "arbitrary")),
        name="compress_pages",
    )(page_table, *([cache4] * npg), cw["pos2"], cw["w1"], cw["w2"], cw["kg"])


def _select_blocks(score, blk, cur, nblk):
    forced = (blk == 0) | (blk == cur) | (blk == cur - 1)
    score = jnp.where(forced, jnp.inf, score)
    score = jnp.where(blk <= cur, score, -jnp.inf)
    rank = jnp.zeros(score.shape, F32)
    for i in range(nblk):
        col = score[:, i:i + 1]
        ahead = (col > score) | ((col == score) & (blk > i))
        rank = rank + jnp.where(ahead, 1.0, 0.0)
    return (rank < float(min(NSA_TOPK, nblk))) & (blk <= cur)


def _alibi_slope(head):
    return 2.0 ** (-8.0 * (head + 1) / NSA_HEADS)


def _nsa_prompt_kernel(q_ref, gates_ref, selb_ref, winb_ref, kc_ref, vc_ref, o_ref, *, tq, t_len, band):
    qi = pl.program_id(1)
    q0 = qi * tq
    nblk = t_len // NSA_BLOCK
    hd = NSA_HD
    q_pos = q0 + lax.broadcasted_iota(jnp.int32, (tq, 1), 0)
    cur = q_pos // NSA_BLOCK
    blk = lax.broadcasted_iota(jnp.int32, (1, nblk), 1)
    d_cmp = q_pos - ((blk + 1) * NSA_BLOCK - 1)
    key_pos = lax.broadcasted_iota(jnp.int32, (1, t_len), 1)
    d_sel = q_pos - key_pos
    expand = (lax.broadcasted_iota(jnp.int32, (nblk, t_len), 1) // NSA_BLOCK
              == lax.broadcasted_iota(jnp.int32, (nblk, t_len), 0)).astype(BF16)
    w0 = jnp.maximum(q0 - NSA_WINDOW, 0)
    w0 = pl.multiple_of(w0, tq)
    win_pos = w0 + lax.broadcasted_iota(jnp.int32, (1, band), 1)
    d_win = q_pos - win_pos
    m_win = (d_win >= 0) & (d_win < NSA_WINDOW)
    d_cmp_f, d_sel_f, d_win_f = d_cmp.astype(F32), d_sel.astype(F32), d_win.astype(F32)
    for g in range(NSA_KV_HEADS):
        kc = kc_ref[0, :, g * hd:(g + 1) * hd].astype(BF16)
        vc = vc_ref[0, :, g * hd:(g + 1) * hd].astype(BF16)
        k_sel = selb_ref[0, :, g * hd:(g + 1) * hd]
        v_sel = selb_ref[0, :, NSA_KV_DIM + g * hd:NSA_KV_DIM + (g + 1) * hd]
        k_win = winb_ref[0, pl.ds(w0, band), g * hd:(g + 1) * hd]
        v_win = winb_ref[0, pl.ds(w0, band), NSA_KV_DIM + g * hd:NSA_KV_DIM + (g + 1) * hd]
        qs, slopes, o_cmp = [], [], []
        score = jnp.zeros((tq, nblk), F32)
        for r in range(NSA_GROUP):
            h = g * NSA_GROUP + r
            q = q_ref[0, :, h * hd:(h + 1) * hd]
            slope = _alibi_slope(h)
            p = _masked_softmax(_dot_nt(q, kc) - slope * d_cmp_f, d_cmp >= 0)
            score = score + p
            o_cmp.append(_dot(p.astype(BF16), vc))
            qs.append(q)
            slopes.append(slope)
        sel = _select_blocks(score, blk, cur, nblk)
        sel_keys = _dot(jnp.where(sel, 1.0, 0.0).astype(BF16), expand) > 0.5
        m_sel = sel_keys & (d_sel >= 0)
        for r in range(NSA_GROUP):
            h = g * NSA_GROUP + r
            p = _masked_softmax(_dot_nt(qs[r], k_sel) - slopes[r] * d_sel_f, m_sel)
            o_sel = _dot(p.astype(BF16), v_sel)
            p = _masked_softmax(_dot_nt(qs[r], k_win) - slopes[r] * d_win_f, m_win)
            o_win = _dot(p.astype(BF16), v_win)
            gate = lambda b: gates_ref[0, :, b * NSA_HEADS + h:b * NSA_HEADS + h + 1]
            o = gate(0) * o_cmp[r] + gate(1) * o_sel + gate(2) * o_win
            o_ref[0, :, h * hd:(h + 1) * hd] = o.astype(o_ref.dtype)


def nsa_prompt_attention(qn, gates, sel_b, win_b, k_cmp, v_cmp, tq=128):
    b, t, _ = qn.shape
    tq = min(tq, t)
    band = min(NSA_WINDOW + tq, t)
    nblk = t // NSA_BLOCK
    full = lambda w, n: pl.BlockSpec((1, n, w), lambda i, j: (i, 0, 0))
    tile = lambda w: pl.BlockSpec((1, tq, w), lambda i, j: (i, j, 0))
    return pl.pallas_call(
        functools.partial(_nsa_prompt_kernel, tq=tq, t_len=t, band=band),
        grid=(b, t // tq),
        in_specs=[tile(NSA_Q_DIM), tile(GATE_W), full(2 * NSA_KV_DIM, t), full(2 * NSA_KV_DIM, t),
                  full(NSA_KV_DIM, nblk), full(NSA_KV_DIM, nblk)],
        out_specs=tile(NSA_Q_DIM),
        out_shape=jax.ShapeDtypeStruct((b, t, NSA_Q_DIM), BF16),
        compiler_params=_params(("parallel", "arbitrary")),
        name="nsa_prompt_attention",
    )(qn, gates, sel_b, win_b, k_cmp, v_cmp)


def _head_group_mask(nrows):
    row = lax.broadcasted_iota(jnp.int32, (nrows, 1), 0)
    return [(row // NSA_GROUP) == g for g in range(NSA_KV_HEADS)]


def _slope_column(nrows, head0=0):
    row = lax.broadcasted_iota(jnp.int32, (nrows, 1), 0)
    col = jnp.zeros((nrows, 1), F32)
    for h in range(NSA_HEADS):
        col = jnp.where(row + head0 == h, 2.0 ** (-8.0 * (h + 1) / NSA_HEADS), col)
    return col


def _nsa_sample1_kernel(q_ref, kc_ref, vc_ref, wcache_ref, wnew_ref, ocmp_ref, owin_ref, idx_ref, *,
                        nblk, nblk_pad, past, w_buf):
    hd = NSA_HD
    q = q_ref[0]
    groups = _head_group_mask(NSA_HEADS)
    slope = _slope_column(NSA_HEADS)
    q_pos = past
    cur = q_pos // NSA_BLOCK
    blk = lax.broadcasted_iota(jnp.int32, (1, nblk_pad), 1)
    d_cmp = q_pos - ((blk + 1) * NSA_BLOCK - 1)
    s = jnp.zeros((NSA_HEADS, nblk_pad), F32)
    for g in range(NSA_KV_HEADS):
        s = s + jnp.where(groups[g], _dot_nt(q, kc_ref[0, :, g * hd:(g + 1) * hd]), 0.0)
    p = _masked_softmax(s - slope * d_cmp.astype(F32), (d_cmp >= 0) & (blk < nblk))
    o = jnp.zeros((NSA_HEADS, hd), F32)
    for g in range(NSA_KV_HEADS):
        o = o + jnp.where(groups[g], _dot(p, vc_ref[0, :, g * hd:(g + 1) * hd]), 0.0)
    ocmp_ref[0] = o
    gsel = (lax.broadcasted_iota(jnp.int32, (8, NSA_HEADS), 1) // NSA_GROUP
            == lax.broadcasted_iota(jnp.int32, (8, NSA_HEADS), 0)).astype(F32)
    score = _dot(gsel, p, HIGHEST)
    forced = (blk == 0) | (blk == cur) | (blk == cur - 1)
    score = jnp.where(forced, jnp.inf, score)
    score = jnp.where((blk <= cur) & (blk < nblk), score, -jnp.inf)
    blk_f = blk.astype(F32)
    lane_k = lax.broadcasted_iota(jnp.int32, (1, NSA_TOPK), 1)
    idx = jnp.zeros((8, NSA_TOPK), F32)
    for k in range(NSA_TOPK):
        m = jnp.max(score, axis=-1, keepdims=True)
        first = jnp.min(jnp.where(score == m, blk_f, 1e9), axis=-1, keepdims=True)
        idx = jnp.where(lane_k == k, first, idx)
        score = jnp.where(blk_f == first, -jnp.inf, score)
    idx_ref[0] = idx.astype(jnp.int32)
    i_buf = lax.broadcasted_iota(jnp.int32, (1, w_buf), 1)
    win_pos = past - w_buf + i_buf
    d_win = q_pos - win_pos
    m_win = (d_win >= 0) & (d_win < NSA_WINDOW) & (win_pos >= 0)
    s = jnp.zeros((NSA_HEADS, w_buf), F32)
    k_new = jnp.zeros((NSA_HEADS, hd), F32)
    v_new = jnp.zeros((NSA_HEADS, hd), F32)
    for g in range(NSA_KV_HEADS):
        s = s + jnp.where(groups[g], _dot_nt(q, wcache_ref[0, 0, :, g * hd:(g + 1) * hd]), 0.0)
        k_new = k_new + jnp.where(groups[g], wnew_ref[0, :, g * hd:(g + 1) * hd], 0.0)
        v_new = v_new + jnp.where(groups[g], wnew_ref[0, :, NSA_KV_DIM + g * hd:NSA_KV_DIM + (g + 1) * hd], 0.0)
    s = jnp.where(m_win, s - slope * d_win.astype(F32), -jnp.inf)
    s_new = jnp.sum(q * k_new, axis=-1, keepdims=True)
    m = jnp.maximum(jnp.max(s, axis=-1, keepdims=True), s_new)
    e = jnp.where(m_win, jnp.exp(s - m), 0.0)
    e_new = jnp.exp(s_new - m)
    denom = jnp.maximum(jnp.sum(e, axis=-1, keepdims=True) + e_new, 1e-30)
    o = e_new * v_new
    for g in range(NSA_KV_HEADS):
        o = o + jnp.where(groups[g], _dot(e, wcache_ref[0, 0, :, NSA_KV_DIM + g * hd:NSA_KV_DIM + (g + 1) * hd]), 0.0)
    owin_ref[0] = o / denom


def nsa_sample_attention1(q, k_cmp, v_cmp, cache_win, layer, win_new, nblk, past):
    nbatch = q.shape[0]
    nblk_pad = k_cmp.shape[1]
    w_buf = cache_win.shape[2]
    hq = pl.BlockSpec((1, NSA_HEADS, NSA_HD), lambda b: (b, 0, 0))
    return pl.pallas_call(
        functools.partial(_nsa_sample1_kernel, nblk=nblk, nblk_pad=nblk_pad, past=past, w_buf=w_buf),
        grid=(nbatch,),
        in_specs=[hq, pl.BlockSpec((1, nblk_pad, NSA_KV_DIM), lambda b: (b, 0, 0)),
                  pl.BlockSpec((1, nblk_pad, NSA_KV_DIM), lambda b: (b, 0, 0)),
                  pl.BlockSpec((1, 1, w_buf, 2 * NSA_KV_DIM), lambda b: (layer, b, 0, 0)),
                  pl.BlockSpec((1, 1, 2 * NSA_KV_DIM), lambda b: (b, 0, 0))],
        out_specs=[hq, hq, pl.BlockSpec((1, 8, NSA_TOPK), lambda b: (b, 0, 0))],
        out_shape=[jax.ShapeDtypeStruct((nbatch, NSA_HEADS, NSA_HD), F32),
                   jax.ShapeDtypeStruct((nbatch, NSA_HEADS, NSA_HD), F32),
                   jax.ShapeDtypeStruct((nbatch, 8, NSA_TOPK), jnp.int32)],
        compiler_params=_params(("parallel",)),
        name="nsa_sample_attention1",
    )(q, k_cmp, v_cmp, cache_win, win_new)


def _nsa_sample2_kernel(idx_ref, pt_ref, q_ref, b0_ref, b1_ref, b2_ref, b3_ref, new_ref, ocmp_ref, owin_ref, gates_ref,
                        o_ref, m_scr, l_scr, acc_scr, *, past, n_cache_blk):
    hd = NSA_HD
    b = pl.program_id(0)
    k = pl.program_id(1)
    blocks = (b0_ref, b1_ref, b2_ref, b3_ref)

    @pl.when(k == 0)
    def _():
        m_scr[...] = jnp.full(m_scr.shape, -jnp.inf, F32)
        l_scr[...] = jnp.zeros(l_scr.shape, F32)
        acc_scr[...] = jnp.zeros(acc_scr.shape, F32)

    q_pos = past
    lane = lax.broadcasted_iota(jnp.int32, (1, NSA_BLOCK), 1)
    for g in range(NSA_KV_HEADS):
        j = idx_ref[(b * 8 + g) * NSA_TOPK + k]
        slope = _slope_column(8, head0=g * NSA_GROUP)
        q = q_ref[0, g]
        kb = blocks[g][0, 0, :, 0, g, :]
        vb = blocks[g][0, 0, :, 1, g, :]
        d = q_pos - (j * NSA_BLOCK + lane)
        valid = (d >= 0) & (j < n_cache_blk)
        s = jnp.where(valid, _dot_nt(q, kb) - slope * d.astype(F32), -jnp.inf)
        k_new = new_ref[0, :, g * hd:(g + 1) * hd]
        v_new = new_ref[0, :, NSA_KV_DIM + g * hd:NSA_KV_DIM + (g + 1) * hd]
        new_ok = j == q_pos // NSA_BLOCK
        s_new = jnp.where(new_ok, jnp.sum(q * k_new, axis=-1, keepdims=True), -jnp.inf)
        m_old = m_scr[g]
        m_new = jnp.maximum(jnp.maximum(m_old, jnp.max(s, axis=-1, keepdims=True)), s_new)
        m_safe = jnp.where(m_new > -jnp.inf, m_new, 0.0)
        alpha = jnp.exp(m_old - m_safe)
        p = jnp.where(valid, jnp.exp(s - m_safe), 0.0)
        p_new = jnp.where(new_ok, jnp.exp(s_new - m_safe), 0.0)
        l_scr[g] = alpha * l_scr[g] + jnp.sum(p, axis=-1, keepdims=True) + p_new
        acc_scr[g] = alpha * acc_scr[g] + _dot(p, vb) + p_new * v_new
        m_scr[g] = m_new

    @pl.when(k == NSA_TOPK - 1)
    def _():
        for g in range(NSA_KV_HEADS):
            o_sel = acc_scr[g] / jnp.maximum(l_scr[g], 1e-30)
            gt = gates_ref[0, g]
            o_ref[0, g] = gt[:, 0:1] * ocmp_ref[0, g] + gt[:, 1:2] * o_sel + gt[:, 2:3] * owin_ref[0, g]


def nsa_sample_attention2(idx, page_table, q8, cache_sel, layer, sel_new, o_cmp8, o_win8, gates8, past):
    nbatch = q8.shape[0]
    n_cache_blk = past // NSA_BLOCK
    def blk_map(g):
        def index_map(b, k, idx_ref, pt_ref):
            j = jnp.minimum(idx_ref[(b * 8 + g) * NSA_TOPK + k], n_cache_blk - 1)
            return (layer, pt_ref[b, j // BLOCKS_PER_PAGE], j % BLOCKS_PER_PAGE, 0, 0, 0)
        return index_map

    g4 = lambda w: pl.BlockSpec((1, NSA_KV_HEADS, 8, w), lambda b, k, i, p: (b, 0, 0, 0))
    grid_spec = pltpu.PrefetchScalarGridSpec(
        num_scalar_prefetch=2,
        grid=(nbatch, NSA_TOPK),
        in_specs=[g4(NSA_HD)] + [pl.BlockSpec((1, 1, NSA_BLOCK, 2, NSA_KV_HEADS, NSA_HD), blk_map(g))
                                 for g in range(NSA_KV_HEADS)]
                 + [pl.BlockSpec((1, 1, 2 * NSA_KV_DIM), lambda b, k, i, p: (b, 0, 0)), g4(NSA_HD), g4(NSA_HD), g4(LANES)],
        out_specs=g4(NSA_HD),
        scratch_shapes=[pltpu.VMEM((NSA_KV_HEADS, 8, 1), F32), pltpu.VMEM((NSA_KV_HEADS, 8, 1), F32),
                        pltpu.VMEM((NSA_KV_HEADS, 8, NSA_HD), F32)],
    )
    return pl.pallas_call(
        functools.partial(_nsa_sample2_kernel, past=past, n_cache_blk=n_cache_blk),
        grid_spec=grid_spec,
        out_shape=jax.ShapeDtypeStruct((nbatch, NSA_KV_HEADS, 8, NSA_HD), F32),
        compiler_params=_params(("parallel", "arbitrary")),
        name="nsa_sample_attention2",
    )(idx, page_table, q8, cache_sel, cache_sel, cache_sel, cache_sel, sel_new, o_cmp8, o_win8, gates8)


GLA_Q0, GLA_K0, GLA_V0 = 0, GLA_QK_DIM, 2 * GLA_QK_DIM
GLA_A0 = GLA_V0 + GLA_V_DIM
GLA_G0 = GLA_A0 + GLA_A_W


def _gla_kernel(z_ref, s0_ref, a2_ref, ab_ref, ng_ref, o_ref, st_ref, s_scr, *, tt, t_valid, n_t):
    ti = pl.program_id(1)
    c = GLA_CHUNK

    @pl.when(ti == 0)
    def _():
        for h in range(GLA_HEADS):
            s_scr[h] = s0_ref[0, h].T

    a_lr = z_ref[0, :, GLA_A0:GLA_A0 + GLA_A_W].astype(BF16)
    pre = _dot(a_lr, a2_ref[...]) + ab_ref[...]
    log_a = (jnp.minimum(pre, 0.0) - jnp.log(1.0 + jnp.exp(-jnp.abs(pre)))) / GLA_TAU
    row = ti * tt + lax.broadcasted_iota(jnp.int32, (tt, 1), 0)
    log_a = jnp.where(row < t_valid, log_a, 0.0)
    tri = lax.broadcasted_iota(jnp.int32, (c, c), 0) >= lax.broadcasted_iota(jnp.int32, (c, c), 1)
    tri_f = tri.astype(F32)
    for ci in range(tt // c):
        rows = slice(ci * c, (ci + 1) * c)
        for h in range(GLA_HEADS):
            ks = slice(h * GLA_DK, (h + 1) * GLA_DK)
            vs = slice(h * GLA_DV, (h + 1) * GLA_DV)
            cum = _dot(tri_f, log_a[rows, ks], HIGHEST)
            last = cum[c - 1:c, :]
            q_c = z_ref[0, rows, GLA_Q0 + h * GLA_DK:GLA_Q0 + (h + 1) * GLA_DK] * (GLA_DK ** -0.5)
            k_c = z_ref[0, rows, GLA_K0 + h * GLA_DK:GLA_K0 + (h + 1) * GLA_DK]
            v_c = z_ref[0, rows, GLA_V0 + h * GLA_DV:GLA_V0 + (h + 1) * GLA_DV].astype(BF16)
            q_dec = (q_c * jnp.exp(cum)).astype(BF16)
            k_inv = (k_c * jnp.exp(-cum)).astype(BF16)
            k_dec = (k_c * jnp.exp(last - cum)).astype(BF16)
            att = jnp.where(tri, _dot_nt(q_dec, k_inv), 0.0)
            st = s_scr[h]
            o = _dot(att.astype(BF16), v_c) + _dot_nt(q_dec, st.astype(BF16))
            s_scr[h] = st * jnp.exp(last) + _dot_tn(v_c, k_dec)
            o = o * lax.rsqrt(jnp.mean(o * o, axis=-1, keepdims=True) + NORM_EPS) * ng_ref[...]
            gate = z_ref[0, rows, GLA_G0 + h * GLA_DV:GLA_G0 + (h + 1) * GLA_DV]
            o_ref[0, rows, vs] = (o * (gate * _sigmoid(gate))).astype(o_ref.dtype)

    @pl.when(ti == n_t - 1)
    def _():
        for h in range(GLA_HEADS):
            st_ref[0, h] = s_scr[h].T


def gla_mixer(z_gla, s0, a2p, ab, ng, t_valid):
    b, t, _ = z_gla.shape
    tt = min(t, 512)
    n_t = t // tt
    return pl.pallas_call(
        functools.partial(_gla_kernel, tt=tt, t_valid=t_valid, n_t=n_t),
        grid=(b, n_t),
        in_specs=[pl.BlockSpec((1, tt, GLA_W), lambda i, j: (i, j, 0)),
                  pl.BlockSpec((1, GLA_HEADS, GLA_DK, GLA_DV), lambda i, j: (i, 0, 0, 0)),
                  pl.BlockSpec((GLA_A_W, GLA_QK_DIM), lambda i, j: (0, 0)),
                  pl.BlockSpec((1, GLA_QK_DIM), lambda i, j: (0, 0)),
                  pl.BlockSpec((1, GLA_DV), lambda i, j: (0, 0))],
        out_specs=[pl.BlockSpec((1, tt, GLA_V_DIM), lambda i, j: (i, j, 0)),
                   pl.BlockSpec((1, GLA_HEADS, GLA_DK, GLA_DV), lambda i, j: (i, 0, 0, 0))],
        out_shape=[jax.ShapeDtypeStruct((b, t, GLA_V_DIM), BF16),
                   jax.ShapeDtypeStruct((b, GLA_HEADS, GLA_DK, GLA_DV), F32)],
        scratch_shapes=[pltpu.VMEM((GLA_HEADS, GLA_DV, GLA_DK), F32)],
        compiler_params=_params(("parallel", "arbitrary")),
        name="gla_mixer",
    )(z_gla, s0, a2p, ab, ng)


RW_R0, RW_K0, RW_V0 = 0, RW_DIM, 2 * RW_DIM
RW_WL0 = 3 * RW_DIM
RW_AL0 = RW_WL0 + RW_LR_W
RW_GL0 = RW_AL0 + RW_LR_W


def _rw_prep_kernel(z_ref, sh_ref, mu_ref, w0_ref, w2_ref, a0_ref, a2_ref, g2_ref, kk_ref, ka_ref, rk_ref, seg_ref,
                    r_ref, dec_ref, k_ref, v_ref, nkk_ref, kka_ref, g_ref, bonus_ref):
    z = z_ref[...]
    xm = z + (sh_ref[...] - z) * mu_ref[...]
    r = xm[:, RW_R0:RW_R0 + RW_DIM]
    k = xm[:, RW_K0:RW_K0 + RW_DIM]
    v = xm[:, RW_V0:RW_V0 + RW_DIM]
    wl = xm[:, RW_WL0:RW_WL0 + RW_LR_W]
    al = xm[:, RW_AL0:RW_AL0 + RW_LR_W]
    gl = xm[:, RW_GL0:RW_GL0 + RW_LR_W]
    w = -_softplus(-(w0_ref[...] + _dot(jnp.tanh(wl).astype(BF16), w2_ref[...]))) - 0.5
    a = _sigmoid(a0_ref[...] + _dot(al.astype(BF16), a2_ref[...]))
    kk = k * kk_ref[...]
    norm = jnp.sqrt(_dot(kk * kk, seg_ref[...], HIGHEST))
    kk = kk / jnp.maximum(norm, 1e-12)
    k2 = k * (1.0 + (a - 1.0) * ka_ref[...])
    r_ref[...] = r
    dec_ref[...] = -jnp.exp(w)
    k_ref[...] = k2
    v_ref[...] = v
    nkk_ref[...] = -kk
    kka_ref[...] = kk * a
    g_ref[...] = _dot(_sigmoid(gl).astype(BF16), g2_ref[...])
    bonus_ref[...] = _dot(r * k2 * rk_ref[...], seg_ref[...], HIGHEST) * v


def rw_prep(z_rw, shifted, rp, tm=512):
    m = z_rw.shape[0]
    tm = min(tm, m)
    row = lambda w: pl.BlockSpec((tm, w), lambda i: (i, 0))
    const = lambda r, w: pl.BlockSpec((r, w), lambda i: (0, 0))
    out = jax.ShapeDtypeStruct((m, RW_DIM), F32)
    return pl.pallas_call(
        _rw_prep_kernel,
        grid=(m // tm,),
        in_specs=[row(RW_W), row(RW_W), const(1, RW_W), const(1, RW_DIM), const(RW_LR_W, RW_DIM), const(1, RW_DIM),
                  const(RW_LR_W, RW_DIM), const(RW_LR_W, RW_DIM), const(1, RW_DIM), const(1, RW_DIM), const(1, RW_DIM),
                  const(RW_DIM, RW_DIM)],
        out_specs=[row(RW_DIM)] * 8,
        out_shape=[out] * 8,
        compiler_params=_params(("parallel",)),
        name="rw_prep",
    )(z_rw, shifted, rp["mu"], rp["w0"], rp["w2"], rp["a0"], rp["a2"], rp["g2"], rp["kk"], rp["ka"], rp["rk"], rp["seg"])


RW_CHUNK = 64
RW_CHUNK_LOG2 = 6


def _rw_scan_kernel(r_ref, dec_ref, k_ref, v_ref, nkk_ref, kka_ref, g_ref, bonus_ref, lng_ref, lnb_ref, s0_ref,
                    o_ref, st_ref, s_scr, *, tt, n_t):
    ti = pl.program_id(1)

    @pl.when(ti == 0)
    def _():
        s_scr[...] = s0_ref[0]

    c = RW_CHUNK
    row_i = lax.broadcasted_iota(jnp.int32, (c, c), 0)
    col_i = lax.broadcasted_iota(jnp.int32, (c, c), 1)
    lower = row_i >= col_i
    strict = row_i > col_i
    lower_f = lower.astype(F32)
    eye_f = (row_i == col_i).astype(F32)
    heads = [slice(h * RW_HD, (h + 1) * RW_HD) for h in range(RW_HEADS)]
    hp = functools.partial(_dot, precision=HIGHEST)
    hp_nt = functools.partial(_dot_nt, precision=HIGHEST)
    hp_tn = functools.partial(_dot_tn, precision=HIGHEST)

    def chunk(i, carry):
        rows = pl.ds(pl.multiple_of(i * c, c), c)
        for h, hs in enumerate(heads):
            lw = dec_ref[0, rows, hs]
            cum = hp(lower_f, lw)
            g_inv = jnp.exp(-cum)
            g_last = jnp.exp(cum[c - 1:c, :])
            bt = nkk_ref[0, rows, hs] * jnp.exp(cum - lw)
            at = kka_ref[0, rows, hs] * g_inv
            kt = k_ref[0, rows, hs] * g_inv
            rt = r_ref[0, rows, hs] * jnp.exp(cum)
            v = v_ref[0, rows, hs]
            ma = jnp.where(strict, hp_nt(bt, at), 0.0)
            mk = jnp.where(strict, hp_nt(bt, kt), 0.0)
            na = jnp.where(lower, hp_nt(rt, at), 0.0)
            nk = jnp.where(lower, hp_nt(rt, kt), 0.0)
            inv = eye_f + ma
            power = ma
            for _ in range(RW_CHUNK_LOG2 - 1):
                power = hp(power, power)
                inv = inv + hp(inv, power)
            s0 = s_scr[h]
            u = hp(inv, hp_nt(bt, s0) + hp(mk, v))
            o = hp_nt(rt, s0) + hp(na, u) + hp(nk, v)
            s_scr[h] = s0 * g_last + hp_tn(u, at * g_last) + hp_tn(v, kt * g_last)
            mean = jnp.mean(o, axis=-1, keepdims=True)
            cen = o - mean
            var = jnp.mean(cen * cen, axis=-1, keepdims=True)
            y = cen * lax.rsqrt(var + RW_GN_EPS) * lng_ref[:, hs] + lnb_ref[:, hs]
            o_ref[0, rows, hs] = (y + bonus_ref[0, rows, hs]) * g_ref[0, rows, hs]
        return carry

    lax.fori_loop(0, tt // c, chunk, 0)

    @pl.when(ti == n_t - 1)
    def _():
        st_ref[0] = s_scr[...]


def rw_scan(r, dec, k, v, nkk, kka, g, bonus, ln_g, ln_b, s0):
    b, t, _ = r.shape
    tt = min(t, 256)
    n_t = t // tt
    tile = pl.BlockSpec((1, tt, RW_DIM), lambda i, j: (i, j, 0))
    vec = pl.BlockSpec((1, RW_DIM), lambda i, j: (0, 0))
    st = pl.BlockSpec((1, RW_HEADS, RW_HD, RW_HD), lambda i, j: (i, 0, 0, 0))
    return pl.pallas_call(
        functools.partial(_rw_scan_kernel, tt=tt, n_t=n_t),
        grid=(b, n_t),
        in_specs=[tile] * 8 + [vec, vec, st],
        out_specs=[tile, st],
        out_shape=[jax.ShapeDtypeStruct((b, t, RW_DIM), F32), jax.ShapeDtypeStruct(s0.shape, F32)],
        scratch_shapes=[pltpu.VMEM((RW_HEADS, RW_HD, RW_HD), F32)],
        compiler_params=_params(("parallel", "arbitrary")),
        name="rw_scan",
    )(r, dec, k, v, nkk, kka, g, bonus, ln_g, ln_b, s0)


def _pad_last(w, width):
    return jnp.pad(w, [(0, 0)] * (w.ndim - 1) + [(0, width - w.shape[-1])])


def _pad_rows(w, rows):
    return jnp.pad(w, [(0, rows - w.shape[0])] + [(0, 0)] * (w.ndim - 1))


def _split_last(z, sizes):
    return jnp.split(z, [int(c) for c in np.cumsum(np.array(sizes))[:-1]], axis=-1)


def _rw_padded(cols):
    r, k, v, wl, al, gl = _split_last(cols, RW_SIZES)
    return jnp.concatenate([r, k, v, _pad_last(wl, RW_LR_W), _pad_last(al, RW_LR_W), _pad_last(gl, RW_LR_W)], axis=-1)


def _rw_unpadded(cols):
    return jnp.concatenate([cols[..., :RW_WL0], cols[..., RW_WL0:RW_WL0 + RW_DECAY_RANK],
                            cols[..., RW_AL0:RW_AL0 + RW_A_RANK], cols[..., RW_GL0:RW_GL0 + RW_G_RANK]], axis=-1)


def _layer_params(l, p):
    q, cmp_, sel, win, gate, gq, gk, gv, ga, gg, rw, mg = _split_last(p["w_in"][l], IN_SIZES)
    seg = np.kron(np.eye(RW_HEADS, dtype=np.float32), np.ones((RW_HD, RW_HD), np.float32))
    row = lambda v: v.reshape(1, -1)
    return {
        "norm1_g": p["norm1_g"][l], "norm2_g": p["norm2_g"][l],
        "w_nsa": jnp.concatenate([q, cmp_, sel, win, _pad_last(gate, GATE_W)], axis=1).astype(BF16),
        "w_gla": jnp.concatenate([gq, gk, gv, _pad_last(ga, GLA_A_W), gg], axis=1).astype(BF16),
        "w_rw": _rw_padded(rw).astype(BF16),
        "w_mg": mg.astype(BF16),
        "qk_g": p["nsa_qk_g"][l],
        "cw": {"pos2": jnp.concatenate([p["cmp_pos"][l]] * 2, axis=-1),
               "w1": p["cmp_w1"][l].reshape(2, NSA_BLOCK, NSA_HD, NSA_CMP_HID).astype(BF16),
               "w2": p["cmp_w2"][l].astype(BF16),
               "kg": row(p["nsa_qk_g"][l, 1])},
        "gla_a2": _pad_rows(p["gla_a2"][l], GLA_A_W).astype(BF16),
        "gla_ab": row(p["gla_a_b"][l]), "gla_ng": row(p["gla_norm_g"][l]),
        "rp": {"mu": row(_rw_padded(p["rw_mu"][l])), "w0": row(p["rw_w0"][l]),
               "w2": _pad_rows(p["rw_w2"][l], RW_LR_W).astype(BF16), "a0": row(p["rw_a0"][l]),
               "a2": _pad_rows(p["rw_a2"][l], RW_LR_W).astype(BF16),
               "g2": _pad_rows(p["rw_g2"][l], RW_LR_W).astype(BF16),
               "kk": row(p["rw_kk"][l]), "ka": row(p["rw_ka"][l]), "rk": row(p["rw_rk"][l]), "seg": jnp.asarray(seg)},
        "ln_g": row(p["rw_ln_g"][l]), "ln_b": row(p["rw_ln_b"][l]),
        "nsa_up": p["nsa_up"][l].astype(BF16), "gla_up": p["gla_up"][l].astype(BF16), "rw_up": p["rw_up"][l].astype(BF16),
        "w_out": p["w_out"][l].astype(BF16), "mlp_w1": p["mlp_w1"][l].astype(BF16), "mlp_w2": p["mlp_w2"][l].astype(BF16),
    }


def _project_in(x2, lp, tm):
    hn = rmsnorm_rows(x2, lp["norm1_g"], tm)
    return tuple(matmul(hn, lp[w], tm=tm) for w in ("w_nsa", "w_gla", "w_rw", "w_mg"))


def _project_out(x2, o_nsa, o_gla, o_rw, z_mg, lp, tm):
    merged = merge_branches(o_nsa, o_gla, o_rw, lp["nsa_up"], lp["gla_up"], lp["rw_up"], z_mg, tm=tm)
    x1 = matmul(merged, lp["w_out"], "residual", x2, tm=tm)
    h2 = rmsnorm_rows(x1, lp["norm2_g"], tm)
    u = matmul(h2, lp["mlp_w1"], "relu2", tm=tm)
    return matmul(u, lp["mlp_w2"], "residual", x1, tm=tm)


def _kv_rows(rows, b, t):
    return rows.reshape(b, t, 2, NSA_KV_HEADS, NSA_HD)


def _prompt_layer(x, lp):
    b, t, d = x.shape
    m = b * t
    tm = 512
    x2 = x.reshape(m, d)
    z_nsa, z_gla, z_rw, z_mg = _project_in(x2, lp, tm)
    qn, cmp_rows, sel_rows, win_rows, sel_b, win_b, gates = nsa_prep(z_nsa, lp["qk_g"])
    nblk = t // NSA_BLOCK
    k_cmp, v_cmp = compress_rows(cmp_rows, lp["cw"], nb=32)
    per_b = lambda a: a.reshape(b, t, a.shape[-1])
    o_nsa = nsa_prompt_attention(per_b(qn), per_b(gates), per_b(sel_b), per_b(win_b),
                                 k_cmp.reshape(b, nblk, NSA_KV_DIM), v_cmp.reshape(b, nblk, NSA_KV_DIM))
    o_gla, gla_st = gla_mixer(per_b(z_gla), jnp.zeros((b, GLA_HEADS, GLA_DK, GLA_DV), F32),
                              lp["gla_a2"], lp["gla_ab"], lp["gla_ng"], t_valid=t)
    z_rw3 = per_b(z_rw)
    shifted = jnp.concatenate([jnp.zeros((b, 1, RW_W), F32), z_rw3[:, :-1]], axis=1).reshape(m, RW_W)
    prep = rw_prep(z_rw, shifted, lp["rp"])
    o_rw, rw_st = rw_scan(*(per_b(a) for a in prep), lp["ln_g"], lp["ln_b"],
                          jnp.zeros((b, RW_HEADS, RW_HD, RW_HD), F32))
    y = _project_out(x2, o_nsa.reshape(m, -1), o_gla.reshape(m, -1), o_rw.reshape(m, -1), z_mg, lp, tm)
    w_keep = min(NSA_WINDOW, t)
    return (y.reshape(b, t, d), _kv_rows(cmp_rows, b, t), _kv_rows(sel_rows, b, t), _kv_rows(win_rows, b, t)[:, t - w_keep:],
            gla_st, rw_st, _rw_unpadded(z_rw3[:, -1]))


SAMPLE_ROWS = SUBLANES_BF16


def _sample_layer(x2, lp, layer, caches, page_table, nbatch):
    cache_cmp, cache_sel, cache_win, state_gla, state_rwkv, state_shift = caches
    tm = SAMPLE_ROWS
    past = page_table.shape[1] * PAGE_SIZE
    nblk = -(-(past + 1) // NSA_BLOCK)
    z_nsa, z_gla, z_rw, z_mg = _project_in(x2, lp, tm)
    qn, cmp_rows, sel_rows, win_rows, _, _, gates = nsa_prep(z_nsa, lp["qk_g"])
    kc_past, vc_past = compress_pages(cache_cmp, layer, page_table, lp["cw"])
    last_blk = jnp.pad(cmp_rows[:nbatch, None, :], ((0, 0), (0, NSA_BLOCK - 1), (0, 0))).reshape(nbatch * NSA_BLOCK, -1)
    kc_last, vc_last = compress_rows(last_blk, lp["cw"], nb=nbatch)
    nblk_pad = -(-nblk // 8) * 8
    with_last = lambda past_s, last_s: jnp.pad(jnp.concatenate([past_s, last_s[:, None, :]], axis=1),
                                               ((0, 0), (0, nblk_pad - nblk), (0, 0)))
    q16 = qn[:nbatch].astype(F32).reshape(nbatch, NSA_HEADS, NSA_HD)
    o_cmp, o_win, idx = nsa_sample_attention1(
        q16, with_last(kc_past, kc_last), with_last(vc_past, vc_last),
        cache_win.reshape(cache_win.shape[0], nbatch, cache_win.shape[2], 2 * NSA_KV_DIM), layer,
        win_rows[:nbatch, None, :], nblk, past)
    grp8 = lambda a: jnp.pad(a.reshape(nbatch, NSA_KV_HEADS, NSA_GROUP, a.shape[-1]),
                             ((0, 0), (0, 0), (0, 8 - NSA_GROUP), (0, 0)))
    gates3 = gates[:nbatch, :3 * NSA_HEADS].reshape(nbatch, 3, NSA_HEADS).transpose(0, 2, 1)
    o8 = nsa_sample_attention2(idx.reshape(-1), page_table, grp8(q16), cache_sel, layer, sel_rows[:nbatch, None, :],
                               grp8(o_cmp), grp8(o_win), grp8(_pad_last(gates3, LANES)), past)
    o_nsa = _pad_rows(o8[:, :, :NSA_GROUP].reshape(nbatch, NSA_Q_DIM), tm)
    z_gla_c = jnp.pad(z_gla[:nbatch, None, :], ((0, 0), (0, GLA_CHUNK - 1), (0, 0)))
    o_gla, gla_st = gla_mixer(z_gla_c, state_gla[layer], lp["gla_a2"], lp["gla_ab"], lp["gla_ng"], t_valid=1)
    o_gla = _pad_rows(o_gla[:, 0], tm)
    shifted = _pad_rows(_rw_padded(state_shift[layer]), tm)
    prep = rw_prep(z_rw, shifted, lp["rp"])
    prep = [jnp.pad(a[:nbatch, None, :], ((0, 0), (0, RW_CHUNK - 1), (0, 0))) for a in prep]
    o_rw, rw_st = rw_scan(*prep, lp["ln_g"], lp["ln_b"], state_rwkv[layer])
    o_rw = _pad_rows(o_rw[:, 0], tm)
    y = _project_out(x2, o_nsa, o_gla, o_rw, z_mg, lp, tm)
    return (y, _kv_rows(cmp_rows[:nbatch], nbatch, 1), _kv_rows(sel_rows[:nbatch], nbatch, 1),
            _kv_rows(win_rows[:nbatch], nbatch, 1), gla_st, rw_st, _rw_unpadded(z_rw[:nbatch]))


def kernel(x_prompt, x_sample, cache_cmp_kv, cache_sel_kv, cache_win_kv, state_gla, state_rwkv, state_rwkv_shift,
           page_table, norm1_g, w_in, nsa_qk_g, cmp_pos, cmp_w1, cmp_w2, nsa_up, gla_a2, gla_a_b, gla_norm_g, gla_up,
           rw_mu, rw_w0, rw_w2, rw_a0, rw_a2, rw_g2, rw_kk, rw_ka, rw_rk, rw_ln_g, rw_ln_b, rw_up, w_out, norm2_g,
           mlp_w1, mlp_w2):
    p = dict(norm1_g=norm1_g, w_in=w_in, nsa_qk_g=nsa_qk_g, cmp_pos=cmp_pos, cmp_w1=cmp_w1, cmp_w2=cmp_w2, nsa_up=nsa_up,
             gla_a2=gla_a2, gla_a_b=gla_a_b, gla_norm_g=gla_norm_g, gla_up=gla_up, rw_mu=rw_mu, rw_w0=rw_w0, rw_w2=rw_w2,
             rw_a0=rw_a0, rw_a2=rw_a2, rw_g2=rw_g2, rw_kk=rw_kk, rw_ka=rw_ka, rw_rk=rw_rk, rw_ln_g=rw_ln_g,
             rw_ln_b=rw_ln_b, rw_up=rw_up, w_out=w_out, norm2_g=norm2_g, mlp_w1=mlp_w1, mlp_w2=mlp_w2)
    nbatch = x_sample.shape[0]
    caches = (cache_cmp_kv, cache_sel_kv, cache_win_kv, state_gla, state_rwkv, state_rwkv_shift)
    xp = x_prompt
    xs = _pad_rows(x_sample.reshape(nbatch, D_MODEL), SAMPLE_ROWS)
    outs_p, outs_s = [], []
    for l in range(DEPTH):
        lp = _layer_params(l, p)
        xp, *rest_p = _prompt_layer(xp, lp)
        xs, *rest_s = _sample_layer(xs, lp, l, caches, page_table, nbatch)
        outs_p.append(rest_p)
        outs_s.append(rest_s)
    stack = lambda outs, i: jnp.stack([o[i] for o in outs])
    return ((xp, xs[:nbatch].reshape(x_sample.shape))
            + tuple(stack(outs_p, i) for i in range(6)) + tuple(stack(outs_s, i) for i in range(6)))
```

```python
import functools

import numpy as np
import jax
import jax.numpy as jnp
from jax import lax
from jax.experimental import pallas as pl
from jax.experimental.pallas import tpu as pltpu

F32 = jnp.float32
BF16 = jnp.bfloat16
HIGHEST = lax.Precision.HIGHEST

D_MODEL = 2048
DEPTH = 2
PAGE_SIZE = 128
NSA_HEADS = 16
NSA_KV_HEADS = 4
NSA_GROUP = NSA_HEADS // NSA_KV_HEADS
NSA_HD = 64
NSA_BLOCK = 64
NSA_TOPK = 16
NSA_WINDOW = 512
NSA_CMP_HID = 128
GLA_HEADS = 4
GLA_DK = 64
GLA_DV = 128
GLA_GATE_RANK = 16
GLA_TAU = 16.0
GLA_CHUNK = 64
RW_HEADS = 8
RW_HD = 64
RW_DECAY_RANK = 32
RW_A_RANK = 32
RW_G_RANK = 96
RW_GN_EPS = 64e-5
D_FF = 4 * D_MODEL
NORM_EPS = 1e-6

NSA_Q_DIM = NSA_HEADS * NSA_HD
NSA_KV_DIM = NSA_KV_HEADS * NSA_HD
GLA_QK_DIM = GLA_HEADS * GLA_DK
GLA_V_DIM = GLA_HEADS * GLA_DV
RW_DIM = RW_HEADS * RW_HD
RW_SIZES = (RW_DIM, RW_DIM, RW_DIM, RW_DECAY_RANK, RW_A_RANK, RW_G_RANK)
RW_PROJ = sum(RW_SIZES)
IN_SIZES = (NSA_Q_DIM, 2 * NSA_KV_DIM, 2 * NSA_KV_DIM, 2 * NSA_KV_DIM, 3 * NSA_HEADS,
            GLA_QK_DIM, GLA_QK_DIM, GLA_V_DIM, GLA_GATE_RANK, GLA_V_DIM, RW_PROJ, 3 * D_MODEL)

LANES = 128
SUBLANES_BF16 = 16
VMEM_LIMIT = 56 * 1024 * 1024

GATE_W = LANES
NSA_W = NSA_Q_DIM + 6 * NSA_KV_DIM + GATE_W
GLA_A_W = LANES
GLA_W = 2 * GLA_QK_DIM + 2 * GLA_V_DIM + GLA_A_W
RW_LR_W = LANES
RW_W = 3 * RW_DIM + 3 * RW_LR_W
MERGE_W = 3 * D_MODEL


def _params(sem):
    return pltpu.CompilerParams(dimension_semantics=sem, vmem_limit_bytes=VMEM_LIMIT)


def _largest_tile(n, cap):
    best = None
    for t in range(LANES, min(n, cap) + 1, LANES):
        if n % t == 0:
            best = t
    return best if best is not None else n


def _sigmoid(x):
    return 1.0 / (1.0 + jnp.exp(-x))


def _softplus(y):
    return jnp.maximum(y, 0.0) + jnp.log(1.0 + jnp.exp(-jnp.abs(y)))


def _masked_softmax(s, mask):
    s = jnp.where(mask, s, -jnp.inf)
    m = jnp.max(s, axis=-1, keepdims=True)
    m = jnp.where(m > -jnp.inf, m, 0.0)
    e = jnp.where(mask, jnp.exp(s - m), 0.0)
    return e / jnp.maximum(jnp.sum(e, axis=-1, keepdims=True), 1e-30)


def _dot(a, b, precision=None):
    return jnp.dot(a, b, preferred_element_type=F32, precision=precision)


def _dot_nt(a, b, precision=None):
    return lax.dot_general(a, b, (((1,), (1,)), ((), ())), preferred_element_type=F32, precision=precision)


def _dot_tn(a, b, precision=None):
    return lax.dot_general(a, b, (((0,), (0,)), ((), ())), preferred_element_type=F32, precision=precision)


def _split2(x):
    hi = x.astype(BF16)
    return hi, (x - hi.astype(F32)).astype(BF16)


def _split3(x):
    hi = x.astype(BF16)
    rest = x - hi.astype(F32)
    mid = rest.astype(BF16)
    return hi, mid, (rest - mid.astype(F32)).astype(BF16)


def _dot3(dot, a, b):
    (a_hi, a_lo), (b_hi, b_lo) = a, b
    return dot(a_hi, b_hi) + dot(a_hi, b_lo) + dot(a_lo, b_hi)


def _rmsnorm_kernel(x_ref, g_ref, o_ref):
    x = x_ref[...]
    y = x * lax.rsqrt(jnp.mean(x * x, axis=-1, keepdims=True) + NORM_EPS) * g_ref[...]
    o_ref[...] = y.astype(o_ref.dtype)


def rmsnorm_rows(x, g, tm):
    m, d = x.shape
    return pl.pallas_call(
        _rmsnorm_kernel,
        grid=(m // tm,),
        in_specs=[pl.BlockSpec((tm, d), lambda i: (i, 0)), pl.BlockSpec((1, d), lambda i: (0, 0))],
        out_specs=pl.BlockSpec((tm, d), lambda i: (i, 0)),
        out_shape=jax.ShapeDtypeStruct((m, d), BF16),
        compiler_params=_params(("parallel",)),
        name="rmsnorm_rows",
    )(x, g.reshape(1, d))


def _mm_kernel(*refs, nk, mode):
    if mode == "residual":
        a_ref, w_ref, r_ref, o_ref = refs[:4]
        scratch = refs[4:]
    else:
        a_ref, w_ref, o_ref = refs[:3]
        scratch = refs[3:]
    part = _dot(a_ref[...], w_ref[...])

    def finish(acc):
        if mode == "relu2":
            acc = jnp.square(jnp.maximum(acc, 0.0))
        elif mode == "residual":
            acc = r_ref[...] + acc
        o_ref[...] = acc.astype(o_ref.dtype)

    if nk == 1:
        finish(part)
    else:
        acc_ref = scratch[0]
        k = pl.program_id(2)

        @pl.when(k == 0)
        def _():
            acc_ref[...] = part

        @pl.when(k > 0)
        def _():
            acc_ref[...] += part

        @pl.when(k == nk - 1)
        def _():
            finish(acc_ref[...])


def matmul(a, w, mode="plain", residual=None, tm=512):
    m, kdim = a.shape
    n = w.shape[1]
    tm = min(tm, m)
    tk = min(kdim, 2048)
    nk = kdim // tk
    tn = _largest_tile(n, 1024 if (mode == "residual" or nk > 1) else 2048)
    in_specs = [pl.BlockSpec((tm, tk), lambda i, j, k: (i, k)), pl.BlockSpec((tk, tn), lambda i, j, k: (k, j))]
    args = [a, w]
    if mode == "residual":
        in_specs.append(pl.BlockSpec((tm, tn), lambda i, j, k: (i, j)))
        args.append(residual)
    out_dtype = BF16 if mode == "relu2" else F32
    return pl.pallas_call(
        functools.partial(_mm_kernel, nk=nk, mode=mode),
        grid=(m // tm, n // tn, nk),
        in_specs=in_specs,
        out_specs=pl.BlockSpec((tm, tn), lambda i, j, k: (i, j)),
        out_shape=jax.ShapeDtypeStruct((m, n), out_dtype),
        scratch_shapes=[pltpu.VMEM((tm, tn), F32)] if nk > 1 else [],
        compiler_params=_params(("parallel", "parallel", "arbitrary")),
        name="matmul_" + mode,
    )(*args)


def _merge_kernel(on_ref, og_ref, or_ref, wn_ref, wg_ref, wr_ref, m0_ref, m1_ref, m2_ref, o_ref):
    acc = _sigmoid(m0_ref[...]) * _dot(on_ref[...].astype(BF16), wn_ref[...])
    acc += _sigmoid(m1_ref[...]) * _dot(og_ref[...].astype(BF16), wg_ref[...])
    acc += _sigmoid(m2_ref[...]) * _dot(or_ref[...].astype(BF16), wr_ref[...])
    o_ref[...] = acc.astype(o_ref.dtype)


def merge_branches(o_nsa, o_gla, o_rw, w_nsa, w_gla, w_rw, z_merge, tm=512, tn=1024):
    m = o_nsa.shape[0]
    tm = min(tm, m)
    nj = D_MODEL // tn
    row = lambda kd: pl.BlockSpec((tm, kd), lambda i, j: (i, 0))
    wsp = lambda kd: pl.BlockSpec((kd, tn), lambda i, j: (0, j))
    gate = lambda b: pl.BlockSpec((tm, tn), lambda i, j: (i, j + b * nj))
    return pl.pallas_call(
        _merge_kernel,
        grid=(m // tm, nj),
        in_specs=[row(NSA_Q_DIM), row(GLA_V_DIM), row(RW_DIM), wsp(NSA_Q_DIM), wsp(GLA_V_DIM), wsp(RW_DIM),
                  gate(0), gate(1), gate(2)],
        out_specs=pl.BlockSpec((tm, tn), lambda i, j: (i, j)),
        out_shape=jax.ShapeDtypeStruct((m, D_MODEL), BF16),
        compiler_params=_params(("parallel", "parallel")),
        name="merge_branches",
    )(o_nsa, o_gla, o_rw, w_nsa, w_gla, w_rw, z_merge, z_merge, z_merge)


def _head_rmsnorm(x, g):
    return x * lax.rsqrt(jnp.mean(x * x, axis=-1, keepdims=True) + NORM_EPS) * g


def _nsa_prep_kernel(z_ref, g_ref, qn_ref, cmp_ref, sel_ref, win_ref, selb_ref, winb_ref, gates_ref):
    hd = NSA_HD
    g_q, g_sel, g_win = g_ref[0:1, :], g_ref[2:3, :], g_ref[3:4, :]
    scale = NSA_HD ** -0.5
    for h in range(NSA_HEADS):
        q = _head_rmsnorm(z_ref[:, h * hd:(h + 1) * hd], g_q)
        qn_ref[:, h * hd:(h + 1) * hd] = (q * scale).astype(BF16)
    c0 = NSA_Q_DIM
    cmp_ref[...] = z_ref[:, c0:c0 + 2 * NSA_KV_DIM]
    for (off, gain, o_ref, ob_ref) in ((c0 + 2 * NSA_KV_DIM, g_sel, sel_ref, selb_ref),
                                       (c0 + 4 * NSA_KV_DIM, g_win, win_ref, winb_ref)):
        for h in range(NSA_KV_HEADS):
            k = _head_rmsnorm(z_ref[:, off + h * hd:off + (h + 1) * hd], gain)
            o_ref[:, h * hd:(h + 1) * hd] = k
            ob_ref[:, h * hd:(h + 1) * hd] = k.astype(BF16)
        v = z_ref[:, off + NSA_KV_DIM:off + 2 * NSA_KV_DIM]
        o_ref[:, NSA_KV_DIM:] = v
        ob_ref[:, NSA_KV_DIM:] = v.astype(BF16)
    gates_ref[...] = _sigmoid(z_ref[:, c0 + 6 * NSA_KV_DIM:])


def nsa_prep(z_nsa, qk_g, tm=256):
    m = z_nsa.shape[0]
    tm = min(tm, m)
    kvw = 2 * NSA_KV_DIM
    row = lambda w: pl.BlockSpec((tm, w), lambda i: (i, 0))
    return pl.pallas_call(
        _nsa_prep_kernel,
        grid=(m // tm,),
        in_specs=[row(NSA_W), pl.BlockSpec((4, NSA_HD), lambda i: (0, 0))],
        out_specs=[row(NSA_Q_DIM), row(kvw), row(kvw), row(kvw), row(kvw), row(kvw), row(GATE_W)],
        out_shape=[jax.ShapeDtypeStruct((m, NSA_Q_DIM), BF16),
                   jax.ShapeDtypeStruct((m, kvw), F32), jax.ShapeDtypeStruct((m, kvw), F32),
                   jax.ShapeDtypeStruct((m, kvw), F32), jax.ShapeDtypeStruct((m, kvw), BF16),
                   jax.ShapeDtypeStruct((m, kvw), BF16), jax.ShapeDtypeStruct((m, GATE_W), F32)],
        compiler_params=_params(("parallel",)),
        name="nsa_prep",
    )(z_nsa, qk_g)


QUARTERS = 2 * NSA_KV_DIM // LANES


def _compress_compute(x_ref, nb, pos_ref, w1_ref, w2_ref, kg_ref, kc_ref, vc_ref, acc_ref):
    acc_ref[...] = jnp.zeros_like(acc_ref)
    row_stride = NSA_BLOCK * QUARTERS

    def body(l, carry):
        for q in range(QUARTERS):
            kv = q // 2
            x = x_ref[pl.ds(l * QUARTERS + q, nb, stride=row_stride), :] + pos_ref[kv, pl.ds(l, 1), :]
            for half in range(2):
                acc_ref[q * 2 + half] += _dot(x[:, half * NSA_HD:(half + 1) * NSA_HD].astype(BF16), w1_ref[kv, l])
        return carry

    lax.fori_loop(0, NSA_BLOCK, body, 0)
    for q in range(QUARTERS):
        kv = q // 2
        for half in range(2):
            g = (q % 2) * 2 + half
            hid = jax.nn.gelu(acc_ref[q * 2 + half], approximate=True)
            s = _dot(hid.astype(BF16), w2_ref[kv])
            if kv == 0:
                kc_ref[:, g * NSA_HD:(g + 1) * NSA_HD] = _head_rmsnorm(s, kg_ref[...])
            else:
                vc_ref[:, g * NSA_HD:(g + 1) * NSA_HD] = s


def _compress_kernel(x_ref, pos_ref, w1_ref, w2_ref, kg_ref, kc_ref, vc_ref, acc_ref, *, nb):
    _compress_compute(x_ref, nb, pos_ref, w1_ref, w2_ref, kg_ref, kc_ref, vc_ref, acc_ref)


def _cmp_weight_specs(nidx):
    zero = lambda n: (lambda *idx: (0,) * n)
    return [pl.BlockSpec((2, NSA_BLOCK, LANES), zero(3)),
            pl.BlockSpec((2, NSA_BLOCK, NSA_HD, NSA_CMP_HID), zero(4)),
            pl.BlockSpec((2, NSA_CMP_HID, NSA_HD), zero(3)),
            pl.BlockSpec((1, NSA_HD), zero(2))]


def compress_rows(rows, cw, nb):
    nblk = rows.shape[0] // NSA_BLOCK
    nb = min(nb, nblk)
    x4 = rows.reshape(nblk * NSA_BLOCK * QUARTERS, LANES)
    out = jax.ShapeDtypeStruct((nblk, NSA_KV_DIM), F32)
    return pl.pallas_call(
        functools.partial(_compress_kernel, nb=nb),
        grid=(nblk // nb,),
        in_specs=[pl.BlockSpec((nb * NSA_BLOCK * QUARTERS, LANES), lambda i: (i, 0))] + _cmp_weight_specs(1),
        out_specs=[pl.BlockSpec((nb, NSA_KV_DIM), lambda i: (i, 0))] * 2,
        out_shape=[out, out],
        scratch_shapes=[pltpu.VMEM((2 * QUARTERS, nb, NSA_CMP_HID), F32)],
        compiler_params=_params(("parallel",)),
        name="compress_rows",
    )(x4, cw["pos2"], cw["w1"], cw["w2"], cw["kg"])


PAGES_PER_STEP = 32
BLOCKS_PER_PAGE = PAGE_SIZE // NSA_BLOCK


def _compress_pages_kernel(pt_ref, *refs, npg):
    page_refs = refs[:npg]
    pos_ref, w1_ref, w2_ref, kg_ref, kc_ref, vc_ref, x_ref, acc_ref = refs[npg:]
    rows = PAGE_SIZE * QUARTERS
    for p in range(npg):
        x_ref[p * rows:(p + 1) * rows, :] = page_refs[p][0, 0]
    _compress_compute(x_ref, npg * BLOCKS_PER_PAGE, pos_ref, w1_ref, w2_ref, kg_ref, kc_ref.at[0], vc_ref.at[0], acc_ref)


def compress_pages(cache, layer, page_table, cw):
    nbatch, n_pages = page_table.shape
    npg = min(PAGES_PER_STEP, n_pages)
    nb = npg * BLOCKS_PER_PAGE
    cache4 = cache.reshape(cache.shape[0], cache.shape[1], PAGE_SIZE * QUARTERS, LANES)
    page_spec = lambda p: pl.BlockSpec((1, 1, PAGE_SIZE * QUARTERS, LANES),
                                       lambda b, c, pt: (layer, pt[b, c * npg + p], 0, 0))
    wspecs = _cmp_weight_specs(3)
    out = jax.ShapeDtypeStruct((nbatch, n_pages * BLOCKS_PER_PAGE, NSA_KV_DIM), F32)
    grid_spec = pltpu.PrefetchScalarGridSpec(
        num_scalar_prefetch=1,
        grid=(nbatch, n_pages // npg),
        in_specs=[page_spec(p) for p in range(npg)] + wspecs,
        out_specs=[pl.BlockSpec((1, nb, NSA_KV_DIM), lambda b, c, pt: (b, c, 0))] * 2,
        scratch_shapes=[pltpu.VMEM((npg * PAGE_SIZE * QUARTERS, LANES), F32),
                        pltpu.VMEM((2 * QUARTERS, nb, NSA_CMP_HID), F32)],
    )
    return pl.pallas_call(
        functools.partial(_compress_pages_kernel, npg=npg),
        grid_spec=grid_spec,
        out_shape=[out, out],
        compiler_params=_params(("parallel", "arbitrary")),
        name="compress_pages",
    )(page_table, *([cache4] * npg), cw["pos2"], cw["w1"], cw["w2"], cw["kg"])


def _select_blocks(score, blk, cur, nblk):
    forced = (blk == 0) | (blk == cur) | (blk == cur - 1)
    score = jnp.where(forced, jnp.inf, score)
    score = jnp.where(blk <= cur, score, -jnp.inf)
    rank = jnp.zeros(score.shape, F32)
    for i in range(nblk):
        col = score[:, i:i + 1]
        ahead = (col > score) | ((col == score) & (blk > i))
        rank = rank + jnp.where(ahead, 1.0, 0.0)
    return (rank < float(min(NSA_TOPK, nblk))) & (blk <= cur)


def _alibi_slope(head):
    return 2.0 ** (-8.0 * (head + 1) / NSA_HEADS)


FAR = 1e38
M_INIT = -1e30


def _nsa_prompt_kernel(q_ref, gates_ref, selb_ref, winb_ref, kc_ref, vc_ref, o_ref, *, tq, t_len, band, kchunk):
    qi = pl.program_id(1)
    q0 = qi * tq
    nblk = t_len // NSA_BLOCK
    hd = NSA_HD
    q_pos = q0 + lax.broadcasted_iota(jnp.int32, (tq, 1), 0)
    cur = q_pos // NSA_BLOCK
    blk = lax.broadcasted_iota(jnp.int32, (1, nblk), 1)
    d_cmp = q_pos - ((blk + 1) * NSA_BLOCK - 1)
    d_cmp_f = d_cmp.astype(F32)
    w0 = pl.multiple_of(jnp.maximum(q0 - NSA_WINDOW, 0), tq)
    d_win = q_pos - (w0 + lax.broadcasted_iota(jnp.int32, (1, band), 1))
    d_win_m = jnp.where((d_win >= 0) & (d_win < NSA_WINDOW), d_win.astype(F32), FAR)
    blk_row = lax.broadcasted_iota(jnp.int32, (nblk, kchunk), 0)
    key_col = lax.broadcasted_iota(jnp.int32, (nblk, kchunk), 1)
    key_lane = lax.broadcasted_iota(jnp.int32, (1, kchunk), 1)
    n_chunks = (q0 + tq + kchunk - 1) // kchunk
    for g in range(NSA_KV_HEADS):
        kc = kc_ref[0, :, g * hd:(g + 1) * hd].astype(BF16)
        vc = vc_ref[0, :, g * hd:(g + 1) * hd].astype(BF16)
        k_win = winb_ref[0, pl.ds(w0, band), g * hd:(g + 1) * hd]
        v_win = winb_ref[0, pl.ds(w0, band), NSA_KV_DIM + g * hd:NSA_KV_DIM + (g + 1) * hd]
        qs, slopes, o_cmp = [], [], []
        score = jnp.zeros((tq, nblk), F32)
        for r in range(NSA_GROUP):
            h = g * NSA_GROUP + r
            q = q_ref[0, :, h * hd:(h + 1) * hd]
            slope = _alibi_slope(h)
            p = _masked_softmax(_dot_nt(q, kc) - slope * d_cmp_f, d_cmp >= 0)
            score = score + p
            o_cmp.append(_dot(p.astype(BF16), vc))
            qs.append(q)
            slopes.append(slope)
        sel_f = jnp.where(_select_blocks(score, blk, cur, nblk), 1.0, 0.0).astype(BF16)

        def sel_chunk(ci, carry):
            k0 = pl.multiple_of(ci * kchunk, kchunk)
            k_c = selb_ref[0, pl.ds(k0, kchunk), g * hd:(g + 1) * hd]
            v_c = selb_ref[0, pl.ds(k0, kchunk), NSA_KV_DIM + g * hd:NSA_KV_DIM + (g + 1) * hd]
            expand = ((k0 + key_col) // NSA_BLOCK == blk_row).astype(BF16)
            d = q_pos - (k0 + key_lane)
            d_m = jnp.where((_dot(sel_f, expand) > 0.5) & (d >= 0), d.astype(F32), FAR)
            out = []
            for r in range(NSA_GROUP):
                m_old, l_old, acc = carry[r]
                s = _dot_nt(qs[r], k_c) - slopes[r] * d_m
                m_new = jnp.maximum(m_old, jnp.max(s, axis=-1, keepdims=True))
                alpha = jnp.exp(m_old - m_new)
                p = jnp.exp(s - m_new)
                out.append((m_new, alpha * l_old + jnp.sum(p, axis=-1, keepdims=True),
                            alpha * acc + _dot(p.astype(BF16), v_c)))
            return tuple(out)

        init = tuple((jnp.full((tq, 1), M_INIT, F32), jnp.zeros((tq, 1), F32), jnp.zeros((tq, hd), F32))
                     for _ in range(NSA_GROUP))
        sel_state = lax.fori_loop(0, n_chunks, sel_chunk, init)
        for r in range(NSA_GROUP):
            h = g * NSA_GROUP + r
            _, l_sel, acc_sel = sel_state[r]
            o_sel = acc_sel / jnp.maximum(l_sel, 1e-30)
            s = _dot_nt(qs[r], k_win) - slopes[r] * d_win_m
            p = jnp.exp(s - jnp.maximum(jnp.max(s, axis=-1, keepdims=True), M_INIT))
            o_win = _dot(p.astype(BF16), v_win) / jnp.maximum(jnp.sum(p, axis=-1, keepdims=True), 1e-30)
            gate = lambda b: gates_ref[0, :, b * NSA_HEADS + h:b * NSA_HEADS + h + 1]
            o = gate(0) * o_cmp[r] + gate(1) * o_sel + gate(2) * o_win
            o_ref[0, :, h * hd:(h + 1) * hd] = o.astype(o_ref.dtype)


def nsa_prompt_attention(qn, gates, sel_b, win_b, k_cmp, v_cmp, tq=128):
    b, t, _ = qn.shape
    tq = min(tq, t)
    band = min(NSA_WINDOW + tq, t)
    nblk = t // NSA_BLOCK
    full = lambda w, n: pl.BlockSpec((1, n, w), lambda i, j: (i, 0, 0))
    tile = lambda w: pl.BlockSpec((1, tq, w), lambda i, j: (i, j, 0))
    return pl.pallas_call(
        functools.partial(_nsa_prompt_kernel, tq=tq, t_len=t, band=band, kchunk=min(256, t)),
        grid=(b, t // tq),
        in_specs=[tile(NSA_Q_DIM), tile(GATE_W), full(2 * NSA_KV_DIM, t), full(2 * NSA_KV_DIM, t),
                  full(NSA_KV_DIM, nblk), full(NSA_KV_DIM, nblk)],
        out_specs=tile(NSA_Q_DIM),
        out_shape=jax.ShapeDtypeStruct((b, t, NSA_Q_DIM), BF16),
        compiler_params=_params(("parallel", "arbitrary")),
        name="nsa_prompt_attention",
    )(qn, gates, sel_b, win_b, k_cmp, v_cmp)


def _head_group_mask(nrows):
    row = lax.broadcasted_iota(jnp.int32, (nrows, 1), 0)
    return [(row // NSA_GROUP) == g for g in range(NSA_KV_HEADS)]


def _slope_column(nrows, head0=0):
    row = lax.broadcasted_iota(jnp.int32, (nrows, 1), 0)
    col = jnp.zeros((nrows, 1), F32)
    for h in range(NSA_HEADS):
        col = jnp.where(row + head0 == h, 2.0 ** (-8.0 * (h + 1) / NSA_HEADS), col)
    return col


def _nsa_sample1_kernel(q_ref, kc_ref, vc_ref, wcache_ref, wnew_ref, ocmp_ref, owin_ref, idx_ref, *,
                        nblk, nblk_pad, past, w_buf):
    hd = NSA_HD
    q = q_ref[0]
    groups = _head_group_mask(NSA_HEADS)
    slope = _slope_column(NSA_HEADS)
    q_pos = past
    cur = q_pos // NSA_BLOCK
    blk = lax.broadcasted_iota(jnp.int32, (1, nblk_pad), 1)
    d_cmp = q_pos - ((blk + 1) * NSA_BLOCK - 1)
    s = jnp.zeros((NSA_HEADS, nblk_pad), F32)
    for g in range(NSA_KV_HEADS):
        s = s + jnp.where(groups[g], _dot_nt(q, kc_ref[0, :, g * hd:(g + 1) * hd]), 0.0)
    p = _masked_softmax(s - slope * d_cmp.astype(F32), (d_cmp >= 0) & (blk < nblk))
    o = jnp.zeros((NSA_HEADS, hd), F32)
    for g in range(NSA_KV_HEADS):
        o = o + jnp.where(groups[g], _dot(p, vc_ref[0, :, g * hd:(g + 1) * hd]), 0.0)
    ocmp_ref[0] = o
    gsel = (lax.broadcasted_iota(jnp.int32, (8, NSA_HEADS), 1) // NSA_GROUP
            == lax.broadcasted_iota(jnp.int32, (8, NSA_HEADS), 0)).astype(F32)
    score = _dot(gsel, p, HIGHEST)
    forced = (blk == 0) | (blk == cur) | (blk == cur - 1)
    score = jnp.where(forced, jnp.inf, score)
    score = jnp.where((blk <= cur) & (blk < nblk), score, -jnp.inf)
    blk_f = blk.astype(F32)
    lane_k = lax.broadcasted_iota(jnp.int32, (1, NSA_TOPK), 1)
    idx = jnp.zeros((8, NSA_TOPK), F32)
    for k in range(NSA_TOPK):
        m = jnp.max(score, axis=-1, keepdims=True)
        first = jnp.min(jnp.where(score == m, blk_f, 1e9), axis=-1, keepdims=True)
        idx = jnp.where(lane_k == k, first, idx)
        score = jnp.where(blk_f == first, -jnp.inf, score)
    idx_ref[0] = idx.astype(jnp.int32)
    i_buf = lax.broadcasted_iota(jnp.int32, (1, w_buf), 1)
    win_pos = past - w_buf + i_buf
    d_win = q_pos - win_pos
    m_win = (d_win >= 0) & (d_win < NSA_WINDOW) & (win_pos >= 0)
    s = jnp.zeros((NSA_HEADS, w_buf), F32)
    k_new = jnp.zeros((NSA_HEADS, hd), F32)
    v_new = jnp.zeros((NSA_HEADS, hd), F32)
    for g in range(NSA_KV_HEADS):
        s = s + jnp.where(groups[g], _dot_nt(q, wcache_ref[0, 0, :, g * hd:(g + 1) * hd]), 0.0)
        k_new = k_new + jnp.where(groups[g], wnew_ref[0, :, g * hd:(g + 1) * hd], 0.0)
        v_new = v_new + jnp.where(groups[g], wnew_ref[0, :, NSA_KV_DIM + g * hd:NSA_KV_DIM + (g + 1) * hd], 0.0)
    s = jnp.where(m_win, s - slope * d_win.astype(F32), -jnp.inf)
    s_new = jnp.sum(q * k_new, axis=-1, keepdims=True)
    m = jnp.maximum(jnp.max(s, axis=-1, keepdims=True), s_new)
    e = jnp.where(m_win, jnp.exp(s - m), 0.0)
    e_new = jnp.exp(s_new - m)
    denom = jnp.maximum(jnp.sum(e, axis=-1, keepdims=True) + e_new, 1e-30)
    o = e_new * v_new
    for g in range(NSA_KV_HEADS):
        o = o + jnp.where(groups[g], _dot(e, wcache_ref[0, 0, :, NSA_KV_DIM + g * hd:NSA_KV_DIM + (g + 1) * hd]), 0.0)
    owin_ref[0] = o / denom


def nsa_sample_attention1(q, k_cmp, v_cmp, cache_win, layer, win_new, nblk, past):
    nbatch = q.shape[0]
    nblk_pad = k_cmp.shape[1]
    w_buf = cache_win.shape[2]
    hq = pl.BlockSpec((1, NSA_HEADS, NSA_HD), lambda b: (b, 0, 0))
    return pl.pallas_call(
        functools.partial(_nsa_sample1_kernel, nblk=nblk, nblk_pad=nblk_pad, past=past, w_buf=w_buf),
        grid=(nbatch,),
        in_specs=[hq, pl.BlockSpec((1, nblk_pad, NSA_KV_DIM), lambda b: (b, 0, 0)),
                  pl.BlockSpec((1, nblk_pad, NSA_KV_DIM), lambda b: (b, 0, 0)),
                  pl.BlockSpec((1, 1, w_buf, 2 * NSA_KV_DIM), lambda b: (layer, b, 0, 0)),
                  pl.BlockSpec((1, 1, 2 * NSA_KV_DIM), lambda b: (b, 0, 0))],
        out_specs=[hq, hq, pl.BlockSpec((1, 8, NSA_TOPK), lambda b: (b, 0, 0))],
        out_shape=[jax.ShapeDtypeStruct((nbatch, NSA_HEADS, NSA_HD), F32),
                   jax.ShapeDtypeStruct((nbatch, NSA_HEADS, NSA_HD), F32),
                   jax.ShapeDtypeStruct((nbatch, 8, NSA_TOPK), jnp.int32)],
        compiler_params=_params(("parallel",)),
        name="nsa_sample_attention1",
    )(q, k_cmp, v_cmp, cache_win, win_new)


def _nsa_sample2_kernel(idx_ref, pt_ref, q_ref, b0_ref, b1_ref, b2_ref, b3_ref, new_ref, ocmp_ref, owin_ref, gates_ref,
                        o_ref, m_scr, l_scr, acc_scr, *, past, n_cache_blk):
    hd = NSA_HD
    b = pl.program_id(0)
    k = pl.program_id(1)
    blocks = (b0_ref, b1_ref, b2_ref, b3_ref)

    @pl.when(k == 0)
    def _():
        m_scr[...] = jnp.full(m_scr.shape, -jnp.inf, F32)
        l_scr[...] = jnp.zeros(l_scr.shape, F32)
        acc_scr[...] = jnp.zeros(acc_scr.shape, F32)

    q_pos = past
    lane = lax.broadcasted_iota(jnp.int32, (1, PAGE_SIZE), 1)
    for g in range(NSA_KV_HEADS):
        j = idx_ref[(b * 8 + g) * NSA_TOPK + k]
        slope = _slope_column(8, head0=g * NSA_GROUP)
        q = q_ref[0, g]
        k_t = blocks[g][0, 0, 0, 0]
        v_t = blocks[g][0, 0, 1, 0]
        d = q_pos - ((j // BLOCKS_PER_PAGE) * PAGE_SIZE + lane)
        in_block = (lane // NSA_BLOCK) == (j % BLOCKS_PER_PAGE)
        valid = (d >= 0) & in_block & (j < n_cache_blk)
        s = jnp.where(valid, _dot(q, k_t) - slope * d.astype(F32), -jnp.inf)
        k_new = new_ref[0, :, g * hd:(g + 1) * hd]
        v_new = new_ref[0, :, NSA_KV_DIM + g * hd:NSA_KV_DIM + (g + 1) * hd]
        new_ok = j == q_pos // NSA_BLOCK
        s_new = jnp.where(new_ok, jnp.sum(q * k_new, axis=-1, keepdims=True), -jnp.inf)
        m_old = m_scr[g]
        m_new = jnp.maximum(jnp.maximum(m_old, jnp.max(s, axis=-1, keepdims=True)), s_new)
        m_safe = jnp.where(m_new > -jnp.inf, m_new, 0.0)
        alpha = jnp.exp(m_old - m_safe)
        p = jnp.where(valid, jnp.exp(s - m_safe), 0.0)
        p_new = jnp.where(new_ok, jnp.exp(s_new - m_safe), 0.0)
        l_scr[g] = alpha * l_scr[g] + jnp.sum(p, axis=-1, keepdims=True) + p_new
        acc_scr[g] = alpha * acc_scr[g] + _dot_nt(p, v_t) + p_new * v_new
        m_scr[g] = m_new

    @pl.when(k == NSA_TOPK - 1)
    def _():
        for g in range(NSA_KV_HEADS):
            o_sel = acc_scr[g] / jnp.maximum(l_scr[g], 1e-30)
            gt = gates_ref[0, g]
            o_ref[0, g] = gt[:, 0:1] * ocmp_ref[0, g] + gt[:, 1:2] * o_sel + gt[:, 2:3] * owin_ref[0, g]


def nsa_sample_attention2(idx, page_table, q8, cache_sel, layer, sel_new, o_cmp8, o_win8, gates8, past):
    nbatch = q8.shape[0]
    n_cache_blk = past // NSA_BLOCK
    cache_t = jnp.transpose(cache_sel, (0, 1, 3, 4, 5, 2))

    def blk_map(g):
        def index_map(b, k, idx_ref, pt_ref):
            j = jnp.minimum(idx_ref[(b * 8 + g) * NSA_TOPK + k], n_cache_blk - 1)
            return (layer, pt_ref[b, j // BLOCKS_PER_PAGE], 0, g, 0, 0)
        return index_map

    g4 = lambda w: pl.BlockSpec((1, NSA_KV_HEADS, 8, w), lambda b, k, i, p: (b, 0, 0, 0))
    grid_spec = pltpu.PrefetchScalarGridSpec(
        num_scalar_prefetch=2,
        grid=(nbatch, NSA_TOPK),
        in_specs=[g4(NSA_HD)] + [pl.BlockSpec((1, 1, 2, 1, NSA_HD, PAGE_SIZE), blk_map(g)) for g in range(NSA_KV_HEADS)]
                 + [pl.BlockSpec((1, 1, 2 * NSA_KV_DIM), lambda b, k, i, p: (b, 0, 0)), g4(NSA_HD), g4(NSA_HD), g4(LANES)],
        out_specs=g4(NSA_HD),
        scratch_shapes=[pltpu.VMEM((NSA_KV_HEADS, 8, 1), F32), pltpu.VMEM((NSA_KV_HEADS, 8, 1), F32),
                        pltpu.VMEM((NSA_KV_HEADS, 8, NSA_HD), F32)],
    )
    return pl.pallas_call(
        functools.partial(_nsa_sample2_kernel, past=past, n_cache_blk=n_cache_blk),
        grid_spec=grid_spec,
        out_shape=jax.ShapeDtypeStruct((nbatch, NSA_KV_HEADS, 8, NSA_HD), F32),
        compiler_params=_params(("parallel", "arbitrary")),
        name="nsa_sample_attention2",
    )(idx, page_table, q8, cache_t, cache_t, cache_t, cache_t, sel_new, o_cmp8, o_win8, gates8)


GLA_Q0, GLA_K0, GLA_V0 = 0, GLA_QK_DIM, 2 * GLA_QK_DIM
GLA_A0 = GLA_V0 + GLA_V_DIM
GLA_G0 = GLA_A0 + GLA_A_W


def _gla_kernel(z_ref, s0_ref, a2_ref, ab_ref, ng_ref, o_ref, st_ref, s_scr, *, tt, t_valid, n_t):
    ti = pl.program_id(1)
    c = GLA_CHUNK

    @pl.when(ti == 0)
    def _():
        for h in range(GLA_HEADS):
            s_scr[h] = s0_ref[0, h].T

    a_lr = z_ref[0, :, GLA_A0:GLA_A0 + GLA_A_W].astype(BF16)
    pre = _dot(a_lr, a2_ref[...]) + ab_ref[...]
    log_a = (jnp.minimum(pre, 0.0) - jnp.log(1.0 + jnp.exp(-jnp.abs(pre)))) / GLA_TAU
    row = ti * tt + lax.broadcasted_iota(jnp.int32, (tt, 1), 0)
    log_a = jnp.where(row < t_valid, log_a, 0.0)
    tri = lax.broadcasted_iota(jnp.int32, (c, c), 0) >= lax.broadcasted_iota(jnp.int32, (c, c), 1)
    tri_f = tri.astype(F32)
    for ci in range(tt // c):
        rows = slice(ci * c, (ci + 1) * c)
        for h in range(GLA_HEADS):
            ks = slice(h * GLA_DK, (h + 1) * GLA_DK)
            vs = slice(h * GLA_DV, (h + 1) * GLA_DV)
            cum = _dot(tri_f, log_a[rows, ks], HIGHEST)
            last = cum[c - 1:c, :]
            q_c = z_ref[0, rows, GLA_Q0 + h * GLA_DK:GLA_Q0 + (h + 1) * GLA_DK] * (GLA_DK ** -0.5)
            k_c = z_ref[0, rows, GLA_K0 + h * GLA_DK:GLA_K0 + (h + 1) * GLA_DK]
            v_c = z_ref[0, rows, GLA_V0 + h * GLA_DV:GLA_V0 + (h + 1) * GLA_DV].astype(BF16)
            q_dec = (q_c * jnp.exp(cum)).astype(BF16)
            k_inv = (k_c * jnp.exp(-cum)).astype(BF16)
            k_dec = (k_c * jnp.exp(last - cum)).astype(BF16)
            att = jnp.where(tri, _dot_nt(q_dec, k_inv), 0.0)
            st = s_scr[h]
            o = _dot(att.astype(BF16), v_c) + _dot_nt(q_dec, st.astype(BF16))
            s_scr[h] = st * jnp.exp(last) + _dot_tn(v_c, k_dec)
            o = o * lax.rsqrt(jnp.mean(o * o, axis=-1, keepdims=True) + NORM_EPS) * ng_ref[...]
            gate = z_ref[0, rows, GLA_G0 + h * GLA_DV:GLA_G0 + (h + 1) * GLA_DV]
            o_ref[0, rows, vs] = (o * (gate * _sigmoid(gate))).astype(o_ref.dtype)

    @pl.when(ti == n_t - 1)
    def _():
        for h in range(GLA_HEADS):
            st_ref[0, h] = s_scr[h].T


def gla_mixer(z_gla, s0, a2p, ab, ng, t_valid):
    b, t, _ = z_gla.shape
    tt = min(t, 512)
    n_t = t // tt
    return pl.pallas_call(
        functools.partial(_gla_kernel, tt=tt, t_valid=t_valid, n_t=n_t),
        grid=(b, n_t),
        in_specs=[pl.BlockSpec((1, tt, GLA_W), lambda i, j: (i, j, 0)),
                  pl.BlockSpec((1, GLA_HEADS, GLA_DK, GLA_DV), lambda i, j: (i, 0, 0, 0)),
                  pl.BlockSpec((GLA_A_W, GLA_QK_DIM), lambda i, j: (0, 0)),
                  pl.BlockSpec((1, GLA_QK_DIM), lambda i, j: (0, 0)),
                  pl.BlockSpec((1, GLA_DV), lambda i, j: (0, 0))],
        out_specs=[pl.BlockSpec((1, tt, GLA_V_DIM), lambda i, j: (i, j, 0)),
                   pl.BlockSpec((1, GLA_HEADS, GLA_DK, GLA_DV), lambda i, j: (i, 0, 0, 0))],
        out_shape=[jax.ShapeDtypeStruct((b, t, GLA_V_DIM), BF16),
                   jax.ShapeDtypeStruct((b, GLA_HEADS, GLA_DK, GLA_DV), F32)],
        scratch_shapes=[pltpu.VMEM((GLA_HEADS, GLA_DV, GLA_DK), F32)],
        compiler_params=_params(("parallel", "arbitrary")),
        name="gla_mixer",
    )(z_gla, s0, a2p, ab, ng)


RW_R0, RW_K0, RW_V0 = 0, RW_DIM, 2 * RW_DIM
RW_WL0 = 3 * RW_DIM
RW_AL0 = RW_WL0 + RW_LR_W
RW_GL0 = RW_AL0 + RW_LR_W


def _rw_prep_kernel(z_ref, sh_ref, mu_ref, w0_ref, w2_ref, a0_ref, a2_ref, g2_ref, kk_ref, ka_ref, rk_ref, seg_ref,
                    r_ref, dec_ref, k_ref, v_ref, nkk_ref, kka_ref, g_ref, bonus_ref):
    z = z_ref[...]
    xm = z + (sh_ref[...] - z) * mu_ref[...]
    r = xm[:, RW_R0:RW_R0 + RW_DIM]
    k = xm[:, RW_K0:RW_K0 + RW_DIM]
    v = xm[:, RW_V0:RW_V0 + RW_DIM]
    wl = xm[:, RW_WL0:RW_WL0 + RW_LR_W]
    al = xm[:, RW_AL0:RW_AL0 + RW_LR_W]
    gl = xm[:, RW_GL0:RW_GL0 + RW_LR_W]
    w = -_softplus(-(w0_ref[...] + _dot(jnp.tanh(wl).astype(BF16), w2_ref[...]))) - 0.5
    a = _sigmoid(a0_ref[...] + _dot(al.astype(BF16), a2_ref[...]))
    kk = k * kk_ref[...]
    norm = jnp.sqrt(_dot(kk * kk, seg_ref[...], HIGHEST))
    kk = kk / jnp.maximum(norm, 1e-12)
    k2 = k * (1.0 + (a - 1.0) * ka_ref[...])
    r_ref[...] = r
    dec_ref[...] = -jnp.exp(w)
    k_ref[...] = k2
    v_ref[...] = v
    nkk_ref[...] = -kk
    kka_ref[...] = kk * a
    g_ref[...] = _dot(_sigmoid(gl).astype(BF16), g2_ref[...])
    bonus_ref[...] = _dot(r * k2 * rk_ref[...], seg_ref[...], HIGHEST) * v


def rw_prep(z_rw, shifted, rp, tm=512):
    m = z_rw.shape[0]
    tm = min(tm, m)
    row = lambda w: pl.BlockSpec((tm, w), lambda i: (i, 0))
    const = lambda r, w: pl.BlockSpec((r, w), lambda i: (0, 0))
    out = jax.ShapeDtypeStruct((m, RW_DIM), F32)
    return pl.pallas_call(
        _rw_prep_kernel,
        grid=(m // tm,),
        in_specs=[row(RW_W), row(RW_W), const(1, RW_W), const(1, RW_DIM), const(RW_LR_W, RW_DIM), const(1, RW_DIM),
                  const(RW_LR_W, RW_DIM), const(RW_LR_W, RW_DIM), const(1, RW_DIM), const(1, RW_DIM), const(1, RW_DIM),
                  const(RW_DIM, RW_DIM)],
        out_specs=[row(RW_DIM)] * 8,
        out_shape=[out] * 8,
        compiler_params=_params(("parallel",)),
        name="rw_prep",
    )(z_rw, shifted, rp["mu"], rp["w0"], rp["w2"], rp["a0"], rp["a2"], rp["g2"], rp["kk"], rp["ka"], rp["rk"], rp["seg"])


RW_CHUNK = 64
RW_CHUNK_LOG2 = 6


def _rw_scan_kernel(r_ref, dec_ref, k_ref, v_ref, nkk_ref, kka_ref, g_ref, bonus_ref, lng_ref, lnb_ref, s0_ref,
                    o_ref, st_ref, s_scr, *, tt, n_t):
    ti = pl.program_id(1)

    @pl.when(ti == 0)
    def _():
        s_scr[...] = s0_ref[0]

    c = RW_CHUNK
    row_i = lax.broadcasted_iota(jnp.int32, (c, c), 0)
    col_i = lax.broadcasted_iota(jnp.int32, (c, c), 1)
    lower = row_i >= col_i
    strict = row_i > col_i
    lower_b = lower.astype(BF16)
    eye_f = (row_i == col_i).astype(F32)
    heads = [slice(h * RW_HD, (h + 1) * RW_HD) for h in range(RW_HEADS)]
    sp = _split2
    mm = functools.partial(_dot3, _dot)
    mm_nt = functools.partial(_dot3, _dot_nt)
    mm_tn = functools.partial(_dot3, _dot_tn)

    def chunk(i, carry):
        rows = pl.ds(pl.multiple_of(i * c, c), c)
        for h, hs in enumerate(heads):
            lw = dec_ref[0, rows, hs]
            cum = sum(_dot(lower_b, part) for part in _split3(lw))
            g_inv = jnp.exp(-cum)
            g_last = jnp.exp(cum[c - 1:c, :])
            at_f = kka_ref[0, rows, hs] * g_inv
            kt_f = k_ref[0, rows, hs] * g_inv
            bt = sp(nkk_ref[0, rows, hs] * jnp.exp(cum - lw))
            at, kt = sp(at_f), sp(kt_f)
            rt = sp(r_ref[0, rows, hs] * jnp.exp(cum))
            v = sp(v_ref[0, rows, hs])
            ma = jnp.where(strict, mm_nt(bt, at), 0.0)
            mk = jnp.where(strict, mm_nt(bt, kt), 0.0)
            na = jnp.where(lower, mm_nt(rt, at), 0.0)
            nk = jnp.where(lower, mm_nt(rt, kt), 0.0)
            inv = eye_f + ma
            power = sp(ma)
            for _ in range(RW_CHUNK_LOG2 - 1):
                power = sp(mm(power, power))
                inv = inv + mm(sp(inv), power)
            s0_f = s_scr[h]
            s0 = sp(s0_f)
            u_f = mm(sp(inv), sp(mm_nt(bt, s0) + mm(sp(mk), v)))
            u = sp(u_f)
            o = mm_nt(rt, s0) + mm(sp(na), u) + mm(sp(nk), v)
            s_scr[h] = s0_f * g_last + mm_tn(u, sp(at_f * g_last)) + mm_tn(v, sp(kt_f * g_last))
            mean = jnp.mean(o, axis=-1, keepdims=True)
            cen = o - mean
            var = jnp.mean(cen * cen, axis=-1, keepdims=True)
            y = cen * lax.rsqrt(var + RW_GN_EPS) * lng_ref[:, hs] + lnb_ref[:, hs]
            o_ref[0, rows, hs] = (y + bonus_ref[0, rows, hs]) * g_ref[0, rows, hs]
        return carry

    lax.fori_loop(0, tt // c, chunk, 0)

    @pl.when(ti == n_t - 1)
    def _():
        st_ref[0] = s_scr[...]


def rw_scan(r, dec, k, v, nkk, kka, g, bonus, ln_g, ln_b, s0):
    b, t, _ = r.shape
    tt = min(t, 256)
    n_t = t // tt
    tile = pl.BlockSpec((1, tt, RW_DIM), lambda i, j: (i, j, 0))
    vec = pl.BlockSpec((1, RW_DIM), lambda i, j: (0, 0))
    st = pl.BlockSpec((1, RW_HEADS, RW_HD, RW_HD), lambda i, j: (i, 0, 0, 0))
    return pl.pallas_call(
        functools.partial(_rw_scan_kernel, tt=tt, n_t=n_t),
        grid=(b, n_t),
        in_specs=[tile] * 8 + [vec, vec, st],
        out_specs=[tile, st],
        out_shape=[jax.ShapeDtypeStruct((b, t, RW_DIM), F32), jax.ShapeDtypeStruct(s0.shape, F32)],
        scratch_shapes=[pltpu.VMEM((RW_HEADS, RW_HD, RW_HD), F32)],
        compiler_params=_params(("parallel", "arbitrary")),
        name="rw_scan",
    )(r, dec, k, v, nkk, kka, g, bonus, ln_g, ln_b, s0)


def _pad_last(w, width):
    return jnp.pad(w, [(0, 0)] * (w.ndim - 1) + [(0, width - w.shape[-1])])


def _pad_rows(w, rows):
    return jnp.pad(w, [(0, rows - w.shape[0])] + [(0, 0)] * (w.ndim - 1))


def _split_last(z, sizes):
    return jnp.split(z, [int(c) for c in np.cumsum(np.array(sizes))[:-1]], axis=-1)


def _rw_padded(cols):
    r, k, v, wl, al, gl = _split_last(cols, RW_SIZES)
    return jnp.concatenate([r, k, v, _pad_last(wl, RW_LR_W), _pad_last(al, RW_LR_W), _pad_last(gl, RW_LR_W)], axis=-1)


def _rw_unpadded(cols):
    return jnp.concatenate([cols[..., :RW_WL0], cols[..., RW_WL0:RW_WL0 + RW_DECAY_RANK],
                            cols[..., RW_AL0:RW_AL0 + RW_A_RANK], cols[..., RW_GL0:RW_GL0 + RW_G_RANK]], axis=-1)


def _layer_params(l, p):
    q, cmp_, sel, win, gate, gq, gk, gv, ga, gg, rw, mg = _split_last(p["w_in"][l], IN_SIZES)
    seg = np.kron(np.eye(RW_HEADS, dtype=np.float32), np.ones((RW_HD, RW_HD), np.float32))
    row = lambda v: v.reshape(1, -1)
    return {
        "norm1_g": p["norm1_g"][l], "norm2_g": p["norm2_g"][l],
        "w_nsa": jnp.concatenate([q, cmp_, sel, win, _pad_last(gate, GATE_W)], axis=1).astype(BF16),
        "w_gla": jnp.concatenate([gq, gk, gv, _pad_last(ga, GLA_A_W), gg], axis=1).astype(BF16),
        "w_rw": _rw_padded(rw).astype(BF16),
        "w_mg": mg.astype(BF16),
        "qk_g": p["nsa_qk_g"][l],
        "cw": {"pos2": jnp.concatenate([p["cmp_pos"][l]] * 2, axis=-1),
               "w1": p["cmp_w1"][l].reshape(2, NSA_BLOCK, NSA_HD, NSA_CMP_HID).astype(BF16),
               "w2": p["cmp_w2"][l].astype(BF16),
               "kg": row(p["nsa_qk_g"][l, 1])},
        "gla_a2": _pad_rows(p["gla_a2"][l], GLA_A_W).astype(BF16),
        "gla_ab": row(p["gla_a_b"][l]), "gla_ng": row(p["gla_norm_g"][l]),
        "rp": {"mu": row(_rw_padded(p["rw_mu"][l])), "w0": row(p["rw_w0"][l]),
               "w2": _pad_rows(p["rw_w2"][l], RW_LR_W).astype(BF16), "a0": row(p["rw_a0"][l]),
               "a2": _pad_rows(p["rw_a2"][l], RW_LR_W).astype(BF16),
               "g2": _pad_rows(p["rw_g2"][l], RW_LR_W).astype(BF16),
               "kk": row(p["rw_kk"][l]), "ka": row(p["rw_ka"][l]), "rk": row(p["rw_rk"][l]), "seg": jnp.asarray(seg)},
        "ln_g": row(p["rw_ln_g"][l]), "ln_b": row(p["rw_ln_b"][l]),
        "nsa_up": p["nsa_up"][l].astype(BF16), "gla_up": p["gla_up"][l].astype(BF16), "rw_up": p["rw_up"][l].astype(BF16),
        "w_out": p["w_out"][l].astype(BF16), "mlp_w1": p["mlp_w1"][l].astype(BF16), "mlp_w2": p["mlp_w2"][l].astype(BF16),
    }


def _project_in(x2, lp, tm):
    hn = rmsnorm_rows(x2, lp["norm1_g"], tm)
    return tuple(matmul(hn, lp[w], tm=tm) for w in ("w_nsa", "w_gla", "w_rw", "w_mg"))


def _project_out(x2, o_nsa, o_gla, o_rw, z_mg, lp, tm):
    merged = merge_branches(o_nsa, o_gla, o_rw, lp["nsa_up"], lp["gla_up"], lp["rw_up"], z_mg, tm=tm)
    x1 = matmul(merged, lp["w_out"], "residual", x2, tm=tm)
    h2 = rmsnorm_rows(x1, lp["norm2_g"], tm)
    u = matmul(h2, lp["mlp_w1"], "relu2", tm=tm)
    return matmul(u, lp["mlp_w2"], "residual", x1, tm=tm)


def _kv_rows(rows, b, t):
    return rows.reshape(b, t, 2, NSA_KV_HEADS, NSA_HD)


def _prompt_layer(x, lp):
    b, t, d = x.shape
    m = b * t
    tm = 512
    x2 = x.reshape(m, d)
    z_nsa, z_gla, z_rw, z_mg = _project_in(x2, lp, tm)
    qn, cmp_rows, sel_rows, win_rows, sel_b, win_b, gates = nsa_prep(z_nsa, lp["qk_g"])
    nblk = t // NSA_BLOCK
    k_cmp, v_cmp = compress_rows(cmp_rows, lp["cw"], nb=32)
    per_b = lambda a: a.reshape(b, t, a.shape[-1])
    o_nsa = nsa_prompt_attention(per_b(qn), per_b(gates), per_b(sel_b), per_b(win_b),
                                 k_cmp.reshape(b, nblk, NSA_KV_DIM), v_cmp.reshape(b, nblk, NSA_KV_DIM))
    o_gla, gla_st = gla_mixer(per_b(z_gla), jnp.zeros((b, GLA_HEADS, GLA_DK, GLA_DV), F32),
                              lp["gla_a2"], lp["gla_ab"], lp["gla_ng"], t_valid=t)
    z_rw3 = per_b(z_rw)
    shifted = jnp.concatenate([jnp.zeros((b, 1, RW_W), F32), z_rw3[:, :-1]], axis=1).reshape(m, RW_W)
    prep = rw_prep(z_rw, shifted, lp["rp"])
    o_rw, rw_st = rw_scan(*(per_b(a) for a in prep), lp["ln_g"], lp["ln_b"],
                          jnp.zeros((b, RW_HEADS, RW_HD, RW_HD), F32))
    y = _project_out(x2, o_nsa.reshape(m, -1), o_gla.reshape(m, -1), o_rw.reshape(m, -1), z_mg, lp, tm)
    w_keep = min(NSA_WINDOW, t)
    return (y.reshape(b, t, d), _kv_rows(cmp_rows, b, t), _kv_rows(sel_rows, b, t), _kv_rows(win_rows, b, t)[:, t - w_keep:],
            gla_st, rw_st, _rw_unpadded(z_rw3[:, -1]))


SAMPLE_ROWS = SUBLANES_BF16


def _sample_layer(x2, lp, layer, caches, page_table, nbatch):
    cache_cmp, cache_sel, cache_win, state_gla, state_rwkv, state_shift = caches
    tm = SAMPLE_ROWS
    past = page_table.shape[1] * PAGE_SIZE
    nblk = -(-(past + 1) // NSA_BLOCK)
    z_nsa, z_gla, z_rw, z_mg = _project_in(x2, lp, tm)
    qn, cmp_rows, sel_rows, win_rows, _, _, gates = nsa_prep(z_nsa, lp["qk_g"])
    kc_past, vc_past = compress_pages(cache_cmp, layer, page_table, lp["cw"])
    last_blk = jnp.pad(cmp_rows[:nbatch, None, :], ((0, 0), (0, NSA_BLOCK - 1), (0, 0))).reshape(nbatch * NSA_BLOCK, -1)
    kc_last, vc_last = compress_rows(last_blk, lp["cw"], nb=nbatch)
    nblk_pad = -(-nblk // 8) * 8
    with_last = lambda past_s, last_s: jnp.pad(jnp.concatenate([past_s, last_s[:, None, :]], axis=1),
                                               ((0, 0), (0, nblk_pad - nblk), (0, 0)))
    q16 = qn[:nbatch].astype(F32).reshape(nbatch, NSA_HEADS, NSA_HD)
    o_cmp, o_win, idx = nsa_sample_attention1(
        q16, with_last(kc_past, kc_last), with_last(vc_past, vc_last),
        cache_win.reshape(cache_win.shape[0], nbatch, cache_win.shape[2], 2 * NSA_KV_DIM), layer,
        win_rows[:nbatch, None, :], nblk, past)
    grp8 = lambda a: jnp.pad(a.reshape(nbatch, NSA_KV_HEADS, NSA_GROUP, a.shape[-1]),
                             ((0, 0), (0, 0), (0, 8 - NSA_GROUP), (0, 0)))
    gates3 = gates[:nbatch, :3 * NSA_HEADS].reshape(nbatch, 3, NSA_HEADS).transpose(0, 2, 1)
    o8 = nsa_sample_attention2(idx.reshape(-1), page_table, grp8(q16), cache_sel, layer, sel_rows[:nbatch, None, :],
                               grp8(o_cmp), grp8(o_win), grp8(_pad_last(gates3, LANES)), past)
    o_nsa = _pad_rows(o8[:, :, :NSA_GROUP].reshape(nbatch, NSA_Q_DIM), tm)
    z_gla_c = jnp.pad(z_gla[:nbatch, None, :], ((0, 0), (0, GLA_CHUNK - 1), (0, 0)))
    o_gla, gla_st = gla_mixer(z_gla_c, state_gla[layer], lp["gla_a2"], lp["gla_ab"], lp["gla_ng"], t_valid=1)
    o_gla = _pad_rows(o_gla[:, 0], tm)
    shifted = _pad_rows(_rw_padded(state_shift[layer]), tm)
    prep = rw_prep(z_rw, shifted, lp["rp"])
    prep = [jnp.pad(a[:nbatch, None, :], ((0, 0), (0, RW_CHUNK - 1), (0, 0))) for a in prep]
    o_rw, rw_st = rw_scan(*prep, lp["ln_g"], lp["ln_b"], state_rwkv[layer])
    o_rw = _pad_rows(o_rw[:, 0], tm)
    y = _project_out(x2, o_nsa, o_gla, o_rw, z_mg, lp, tm)
    return (y, _kv_rows(cmp_rows[:nbatch], nbatch, 1), _kv_rows(sel_rows[:nbatch], nbatch, 1),
            _kv_rows(win_rows[:nbatch], nbatch, 1), gla_st, rw_st, _rw_unpadded(z_rw[:nbatch]))


def kernel(x_prompt, x_sample, cache_cmp_kv, cache_sel_kv, cache_win_kv, state_gla, state_rwkv, state_rwkv_shift,
           page_table, norm1_g, w_in, nsa_qk_g, cmp_pos, cmp_w1, cmp_w2, nsa_up, gla_a2, gla_a_b, gla_norm_g, gla_up,
           rw_mu, rw_w0, rw_w2, rw_a0, rw_a2, rw_g2, rw_kk, rw_ka, rw_rk, rw_ln_g, rw_ln_b, rw_up, w_out, norm2_g,
           mlp_w1, mlp_w2):
    p = dict(norm1_g=norm1_g, w_in=w_in, nsa_qk_g=nsa_qk_g, cmp_pos=cmp_pos, cmp_w1=cmp_w1, cmp_w2=cmp_w2, nsa_up=nsa_up,
             gla_a2=gla_a2, gla_a_b=gla_a_b, gla_norm_g=gla_norm_g, gla_up=gla_up, rw_mu=rw_mu, rw_w0=rw_w0, rw_w2=rw_w2,
             rw_a0=rw_a0, rw_a2=rw_a2, rw_g2=rw_g2, rw_kk=rw_kk, rw_ka=rw_ka, rw_rk=rw_rk, rw_ln_g=rw_ln_g,
             rw_ln_b=rw_ln_b, rw_up=rw_up, w_out=w_out, norm2_g=norm2_g, mlp_w1=mlp_w1, mlp_w2=mlp_w2)
    nbatch = x_sample.shape[0]
    caches = (cache_cmp_kv, cache_sel_kv, cache_win_kv, state_gla, state_rwkv, state_rwkv_shift)
    xp = x_prompt
    xs = _pad_rows(x_sample.reshape(nbatch, D_MODEL), SAMPLE_ROWS)
    outs_p, outs_s = [], []
    for l in range(DEPTH):
        lp = _layer_params(l, p)
        xp, *rest_p = _prompt_layer(xp, lp)
        xs, *rest_s = _sample_layer(xs, lp, l, caches, page_table, nbatch)
        outs_p.append(rest_p)
        outs_s.append(rest_s)
    stack = lambda outs, i: jnp.stack([o[i] for o in outs])
    return ((xp, xs[:nbatch].reshape(x_sample.shape))
            + tuple(stack(outs_p, i) for i in range(6)) + tuple(stack(outs_s, i) for i in range(6)))
```

```python
import functools

import numpy as np
import jax
import jax.numpy as jnp
from jax import lax
from jax.experimental import pallas as pl
from jax.experimental.pallas import tpu as pltpu

F32 = jnp.float32
BF16 = jnp.bfloat16
HIGHEST = lax.Precision.HIGHEST

D_MODEL = 2048
DEPTH = 2
PAGE_SIZE = 128
NSA_HEADS = 16
NSA_KV_HEADS = 4
NSA_GROUP = NSA_HEADS // NSA_KV_HEADS
NSA_HD = 64
NSA_BLOCK = 64
NSA_TOPK = 16
NSA_WINDOW = 512
NSA_CMP_HID = 128
GLA_HEADS = 4
GLA_DK = 64
GLA_DV = 128
GLA_GATE_RANK = 16
GLA_TAU = 16.0
GLA_CHUNK = 64
RW_HEADS = 8
RW_HD = 64
RW_DECAY_RANK = 32
RW_A_RANK = 32
RW_G_RANK = 96
RW_GN_EPS = 64e-5
D_FF = 4 * D_MODEL
NORM_EPS = 1e-6

NSA_Q_DIM = NSA_HEADS * NSA_HD
NSA_KV_DIM = NSA_KV_HEADS * NSA_HD
GLA_QK_DIM = GLA_HEADS * GLA_DK
GLA_V_DIM = GLA_HEADS * GLA_DV
RW_DIM = RW_HEADS * RW_HD
RW_SIZES = (RW_DIM, RW_DIM, RW_DIM, RW_DECAY_RANK, RW_A_RANK, RW_G_RANK)
RW_PROJ = sum(RW_SIZES)
IN_SIZES = (NSA_Q_DIM, 2 * NSA_KV_DIM, 2 * NSA_KV_DIM, 2 * NSA_KV_DIM, 3 * NSA_HEADS,
            GLA_QK_DIM, GLA_QK_DIM, GLA_V_DIM, GLA_GATE_RANK, GLA_V_DIM, RW_PROJ, 3 * D_MODEL)

LANES = 128
SUBLANES_BF16 = 16
VMEM_LIMIT = 56 * 1024 * 1024

GATE_W = LANES
NSA_W = NSA_Q_DIM + 6 * NSA_KV_DIM + GATE_W
GLA_A_W = LANES
GLA_W = 2 * GLA_QK_DIM + 2 * GLA_V_DIM + GLA_A_W
RW_LR_W = LANES
RW_W = 3 * RW_DIM + 3 * RW_LR_W
MERGE_W = 3 * D_MODEL


def _params(sem):
    return pltpu.CompilerParams(dimension_semantics=sem, vmem_limit_bytes=VMEM_LIMIT)


def _largest_tile(n, cap):
    best = None
    for t in range(LANES, min(n, cap) + 1, LANES):
        if n % t == 0:
            best = t
    return best if best is not None else n


def _sigmoid(x):
    return 1.0 / (1.0 + jnp.exp(-x))


def _softplus(y):
    return jnp.maximum(y, 0.0) + jnp.log(1.0 + jnp.exp(-jnp.abs(y)))


def _masked_softmax(s, mask):
    s = jnp.where(mask, s, -jnp.inf)
    m = jnp.max(s, axis=-1, keepdims=True)
    m = jnp.where(m > -jnp.inf, m, 0.0)
    e = jnp.where(mask, jnp.exp(s - m), 0.0)
    return e / jnp.maximum(jnp.sum(e, axis=-1, keepdims=True), 1e-30)


def _masked_softmax_rows(s, mask):
    s = jnp.where(mask, s, -jnp.inf)
    m = jnp.max(s, axis=0, keepdims=True)
    m = jnp.where(m > -jnp.inf, m, 0.0)
    e = jnp.where(mask, jnp.exp(s - m), 0.0)
    return e / jnp.maximum(jnp.sum(e, axis=0, keepdims=True), 1e-30)


def _dot(a, b, precision=None):
    return jnp.dot(a, b, preferred_element_type=F32, precision=precision)


def _dot_nt(a, b, precision=None):
    return lax.dot_general(a, b, (((1,), (1,)), ((), ())), preferred_element_type=F32, precision=precision)


def _dot_tn(a, b, precision=None):
    return lax.dot_general(a, b, (((0,), (0,)), ((), ())), preferred_element_type=F32, precision=precision)


def _split2(x):
    hi = x.astype(BF16)
    return hi, (x - hi.astype(F32)).astype(BF16)


def _split3(x):
    hi = x.astype(BF16)
    rest = x - hi.astype(F32)
    mid = rest.astype(BF16)
    return hi, mid, (rest - mid.astype(F32)).astype(BF16)


def _dot3(dot, a, b):
    (a_hi, a_lo), (b_hi, b_lo) = a, b
    return dot(a_hi, b_hi) + dot(a_hi, b_lo) + dot(a_lo, b_hi)


def _rmsnorm_kernel(x_ref, g_ref, o_ref):
    x = x_ref[...]
    y = x * lax.rsqrt(jnp.mean(x * x, axis=-1, keepdims=True) + NORM_EPS) * g_ref[...]
    o_ref[...] = y.astype(o_ref.dtype)


def rmsnorm_rows(x, g, tm):
    m, d = x.shape
    return pl.pallas_call(
        _rmsnorm_kernel,
        grid=(m // tm,),
        in_specs=[pl.BlockSpec((tm, d), lambda i: (i, 0)), pl.BlockSpec((1, d), lambda i: (0, 0))],
        out_specs=pl.BlockSpec((tm, d), lambda i: (i, 0)),
        out_shape=jax.ShapeDtypeStruct((m, d), BF16),
        compiler_params=_params(("parallel",)),
        name="rmsnorm_rows",
    )(x, g.reshape(1, d))


def _mm_kernel(*refs, nk, mode):
    if mode == "residual":
        a_ref, w_ref, r_ref, o_ref = refs[:4]
        scratch = refs[4:]
    else:
        a_ref, w_ref, o_ref = refs[:3]
        scratch = refs[3:]
    part = _dot(a_ref[...], w_ref[...])

    def finish(acc):
        if mode == "relu2":
            acc = jnp.square(jnp.maximum(acc, 0.0))
        elif mode == "residual":
            acc = r_ref[...] + acc
        o_ref[...] = acc.astype(o_ref.dtype)

    if nk == 1:
        finish(part)
    else:
        acc_ref = scratch[0]
        k = pl.program_id(2)

        @pl.when(k == 0)
        def _():
            acc_ref[...] = part

        @pl.when(k > 0)
        def _():
            acc_ref[...] += part

        @pl.when(k == nk - 1)
        def _():
            finish(acc_ref[...])


def matmul(a, w, mode="plain", residual=None, tm=512):
    m, kdim = a.shape
    n = w.shape[1]
    tm = min(tm, m)
    tk = min(kdim, 2048)
    nk = kdim // tk
    tn = _largest_tile(n, 1024 if (mode == "residual" or nk > 1) else 2048)
    in_specs = [pl.BlockSpec((tm, tk), lambda i, j, k: (i, k)), pl.BlockSpec((tk, tn), lambda i, j, k: (k, j))]
    args = [a, w]
    if mode == "residual":
        in_specs.append(pl.BlockSpec((tm, tn), lambda i, j, k: (i, j)))
        args.append(residual)
    out_dtype = BF16 if mode == "relu2" else F32
    return pl.pallas_call(
        functools.partial(_mm_kernel, nk=nk, mode=mode),
        grid=(m // tm, n // tn, nk),
        in_specs=in_specs,
        out_specs=pl.BlockSpec((tm, tn), lambda i, j, k: (i, j)),
        out_shape=jax.ShapeDtypeStruct((m, n), out_dtype),
        scratch_shapes=[pltpu.VMEM((tm, tn), F32)] if nk > 1 else [],
        compiler_params=_params(("parallel", "parallel", "arbitrary")),
        name="matmul_" + mode,
    )(*args)


def _merge_kernel(on_ref, og_ref, or_ref, wn_ref, wg_ref, wr_ref, m0_ref, m1_ref, m2_ref, o_ref):
    acc = _sigmoid(m0_ref[...]) * _dot(on_ref[...].astype(BF16), wn_ref[...])
    acc += _sigmoid(m1_ref[...]) * _dot(og_ref[...].astype(BF16), wg_ref[...])
    acc += _sigmoid(m2_ref[...]) * _dot(or_ref[...].astype(BF16), wr_ref[...])
    o_ref[...] = acc.astype(o_ref.dtype)


def merge_branches(o_nsa, o_gla, o_rw, w_nsa, w_gla, w_rw, z_merge, tm=512, tn=1024):
    m = o_nsa.shape[0]
    tm = min(tm, m)
    nj = D_MODEL // tn
    row = lambda kd: pl.BlockSpec((tm, kd), lambda i, j: (i, 0))
    wsp = lambda kd: pl.BlockSpec((kd, tn), lambda i, j: (0, j))
    gate = lambda b: pl.BlockSpec((tm, tn), lambda i, j: (i, j + b * nj))
    return pl.pallas_call(
        _merge_kernel,
        grid=(m // tm, nj),
        in_specs=[row(NSA_Q_DIM), row(GLA_V_DIM), row(RW_DIM), wsp(NSA_Q_DIM), wsp(GLA_V_DIM), wsp(RW_DIM),
                  gate(0), gate(1), gate(2)],
        out_specs=pl.BlockSpec((tm, tn), lambda i, j: (i, j)),
        out_shape=jax.ShapeDtypeStruct((m, D_MODEL), BF16),
        compiler_params=_params(("parallel", "parallel")),
        name="merge_branches",
    )(o_nsa, o_gla, o_rw, w_nsa, w_gla, w_rw, z_merge, z_merge, z_merge)


def _head_rmsnorm(x, g):
    return x * lax.rsqrt(jnp.mean(x * x, axis=-1, keepdims=True) + NORM_EPS) * g


def _nsa_prep_kernel(z_ref, g_ref, qn_ref, cmp_ref, sel_ref, win_ref, selb_ref, winb_ref, gates_ref):
    hd = NSA_HD
    g_q, g_sel, g_win = g_ref[0:1, :], g_ref[2:3, :], g_ref[3:4, :]
    scale = NSA_HD ** -0.5
    for h in range(NSA_HEADS):
        q = _head_rmsnorm(z_ref[:, h * hd:(h + 1) * hd], g_q)
        qn_ref[:, h * hd:(h + 1) * hd] = (q * scale).astype(BF16)
    c0 = NSA_Q_DIM
    cmp_ref[...] = z_ref[:, c0:c0 + 2 * NSA_KV_DIM]
    for (off, gain, o_ref, ob_ref) in ((c0 + 2 * NSA_KV_DIM, g_sel, sel_ref, selb_ref),
                                       (c0 + 4 * NSA_KV_DIM, g_win, win_ref, winb_ref)):
        for h in range(NSA_KV_HEADS):
            k = _head_rmsnorm(z_ref[:, off + h * hd:off + (h + 1) * hd], gain)
            o_ref[:, h * hd:(h + 1) * hd] = k
            ob_ref[:, h * hd:(h + 1) * hd] = k.astype(BF16)
        v = z_ref[:, off + NSA_KV_DIM:off + 2 * NSA_KV_DIM]
        o_ref[:, NSA_KV_DIM:] = v
        ob_ref[:, NSA_KV_DIM:] = v.astype(BF16)
    gates_ref[...] = _sigmoid(z_ref[:, c0 + 6 * NSA_KV_DIM:])


def nsa_prep(z_nsa, qk_g, tm=256):
    m = z_nsa.shape[0]
    tm = min(tm, m)
    kvw = 2 * NSA_KV_DIM
    row = lambda w: pl.BlockSpec((tm, w), lambda i: (i, 0))
    return pl.pallas_call(
        _nsa_prep_kernel,
        grid=(m // tm,),
        in_specs=[row(NSA_W), pl.BlockSpec((4, NSA_HD), lambda i: (0, 0))],
        out_specs=[row(NSA_Q_DIM), row(kvw), row(kvw), row(kvw), row(kvw), row(kvw), row(GATE_W)],
        out_shape=[jax.ShapeDtypeStruct((m, NSA_Q_DIM), BF16),
                   jax.ShapeDtypeStruct((m, kvw), F32), jax.ShapeDtypeStruct((m, kvw), F32),
                   jax.ShapeDtypeStruct((m, kvw), F32), jax.ShapeDtypeStruct((m, kvw), BF16),
                   jax.ShapeDtypeStruct((m, kvw), BF16), jax.ShapeDtypeStruct((m, GATE_W), F32)],
        compiler_params=_params(("parallel",)),
        name="nsa_prep",
    )(z_nsa, qk_g)


QUARTERS = 2 * NSA_KV_DIM // LANES


def _compress_compute(x_ref, nb, pos_ref, w1_ref, w2_ref, kg_ref, kc_ref, vc_ref, acc_ref):
    acc_ref[...] = jnp.zeros_like(acc_ref)
    row_stride = NSA_BLOCK * QUARTERS

    def body(l, carry):
        for q in range(QUARTERS):
            kv = q // 2
            x = x_ref[pl.ds(l * QUARTERS + q, nb, stride=row_stride), :] + pos_ref[kv, pl.ds(l, 1), :]
            for half in range(2):
                acc_ref[q * 2 + half] += _dot(x[:, half * NSA_HD:(half + 1) * NSA_HD].astype(BF16), w1_ref[kv, l])
        return carry

    lax.fori_loop(0, NSA_BLOCK, body, 0)
    for q in range(QUARTERS):
        kv = q // 2
        for half in range(2):
            g = (q % 2) * 2 + half
            hid = jax.nn.gelu(acc_ref[q * 2 + half], approximate=True)
            s = _dot(hid.astype(BF16), w2_ref[kv])
            if kv == 0:
                kc_ref[:, g * NSA_HD:(g + 1) * NSA_HD] = _head_rmsnorm(s, kg_ref[...])
            else:
                vc_ref[:, g * NSA_HD:(g + 1) * NSA_HD] = s


def _compress_kernel(x_ref, pos_ref, w1_ref, w2_ref, kg_ref, kc_ref, vc_ref, acc_ref, *, nb):
    _compress_compute(x_ref, nb, pos_ref, w1_ref, w2_ref, kg_ref, kc_ref, vc_ref, acc_ref)


def _cmp_weight_specs(nidx):
    zero = lambda n: (lambda *idx: (0,) * n)
    return [pl.BlockSpec((2, NSA_BLOCK, LANES), zero(3)),
            pl.BlockSpec((2, NSA_BLOCK, NSA_HD, NSA_CMP_HID), zero(4)),
            pl.BlockSpec((2, NSA_CMP_HID, NSA_HD), zero(3)),
            pl.BlockSpec((1, NSA_HD), zero(2))]


def compress_rows(rows, cw, nb):
    nblk = rows.shape[0] // NSA_BLOCK
    nb = min(nb, nblk)
    x4 = rows.reshape(nblk * NSA_BLOCK * QUARTERS, LANES)
    out = jax.ShapeDtypeStruct((nblk, NSA_KV_DIM), F32)
    return pl.pallas_call(
        functools.partial(_compress_kernel, nb=nb),
        grid=(nblk // nb,),
        in_specs=[pl.BlockSpec((nb * NSA_BLOCK * QUARTERS, LANES), lambda i: (i, 0))] + _cmp_weight_specs(1),
        out_specs=[pl.BlockSpec((nb, NSA_KV_DIM), lambda i: (i, 0))] * 2,
        out_shape=[out, out],
        scratch_shapes=[pltpu.VMEM((2 * QUARTERS, nb, NSA_CMP_HID), F32)],
        compiler_params=_params(("parallel",)),
        name="compress_rows",
    )(x4, cw["pos2"], cw["w1"], cw["w2"], cw["kg"])


PAGES_PER_STEP = 32
BLOCKS_PER_PAGE = PAGE_SIZE // NSA_BLOCK


def _compress_pages_kernel(pt_ref, *refs, npg):
    page_refs = refs[:npg]
    pos_ref, w1_ref, w2_ref, kg_ref, kc_ref, vc_ref, x_ref, acc_ref = refs[npg:]
    rows = PAGE_SIZE * QUARTERS
    for p in range(npg):
        x_ref[p * rows:(p + 1) * rows, :] = page_refs[p][0, 0]
    _compress_compute(x_ref, npg * BLOCKS_PER_PAGE, pos_ref, w1_ref, w2_ref, kg_ref, kc_ref.at[0], vc_ref.at[0], acc_ref)


def compress_pages(cache, layer, page_table, cw):
    nbatch, n_pages = page_table.shape
    npg = min(PAGES_PER_STEP, n_pages)
    nb = npg * BLOCKS_PER_PAGE
    cache4 = cache.reshape(cache.shape[0], cache.shape[1], PAGE_SIZE * QUARTERS, LANES)
    page_spec = lambda p: pl.BlockSpec((1, 1, PAGE_SIZE * QUARTERS, LANES),
                                       lambda b, c, pt: (layer, pt[b, c * npg + p], 0, 0))
    wspecs = _cmp_weight_specs(3)
    out = jax.ShapeDtypeStruct((nbatch, n_pages * BLOCKS_PER_PAGE, NSA_KV_DIM), F32)
    grid_spec = pltpu.PrefetchScalarGridSpec(
        num_scalar_prefetch=1,
        grid=(nbatch, n_pages // npg),
        in_specs=[page_spec(p) for p in range(npg)] + wspecs,
        out_specs=[pl.BlockSpec((1, nb, NSA_KV_DIM), lambda b, c, pt: (b, c, 0))] * 2,
        scratch_shapes=[pltpu.VMEM((npg * PAGE_SIZE * QUARTERS, LANES), F32),
                        pltpu.VMEM((2 * QUARTERS, nb, NSA_CMP_HID), F32)],
    )
    return pl.pallas_call(
        functools.partial(_compress_pages_kernel, npg=npg),
        grid_spec=grid_spec,
        out_shape=[out, out],
        compiler_params=_params(("parallel", "arbitrary")),
        name="compress_pages",
    )(page_table, *([cache4] * npg), cw["pos2"], cw["w1"], cw["w2"], cw["kg"])


def _select_blocks(score, blk, cur, nblk):
    forced = (blk == 0) | (blk == cur) | (blk == cur - 1)
    score = jnp.where(forced, jnp.inf, score)
    score = jnp.where(blk <= cur, score, -jnp.inf)
    rank = jnp.zeros(score.shape, F32)
    for i in range(nblk):
        row = score[i:i + 1, :]
        ahead = (row > score) | ((row == score) & (blk > i))
        rank = rank + jnp.where(ahead, 1.0, 0.0)
    return (rank < float(min(NSA_TOPK, nblk))) & (blk <= cur)


def _alibi_slope(head):
    return 2.0 ** (-8.0 * (head + 1) / NSA_HEADS)


FAR = 1e38
M_INIT = -1e30


def _nsa_prompt_kernel(q_ref, gates_ref, selb_ref, winb_ref, kc_ref, vc_ref, o_ref, *, tq, t_len, band, kchunk):
    qi = pl.program_id(1)
    q0 = qi * tq
    nblk = t_len // NSA_BLOCK
    hd = NSA_HD
    q_pos = q0 + lax.broadcasted_iota(jnp.int32, (tq, 1), 0)
    q_pos_row = q0 + lax.broadcasted_iota(jnp.int32, (1, tq), 1)
    cur = q_pos_row // NSA_BLOCK
    blk = lax.broadcasted_iota(jnp.int32, (nblk, 1), 0)
    d_cmp = q_pos_row - ((blk + 1) * NSA_BLOCK - 1)
    m_cmp = d_cmp >= 0
    d_cmp_f = d_cmp.astype(F32)
    w0 = pl.multiple_of(jnp.maximum(q0 - NSA_WINDOW, 0), tq)
    d_win = q_pos - (w0 + lax.broadcasted_iota(jnp.int32, (1, band), 1))
    d_win_m = jnp.where((d_win >= 0) & (d_win < NSA_WINDOW), d_win.astype(F32), FAR)
    blk_row = lax.broadcasted_iota(jnp.int32, (nblk, kchunk), 0)
    key_col = lax.broadcasted_iota(jnp.int32, (nblk, kchunk), 1)
    key_lane = lax.broadcasted_iota(jnp.int32, (1, kchunk), 1)
    n_chunks = (q0 + tq + kchunk - 1) // kchunk
    for g in range(NSA_KV_HEADS):
        kc = kc_ref[0, :, g * hd:(g + 1) * hd].astype(BF16)
        vc = vc_ref[0, :, g * hd:(g + 1) * hd].astype(BF16)
        k_win = winb_ref[0, pl.ds(w0, band), g * hd:(g + 1) * hd]
        v_win = winb_ref[0, pl.ds(w0, band), NSA_KV_DIM + g * hd:NSA_KV_DIM + (g + 1) * hd]
        heads = [g * NSA_GROUP + r for r in range(NSA_GROUP)]
        group = range(NSA_GROUP)
        qs = [q_ref[0, :, h * hd:(h + 1) * hd] for h in heads]
        slopes = [_alibi_slope(h) for h in heads]
        p_cmp = [_masked_softmax_rows(_dot_nt(kc, qs[r]) - slopes[r] * d_cmp_f, m_cmp) for r in group]
        o_cmp = [_dot_tn(p_cmp[r].astype(BF16), vc) for r in group]
        sel_f = jnp.where(_select_blocks(sum(p_cmp), blk, cur, nblk), 1.0, 0.0).astype(BF16)

        def sel_chunk(ci, carry):
            k0 = pl.multiple_of(ci * kchunk, kchunk)
            k_c = selb_ref[0, pl.ds(k0, kchunk), g * hd:(g + 1) * hd]
            v_c = selb_ref[0, pl.ds(k0, kchunk), NSA_KV_DIM + g * hd:NSA_KV_DIM + (g + 1) * hd]
            expand = ((k0 + key_col) // NSA_BLOCK == blk_row).astype(BF16)
            d = q_pos - (k0 + key_lane)
            d_m = jnp.where((_dot_tn(sel_f, expand) > 0.5) & (d >= 0), d.astype(F32), FAR)
            s = [_dot_nt(qs[r], k_c) - slopes[r] * d_m for r in group]
            m_new = [jnp.maximum(carry[r][0], jnp.max(s[r], axis=-1, keepdims=True)) for r in group]
            p = [jnp.exp(s[r] - m_new[r]) for r in group]
            alpha = [jnp.exp(carry[r][0] - m_new[r]) for r in group]
            pv = [_dot(p[r].astype(BF16), v_c) for r in group]
            return tuple((m_new[r], alpha[r] * carry[r][1] + jnp.sum(p[r], axis=-1, keepdims=True),
                          alpha[r] * carry[r][2] + pv[r]) for r in group)

        init = tuple((jnp.full((tq, 1), M_INIT, F32), jnp.zeros((tq, 1), F32), jnp.zeros((tq, hd), F32))
                     for _ in group)
        sel_state = lax.fori_loop(0, n_chunks, sel_chunk, init)
        s = [_dot_nt(qs[r], k_win) - slopes[r] * d_win_m for r in group]
        p = [jnp.exp(s[r] - jnp.maximum(jnp.max(s[r], axis=-1, keepdims=True), M_INIT)) for r in group]
        pv = [_dot(p[r].astype(BF16), v_win) for r in group]
        for r, h in enumerate(heads):
            o_sel = sel_state[r][2] / jnp.maximum(sel_state[r][1], 1e-30)
            o_win = pv[r] / jnp.maximum(jnp.sum(p[r], axis=-1, keepdims=True), 1e-30)
            gate = lambda b: gates_ref[0, :, b * NSA_HEADS + h:b * NSA_HEADS + h + 1]
            o = gate(0) * o_cmp[r] + gate(1) * o_sel + gate(2) * o_win
            o_ref[0, :, h * hd:(h + 1) * hd] = o.astype(o_ref.dtype)


def nsa_prompt_attention(qn, gates, sel_b, win_b, k_cmp, v_cmp, tq=128):
    b, t, _ = qn.shape
    tq = min(tq, t)
    band = min(NSA_WINDOW + tq, t)
    nblk = t // NSA_BLOCK
    full = lambda w, n: pl.BlockSpec((1, n, w), lambda i, j: (i, 0, 0))
    tile = lambda w: pl.BlockSpec((1, tq, w), lambda i, j: (i, j, 0))
    return pl.pallas_call(
        functools.partial(_nsa_prompt_kernel, tq=tq, t_len=t, band=band, kchunk=min(256, t)),
        grid=(b, t // tq),
        in_specs=[tile(NSA_Q_DIM), tile(GATE_W), full(2 * NSA_KV_DIM, t), full(2 * NSA_KV_DIM, t),
                  full(NSA_KV_DIM, nblk), full(NSA_KV_DIM, nblk)],
        out_specs=tile(NSA_Q_DIM),
        out_shape=jax.ShapeDtypeStruct((b, t, NSA_Q_DIM), BF16),
        compiler_params=_params(("parallel", "arbitrary")),
        name="nsa_prompt_attention",
    )(qn, gates, sel_b, win_b, k_cmp, v_cmp)


def _head_group_mask(nrows):
    row = lax.broadcasted_iota(jnp.int32, (nrows, 1), 0)
    return [(row // NSA_GROUP) == g for g in range(NSA_KV_HEADS)]


def _slope_column(nrows, head0=0):
    row = lax.broadcasted_iota(jnp.int32, (nrows, 1), 0)
    col = jnp.zeros((nrows, 1), F32)
    for h in range(NSA_HEADS):
        col = jnp.where(row + head0 == h, 2.0 ** (-8.0 * (h + 1) / NSA_HEADS), col)
    return col


def _nsa_sample1_kernel(q_ref, kc_ref, vc_ref, wcache_ref, wnew_ref, ocmp_ref, owin_ref, idx_ref, *,
                        nblk, nblk_pad, past, w_buf):
    hd = NSA_HD
    q = q_ref[0]
    groups = _head_group_mask(NSA_HEADS)
    slope = _slope_column(NSA_HEADS)
    q_pos = past
    cur = q_pos // NSA_BLOCK
    blk = lax.broadcasted_iota(jnp.int32, (1, nblk_pad), 1)
    d_cmp = q_pos - ((blk + 1) * NSA_BLOCK - 1)
    s = jnp.zeros((NSA_HEADS, nblk_pad), F32)
    for g in range(NSA_KV_HEADS):
        s = s + jnp.where(groups[g], _dot_nt(q, kc_ref[0, :, g * hd:(g + 1) * hd]), 0.0)
    p = _masked_softmax(s - slope * d_cmp.astype(F32), (d_cmp >= 0) & (blk < nblk))
    o = jnp.zeros((NSA_HEADS, hd), F32)
    for g in range(NSA_KV_HEADS):
        o = o + jnp.where(groups[g], _dot(p, vc_ref[0, :, g * hd:(g + 1) * hd]), 0.0)
    ocmp_ref[0] = o
    gsel = (lax.broadcasted_iota(jnp.int32, (8, NSA_HEADS), 1) // NSA_GROUP
            == lax.broadcasted_iota(jnp.int32, (8, NSA_HEADS), 0)).astype(F32)
    score = _dot(gsel, p, HIGHEST)
    forced = (blk == 0) | (blk == cur) | (blk == cur - 1)
    score = jnp.where(forced, jnp.inf, score)
    score = jnp.where((blk <= cur) & (blk < nblk), score, -jnp.inf)
    blk_f = blk.astype(F32)
    lane_k = lax.broadcasted_iota(jnp.int32, (1, NSA_TOPK), 1)
    idx = jnp.zeros((8, NSA_TOPK), F32)
    for k in range(NSA_TOPK):
        m = jnp.max(score, axis=-1, keepdims=True)
        first = jnp.min(jnp.where(score == m, blk_f, 1e9), axis=-1, keepdims=True)
        idx = jnp.where(lane_k == k, first, idx)
        score = jnp.where(blk_f == first, -jnp.inf, score)
    idx_ref[0] = idx.astype(jnp.int32)
    i_buf = lax.broadcasted_iota(jnp.int32, (1, w_buf), 1)
    win_pos = past - w_buf + i_buf
    d_win = q_pos - win_pos
    m_win = (d_win >= 0) & (d_win < NSA_WINDOW) & (win_pos >= 0)
    s = jnp.zeros((NSA_HEADS, w_buf), F32)
    k_new = jnp.zeros((NSA_HEADS, hd), F32)
    v_new = jnp.zeros((NSA_HEADS, hd), F32)
    for g in range(NSA_KV_HEADS):
        s = s + jnp.where(groups[g], _dot_nt(q, wcache_ref[0, 0, :, g * hd:(g + 1) * hd]), 0.0)
        k_new = k_new + jnp.where(groups[g], wnew_ref[0, :, g * hd:(g + 1) * hd], 0.0)
        v_new = v_new + jnp.where(groups[g], wnew_ref[0, :, NSA_KV_DIM + g * hd:NSA_KV_DIM + (g + 1) * hd], 0.0)
    s = jnp.where(m_win, s - slope * d_win.astype(F32), -jnp.inf)
    s_new = jnp.sum(q * k_new, axis=-1, keepdims=True)
    m = jnp.maximum(jnp.max(s, axis=-1, keepdims=True), s_new)
    e = jnp.where(m_win, jnp.exp(s - m), 0.0)
    e_new = jnp.exp(s_new - m)
    denom = jnp.maximum(jnp.sum(e, axis=-1, keepdims=True) + e_new, 1e-30)
    o = e_new * v_new
    for g in range(NSA_KV_HEADS):
        o = o + jnp.where(groups[g], _dot(e, wcache_ref[0, 0, :, NSA_KV_DIM + g * hd:NSA_KV_DIM + (g + 1) * hd]), 0.0)
    owin_ref[0] = o / denom


def nsa_sample_attention1(q, k_cmp, v_cmp, cache_win, layer, win_new, nblk, past):
    nbatch = q.shape[0]
    nblk_pad = k_cmp.shape[1]
    w_buf = cache_win.shape[2]
    hq = pl.BlockSpec((1, NSA_HEADS, NSA_HD), lambda b: (b, 0, 0))
    return pl.pallas_call(
        functools.partial(_nsa_sample1_kernel, nblk=nblk, nblk_pad=nblk_pad, past=past, w_buf=w_buf),
        grid=(nbatch,),
        in_specs=[hq, pl.BlockSpec((1, nblk_pad, NSA_KV_DIM), lambda b: (b, 0, 0)),
                  pl.BlockSpec((1, nblk_pad, NSA_KV_DIM), lambda b: (b, 0, 0)),
                  pl.BlockSpec((1, 1, w_buf, 2 * NSA_KV_DIM), lambda b: (layer, b, 0, 0)),
                  pl.BlockSpec((1, 1, 2 * NSA_KV_DIM), lambda b: (b, 0, 0))],
        out_specs=[hq, hq, pl.BlockSpec((1, 8, NSA_TOPK), lambda b: (b, 0, 0))],
        out_shape=[jax.ShapeDtypeStruct((nbatch, NSA_HEADS, NSA_HD), F32),
                   jax.ShapeDtypeStruct((nbatch, NSA_HEADS, NSA_HD), F32),
                   jax.ShapeDtypeStruct((nbatch, 8, NSA_TOPK), jnp.int32)],
        compiler_params=_params(("parallel",)),
        name="nsa_sample_attention1",
    )(q, k_cmp, v_cmp, cache_win, win_new)


def _nsa_sample2_kernel(idx_ref, pt_ref, q_ref, b0_ref, b1_ref, b2_ref, b3_ref, new_ref, ocmp_ref, owin_ref, gates_ref,
                        o_ref, m_scr, l_scr, acc_scr, *, past, n_cache_blk):
    hd = NSA_HD
    b = pl.program_id(0)
    k = pl.program_id(1)
    blocks = (b0_ref, b1_ref, b2_ref, b3_ref)

    @pl.when(k == 0)
    def _():
        m_scr[...] = jnp.full(m_scr.shape, -jnp.inf, F32)
        l_scr[...] = jnp.zeros(l_scr.shape, F32)
        acc_scr[...] = jnp.zeros(acc_scr.shape, F32)

    q_pos = past
    lane = lax.broadcasted_iota(jnp.int32, (1, PAGE_SIZE), 1)
    for g in range(NSA_KV_HEADS):
        j = idx_ref[(b * 8 + g) * NSA_TOPK + k]
        slope = _slope_column(8, head0=g * NSA_GROUP)
        q = q_ref[0, g]
        k_t = blocks[g][0, 0, 0, 0]
        v_t = blocks[g][0, 0, 1, 0]
        d = q_pos - ((j // BLOCKS_PER_PAGE) * PAGE_SIZE + lane)
        in_block = (lane // NSA_BLOCK) == (j % BLOCKS_PER_PAGE)
        valid = (d >= 0) & in_block & (j < n_cache_blk)
        s = jnp.where(valid, _dot(q, k_t) - slope * d.astype(F32), -jnp.inf)
        k_new = new_ref[0, :, g * hd:(g + 1) * hd]
        v_new = new_ref[0, :, NSA_KV_DIM + g * hd:NSA_KV_DIM + (g + 1) * hd]
        new_ok = j == q_pos // NSA_BLOCK
        s_new = jnp.where(new_ok, jnp.sum(q * k_new, axis=-1, keepdims=True), -jnp.inf)
        m_old = m_scr[g]
        m_new = jnp.maximum(jnp.maximum(m_old, jnp.max(s, axis=-1, keepdims=True)), s_new)
        m_safe = jnp.where(m_new > -jnp.inf, m_new, 0.0)
        alpha = jnp.exp(m_old - m_safe)
        p = jnp.where(valid, jnp.exp(s - m_safe), 0.0)
        p_new = jnp.where(new_ok, jnp.exp(s_new - m_safe), 0.0)
        l_scr[g] = alpha * l_scr[g] + jnp.sum(p, axis=-1, keepdims=True) + p_new
        acc_scr[g] = alpha * acc_scr[g] + _dot_nt(p, v_t) + p_new * v_new
        m_scr[g] = m_new

    @pl.when(k == NSA_TOPK - 1)
    def _():
        for g in range(NSA_KV_HEADS):
            o_sel = acc_scr[g] / jnp.maximum(l_scr[g], 1e-30)
            gt = gates_ref[0, g]
            o_ref[0, g] = gt[:, 0:1] * ocmp_ref[0, g] + gt[:, 1:2] * o_sel + gt[:, 2:3] * owin_ref[0, g]


def nsa_sample_attention2(idx, page_table, q8, cache_sel, layer, sel_new, o_cmp8, o_win8, gates8, past):
    nbatch = q8.shape[0]
    n_cache_blk = past // NSA_BLOCK
    cache_t = jnp.transpose(cache_sel, (0, 1, 3, 4, 5, 2))

    def blk_map(g):
        def index_map(b, k, idx_ref, pt_ref):
            j = jnp.minimum(idx_ref[(b * 8 + g) * NSA_TOPK + k], n_cache_blk - 1)
            return (layer, pt_ref[b, j // BLOCKS_PER_PAGE], 0, g, 0, 0)
        return index_map

    g4 = lambda w: pl.BlockSpec((1, NSA_KV_HEADS, 8, w), lambda b, k, i, p: (b, 0, 0, 0))
    grid_spec = pltpu.PrefetchScalarGridSpec(
        num_scalar_prefetch=2,
        grid=(nbatch, NSA_TOPK),
        in_specs=[g4(NSA_HD)] + [pl.BlockSpec((1, 1, 2, 1, NSA_HD, PAGE_SIZE), blk_map(g)) for g in range(NSA_KV_HEADS)]
                 + [pl.BlockSpec((1, 1, 2 * NSA_KV_DIM), lambda b, k, i, p: (b, 0, 0)), g4(NSA_HD), g4(NSA_HD), g4(LANES)],
        out_specs=g4(NSA_HD),
        scratch_shapes=[pltpu.VMEM((NSA_KV_HEADS, 8, 1), F32), pltpu.VMEM((NSA_KV_HEADS, 8, 1), F32),
                        pltpu.VMEM((NSA_KV_HEADS, 8, NSA_HD), F32)],
    )
    return pl.pallas_call(
        functools.partial(_nsa_sample2_kernel, past=past, n_cache_blk=n_cache_blk),
        grid_spec=grid_spec,
        out_shape=jax.ShapeDtypeStruct((nbatch, NSA_KV_HEADS, 8, NSA_HD), F32),
        compiler_params=_params(("parallel", "arbitrary")),
        name="nsa_sample_attention2",
    )(idx, page_table, q8, cache_t, cache_t, cache_t, cache_t, sel_new, o_cmp8, o_win8, gates8)


GLA_Q0, GLA_K0, GLA_V0 = 0, GLA_QK_DIM, 2 * GLA_QK_DIM
GLA_A0 = GLA_V0 + GLA_V_DIM
GLA_G0 = GLA_A0 + GLA_A_W


def _gla_kernel(z_ref, s0_ref, a2_ref, ab_ref, ng_ref, o_ref, st_ref, s_scr, *, tt, t_valid, n_t):
    ti = pl.program_id(1)
    c = GLA_CHUNK

    @pl.when(ti == 0)
    def _():
        for h in range(GLA_HEADS):
            s_scr[h] = s0_ref[0, h].T

    a_lr = z_ref[0, :, GLA_A0:GLA_A0 + GLA_A_W].astype(BF16)
    pre = _dot(a_lr, a2_ref[...]) + ab_ref[...]
    log_a = (jnp.minimum(pre, 0.0) - jnp.log(1.0 + jnp.exp(-jnp.abs(pre)))) / GLA_TAU
    row = ti * tt + lax.broadcasted_iota(jnp.int32, (tt, 1), 0)
    log_a = jnp.where(row < t_valid, log_a, 0.0)
    tri = lax.broadcasted_iota(jnp.int32, (c, c), 0) >= lax.broadcasted_iota(jnp.int32, (c, c), 1)
    tri_f = tri.astype(F32)
    hr = range(GLA_HEADS)
    ksl = [slice(h * GLA_DK, (h + 1) * GLA_DK) for h in hr]
    states = [s_scr[h] for h in hr]
    for ci in range(tt // c):
        rows = slice(ci * c, (ci + 1) * c)
        cum = [_dot(tri_f, log_a[rows, ksl[h]], HIGHEST) for h in hr]
        last = [cum[h][c - 1:c, :] for h in hr]
        q_c = [z_ref[0, rows, GLA_Q0 + h * GLA_DK:GLA_Q0 + (h + 1) * GLA_DK] * (GLA_DK ** -0.5) for h in hr]
        k_c = [z_ref[0, rows, GLA_K0 + h * GLA_DK:GLA_K0 + (h + 1) * GLA_DK] for h in hr]
        v_c = [z_ref[0, rows, GLA_V0 + h * GLA_DV:GLA_V0 + (h + 1) * GLA_DV].astype(BF16) for h in hr]
        q_dec = [(q_c[h] * jnp.exp(cum[h])).astype(BF16) for h in hr]
        k_inv = [(k_c[h] * jnp.exp(-cum[h])).astype(BF16) for h in hr]
        k_dec = [(k_c[h] * jnp.exp(last[h] - cum[h])).astype(BF16) for h in hr]
        att = [jnp.where(tri, _dot_nt(q_dec[h], k_inv[h]), 0.0).astype(BF16) for h in hr]
        o = [_dot(att[h], v_c[h]) + _dot_nt(q_dec[h], states[h].astype(BF16)) for h in hr]
        states = [states[h] * jnp.exp(last[h]) + _dot_tn(v_c[h], k_dec[h]) for h in hr]
        for h in hr:
            on = o[h] * lax.rsqrt(jnp.mean(o[h] * o[h], axis=-1, keepdims=True) + NORM_EPS) * ng_ref[...]
            gate = z_ref[0, rows, GLA_G0 + h * GLA_DV:GLA_G0 + (h + 1) * GLA_DV]
            o_ref[0, rows, h * GLA_DV:(h + 1) * GLA_DV] = (on * (gate * _sigmoid(gate))).astype(o_ref.dtype)
    for h in hr:
        s_scr[h] = states[h]

    @pl.when(ti == n_t - 1)
    def _():
        for h in range(GLA_HEADS):
            st_ref[0, h] = s_scr[h].T


def gla_mixer(z_gla, s0, a2p, ab, ng, t_valid):
    b, t, _ = z_gla.shape
    tt = min(t, 512)
    n_t = t // tt
    return pl.pallas_call(
        functools.partial(_gla_kernel, tt=tt, t_valid=t_valid, n_t=n_t),
        grid=(b, n_t),
        in_specs=[pl.BlockSpec((1, tt, GLA_W), lambda i, j: (i, j, 0)),
                  pl.BlockSpec((1, GLA_HEADS, GLA_DK, GLA_DV), lambda i, j: (i, 0, 0, 0)),
                  pl.BlockSpec((GLA_A_W, GLA_QK_DIM), lambda i, j: (0, 0)),
                  pl.BlockSpec((1, GLA_QK_DIM), lambda i, j: (0, 0)),
                  pl.BlockSpec((1, GLA_DV), lambda i, j: (0, 0))],
        out_specs=[pl.BlockSpec((1, tt, GLA_V_DIM), lambda i, j: (i, j, 0)),
                   pl.BlockSpec((1, GLA_HEADS, GLA_DK, GLA_DV), lambda i, j: (i, 0, 0, 0))],
        out_shape=[jax.ShapeDtypeStruct((b, t, GLA_V_DIM), BF16),
                   jax.ShapeDtypeStruct((b, GLA_HEADS, GLA_DK, GLA_DV), F32)],
        scratch_shapes=[pltpu.VMEM((GLA_HEADS, GLA_DV, GLA_DK), F32)],
        compiler_params=_params(("parallel", "arbitrary")),
        name="gla_mixer",
    )(z_gla, s0, a2p, ab, ng)


RW_R0, RW_K0, RW_V0 = 0, RW_DIM, 2 * RW_DIM
RW_WL0 = 3 * RW_DIM
RW_AL0 = RW_WL0 + RW_LR_W
RW_GL0 = RW_AL0 + RW_LR_W


def _rw_prep_kernel(z_ref, sh_ref, mu_ref, w0_ref, w2_ref, a0_ref, a2_ref, g2_ref, kk_ref, ka_ref, rk_ref, seg_ref,
                    r_ref, dec_ref, k_ref, v_ref, nkk_ref, kka_ref, g_ref, bonus_ref):
    z = z_ref[...]
    xm = z + (sh_ref[...] - z) * mu_ref[...]
    r = xm[:, RW_R0:RW_R0 + RW_DIM]
    k = xm[:, RW_K0:RW_K0 + RW_DIM]
    v = xm[:, RW_V0:RW_V0 + RW_DIM]
    wl = xm[:, RW_WL0:RW_WL0 + RW_LR_W]
    al = xm[:, RW_AL0:RW_AL0 + RW_LR_W]
    gl = xm[:, RW_GL0:RW_GL0 + RW_LR_W]
    w = -_softplus(-(w0_ref[...] + _dot(jnp.tanh(wl).astype(BF16), w2_ref[...]))) - 0.5
    a = _sigmoid(a0_ref[...] + _dot(al.astype(BF16), a2_ref[...]))
    kk = k * kk_ref[...]
    norm = jnp.sqrt(_dot(kk * kk, seg_ref[...], HIGHEST))
    kk = kk / jnp.maximum(norm, 1e-12)
    k2 = k * (1.0 + (a - 1.0) * ka_ref[...])
    r_ref[...] = r
    dec_ref[...] = -jnp.exp(w)
    k_ref[...] = k2
    v_ref[...] = v
    nkk_ref[...] = -kk
    kka_ref[...] = kk * a
    g_ref[...] = _dot(_sigmoid(gl).astype(BF16), g2_ref[...])
    bonus_ref[...] = _dot(r * k2 * rk_ref[...], seg_ref[...], HIGHEST) * v


def rw_prep(z_rw, shifted, rp, tm=512):
    m = z_rw.shape[0]
    tm = min(tm, m)
    row = lambda w: pl.BlockSpec((tm, w), lambda i: (i, 0))
    const = lambda r, w: pl.BlockSpec((r, w), lambda i: (0, 0))
    out = jax.ShapeDtypeStruct((m, RW_DIM), F32)
    return pl.pallas_call(
        _rw_prep_kernel,
        grid=(m // tm,),
        in_specs=[row(RW_W), row(RW_W), const(1, RW_W), const(1, RW_DIM), const(RW_LR_W, RW_DIM), const(1, RW_DIM),
                  const(RW_LR_W, RW_DIM), const(RW_LR_W, RW_DIM), const(1, RW_DIM), const(1, RW_DIM), const(1, RW_DIM),
                  const(RW_DIM, RW_DIM)],
        out_specs=[row(RW_DIM)] * 8,
        out_shape=[out] * 8,
        compiler_params=_params(("parallel",)),
        name="rw_prep",
    )(z_rw, shifted, rp["mu"], rp["w0"], rp["w2"], rp["a0"], rp["a2"], rp["g2"], rp["kk"], rp["ka"], rp["rk"], rp["seg"])


RW_CHUNK = 64
RW_CHUNK_LOG2 = 6


def _rw_scan_kernel(r_ref, dec_ref, k_ref, v_ref, nkk_ref, kka_ref, g_ref, bonus_ref, lng_ref, lnb_ref, s0_ref,
                    o_ref, st_ref, s_scr, *, tt, n_t):
    ti = pl.program_id(1)

    @pl.when(ti == 0)
    def _():
        s_scr[...] = s0_ref[0]

    c = RW_CHUNK
    row_i = lax.broadcasted_iota(jnp.int32, (c, c), 0)
    col_i = lax.broadcasted_iota(jnp.int32, (c, c), 1)
    lower = row_i >= col_i
    strict = row_i > col_i
    lower_b = lower.astype(BF16)
    eye_f = (row_i == col_i).astype(F32)
    heads = [slice(h * RW_HD, (h + 1) * RW_HD) for h in range(RW_HEADS)]
    sp = _split2
    mm = functools.partial(_dot3, _dot)
    mm_nt = functools.partial(_dot3, _dot_nt)
    mm_tn = functools.partial(_dot3, _dot_tn)

    def chunk(i, carry):
        rows = pl.ds(pl.multiple_of(i * c, c), c)
        st = []
        for h, hs in enumerate(heads):
            lw = dec_ref[0, rows, hs]
            cum = sum(_dot(lower_b, part) for part in _split3(lw))
            g_inv = jnp.exp(-cum)
            d = {"g_last": jnp.exp(cum[c - 1:c, :]), "s0_f": s_scr[h],
                 "at_f": kka_ref[0, rows, hs] * g_inv, "kt_f": k_ref[0, rows, hs] * g_inv,
                 "bt": sp(nkk_ref[0, rows, hs] * jnp.exp(cum - lw)), "rt": sp(r_ref[0, rows, hs] * jnp.exp(cum)),
                 "v": sp(v_ref[0, rows, hs]),
                 "tail": (lng_ref[:, hs], lnb_ref[:, hs], bonus_ref[0, rows, hs], g_ref[0, rows, hs])}
            d["at"], d["kt"], d["s0"] = sp(d["at_f"]), sp(d["kt_f"]), sp(d["s0_f"])
            st.append(d)
        for d in st:
            d["ma"] = jnp.where(strict, mm_nt(d["bt"], d["at"]), 0.0)
            d["mk"] = sp(jnp.where(strict, mm_nt(d["bt"], d["kt"]), 0.0))
            d["na"] = sp(jnp.where(lower, mm_nt(d["rt"], d["at"]), 0.0))
            d["nk"] = sp(jnp.where(lower, mm_nt(d["rt"], d["kt"]), 0.0))
            d["inv"] = eye_f + d["ma"]
            d["power"] = sp(d["ma"])
            d["rhs"] = mm_nt(d["bt"], d["s0"]) + mm(d["mk"], d["v"])
        for _ in range(RW_CHUNK_LOG2 - 1):
            for d in st:
                d["power"] = sp(mm(d["power"], d["power"]))
                d["inv"] = d["inv"] + mm(sp(d["inv"]), d["power"])
        outs = []
        for d in st:
            u = sp(mm(sp(d["inv"]), sp(d["rhs"])))
            o = mm_nt(d["rt"], d["s0"]) + mm(d["na"], u) + mm(d["nk"], d["v"])
            s_new = (d["s0_f"] * d["g_last"] + mm_tn(u, sp(d["at_f"] * d["g_last"]))
                     + mm_tn(d["v"], sp(d["kt_f"] * d["g_last"])))
            ln_g, ln_b, bonus, gate = d["tail"]
            mean = jnp.mean(o, axis=-1, keepdims=True)
            cen = o - mean
            var = jnp.mean(cen * cen, axis=-1, keepdims=True)
            y = cen * lax.rsqrt(var + RW_GN_EPS) * ln_g + ln_b
            outs.append((s_new, (y + bonus) * gate))
        for h, hs in enumerate(heads):
            s_scr[h] = outs[h][0]
            o_ref[0, rows, hs] = outs[h][1]
        return carry

    lax.fori_loop(0, tt // c, chunk, 0)

    @pl.when(ti == n_t - 1)
    def _():
        st_ref[0] = s_scr[...]


def rw_scan(r, dec, k, v, nkk, kka, g, bonus, ln_g, ln_b, s0):
    b, t, _ = r.shape
    tt = min(t, 256)
    n_t = t // tt
    tile = pl.BlockSpec((1, tt, RW_DIM), lambda i, j: (i, j, 0))
    vec = pl.BlockSpec((1, RW_DIM), lambda i, j: (0, 0))
    st = pl.BlockSpec((1, RW_HEADS, RW_HD, RW_HD), lambda i, j: (i, 0, 0, 0))
    return pl.pallas_call(
        functools.partial(_rw_scan_kernel, tt=tt, n_t=n_t),
        grid=(b, n_t),
        in_specs=[tile] * 8 + [vec, vec, st],
        out_specs=[tile, st],
        out_shape=[jax.ShapeDtypeStruct((b, t, RW_DIM), F32), jax.ShapeDtypeStruct(s0.shape, F32)],
        scratch_shapes=[pltpu.VMEM((RW_HEADS, RW_HD, RW_HD), F32)],
        compiler_params=_params(("parallel", "arbitrary")),
        name="rw_scan",
    )(r, dec, k, v, nkk, kka, g, bonus, ln_g, ln_b, s0)


def _pad_last(w, width):
    return jnp.pad(w, [(0, 0)] * (w.ndim - 1) + [(0, width - w.shape[-1])])


def _pad_rows(w, rows):
    return jnp.pad(w, [(0, rows - w.shape[0])] + [(0, 0)] * (w.ndim - 1))


def _split_last(z, sizes):
    return jnp.split(z, [int(c) for c in np.cumsum(np.array(sizes))[:-1]], axis=-1)


def _rw_padded(cols):
    r, k, v, wl, al, gl = _split_last(cols, RW_SIZES)
    return jnp.concatenate([r, k, v, _pad_last(wl, RW_LR_W), _pad_last(al, RW_LR_W), _pad_last(gl, RW_LR_W)], axis=-1)


def _rw_unpadded(cols):
    return jnp.concatenate([cols[..., :RW_WL0], cols[..., RW_WL0:RW_WL0 + RW_DECAY_RANK],
                            cols[..., RW_AL0:RW_AL0 + RW_A_RANK], cols[..., RW_GL0:RW_GL0 + RW_G_RANK]], axis=-1)


def _layer_params(l, p):
    q, cmp_, sel, win, gate, gq, gk, gv, ga, gg, rw, mg = _split_last(p["w_in"][l], IN_SIZES)
    seg = np.kron(np.eye(RW_HEADS, dtype=np.float32), np.ones((RW_HD, RW_HD), np.float32))
    row = lambda v: v.reshape(1, -1)
    return {
        "norm1_g": p["norm1_g"][l], "norm2_g": p["norm2_g"][l],
        "w_nsa": jnp.concatenate([q, cmp_, sel, win, _pad_last(gate, GATE_W)], axis=1).astype(BF16),
        "w_gla": jnp.concatenate([gq, gk, gv, _pad_last(ga, GLA_A_W), gg], axis=1).astype(BF16),
        "w_rw": _rw_padded(rw).astype(BF16),
        "w_mg": mg.astype(BF16),
        "qk_g": p["nsa_qk_g"][l],
        "cw": {"pos2": jnp.concatenate([p["cmp_pos"][l]] * 2, axis=-1),
               "w1": p["cmp_w1"][l].reshape(2, NSA_BLOCK, NSA_HD, NSA_CMP_HID).astype(BF16),
               "w2": p["cmp_w2"][l].astype(BF16),
               "kg": row(p["nsa_qk_g"][l, 1])},
        "gla_a2": _pad_rows(p["gla_a2"][l], GLA_A_W).astype(BF16),
        "gla_ab": row(p["gla_a_b"][l]), "gla_ng": row(p["gla_norm_g"][l]),
        "rp": {"mu": row(_rw_padded(p["rw_mu"][l])), "w0": row(p["rw_w0"][l]),
               "w2": _pad_rows(p["rw_w2"][l], RW_LR_W).astype(BF16), "a0": row(p["rw_a0"][l]),
               "a2": _pad_rows(p["rw_a2"][l], RW_LR_W).astype(BF16),
               "g2": _pad_rows(p["rw_g2"][l], RW_LR_W).astype(BF16),
               "kk": row(p["rw_kk"][l]), "ka": row(p["rw_ka"][l]), "rk": row(p["rw_rk"][l]), "seg": jnp.asarray(seg)},
        "ln_g": row(p["rw_ln_g"][l]), "ln_b": row(p["rw_ln_b"][l]),
        "nsa_up": p["nsa_up"][l].astype(BF16), "gla_up": p["gla_up"][l].astype(BF16), "rw_up": p["rw_up"][l].astype(BF16),
        "w_out": p["w_out"][l].astype(BF16), "mlp_w1": p["mlp_w1"][l].astype(BF16), "mlp_w2": p["mlp_w2"][l].astype(BF16),
    }


def _project_in(x2, lp, tm):
    hn = rmsnorm_rows(x2, lp["norm1_g"], tm)
    return tuple(matmul(hn, lp[w], tm=tm) for w in ("w_nsa", "w_gla", "w_rw", "w_mg"))


def _project_out(x2, o_nsa, o_gla, o_rw, z_mg, lp, tm):
    merged = merge_branches(o_nsa, o_gla, o_rw, lp["nsa_up"], lp["gla_up"], lp["rw_up"], z_mg, tm=tm)
    x1 = matmul(merged, lp["w_out"], "residual", x2, tm=tm)
    h2 = rmsnorm_rows(x1, lp["norm2_g"], tm)
    u = matmul(h2, lp["mlp_w1"], "relu2", tm=tm)
    return matmul(u, lp["mlp_w2"], "residual", x1, tm=tm)


def _kv_rows(rows, b, t):
    return rows.reshape(b, t, 2, NSA_KV_HEADS, NSA_HD)


def _prompt_layer(x, lp):
    b, t, d = x.shape
    m = b * t
    tm = 512
    x2 = x.reshape(m, d)
    z_nsa, z_gla, z_rw, z_mg = _project_in(x2, lp, tm)
    qn, cmp_rows, sel_rows, win_rows, sel_b, win_b, gates = nsa_prep(z_nsa, lp["qk_g"])
    nblk = t // NSA_BLOCK
    k_cmp, v_cmp = compress_rows(cmp_rows, lp["cw"], nb=32)
    per_b = lambda a: a.reshape(b, t, a.shape[-1])
    o_nsa = nsa_prompt_attention(per_b(qn), per_b(gates), per_b(sel_b), per_b(win_b),
                                 k_cmp.reshape(b, nblk, NSA_KV_DIM), v_cmp.reshape(b, nblk, NSA_KV_DIM))
    o_gla, gla_st = gla_mixer(per_b(z_gla), jnp.zeros((b, GLA_HEADS, GLA_DK, GLA_DV), F32),
                              lp["gla_a2"], lp["gla_ab"], lp["gla_ng"], t_valid=t)
    z_rw3 = per_b(z_rw)
    shifted = jnp.concatenate([jnp.zeros((b, 1, RW_W), F32), z_rw3[:, :-1]], axis=1).reshape(m, RW_W)
    prep = rw_prep(z_rw, shifted, lp["rp"])
    o_rw, rw_st = rw_scan(*(per_b(a) for a in prep), lp["ln_g"], lp["ln_b"],
                          jnp.zeros((b, RW_HEADS, RW_HD, RW_HD), F32))
    y = _project_out(x2, o_nsa.reshape(m, -1), o_gla.reshape(m, -1), o_rw.reshape(m, -1), z_mg, lp, tm)
    w_keep = min(NSA_WINDOW, t)
    return (y.reshape(b, t, d), _kv_rows(cmp_rows, b, t), _kv_rows(sel_rows, b, t), _kv_rows(win_rows, b, t)[:, t - w_keep:],
            gla_st, rw_st, _rw_unpadded(z_rw3[:, -1]))


SAMPLE_ROWS = SUBLANES_BF16


def _sample_layer(x2, lp, layer, caches, page_table, nbatch):
    cache_cmp, cache_sel, cache_win, state_gla, state_rwkv, state_shift = caches
    tm = SAMPLE_ROWS
    past = page_table.shape[1] * PAGE_SIZE
    nblk = -(-(past + 1) // NSA_BLOCK)
    z_nsa, z_gla, z_rw, z_mg = _project_in(x2, lp, tm)
    qn, cmp_rows, sel_rows, win_rows, _, _, gates = nsa_prep(z_nsa, lp["qk_g"])
    kc_past, vc_past = compress_pages(cache_cmp, layer, page_table, lp["cw"])
    last_blk = jnp.pad(cmp_rows[:nbatch, None, :], ((0, 0), (0, NSA_BLOCK - 1), (0, 0))).reshape(nbatch * NSA_BLOCK, -1)
    kc_last, vc_last = compress_rows(last_blk, lp["cw"], nb=nbatch)
    nblk_pad = -(-nblk // 8) * 8
    with_last = lambda past_s, last_s: jnp.pad(jnp.concatenate([past_s, last_s[:, None, :]], axis=1),
                                               ((0, 0), (0, nblk_pad - nblk), (0, 0)))
    q16 = qn[:nbatch].astype(F32).reshape(nbatch, NSA_HEADS, NSA_HD)
    o_cmp, o_win, idx = nsa_sample_attention1(
        q16, with_last(kc_past, kc_last), with_last(vc_past, vc_last),
        cache_win.reshape(cache_win.shape[0], nbatch, cache_win.shape[2], 2 * NSA_KV_DIM), layer,
        win_rows[:nbatch, None, :], nblk, past)
    grp8 = lambda a: jnp.pad(a.reshape(nbatch, NSA_KV_HEADS, NSA_GROUP, a.shape[-1]),
                             ((0, 0), (0, 0), (0, 8 - NSA_GROUP), (0, 0)))
    gates3 = gates[:nbatch, :3 * NSA_HEADS].reshape(nbatch, 3, NSA_HEADS).transpose(0, 2, 1)
    o8 = nsa_sample_attention2(idx.reshape(-1), page_table, grp8(q16), cache_sel, layer, sel_rows[:nbatch, None, :],
                               grp8(o_cmp), grp8(o_win), grp8(_pad_last(gates3, LANES)), past)
    o_nsa = _pad_rows(o8[:, :, :NSA_GROUP].reshape(nbatch, NSA_Q_DIM), tm)
    z_gla_c = jnp.pad(z_gla[:nbatch, None, :], ((0, 0), (0, GLA_CHUNK - 1), (0, 0)))
    o_gla, gla_st = gla_mixer(z_gla_c, state_gla[layer], lp["gla_a2"], lp["gla_ab"], lp["gla_ng"], t_valid=1)
    o_gla = _pad_rows(o_gla[:, 0], tm)
    shifted = _pad_rows(_rw_padded(state_shift[layer]), tm)
    prep = rw_prep(z_rw, shifted, lp["rp"])
    prep = [jnp.pad(a[:nbatch, None, :], ((0, 0), (0, RW_CHUNK - 1), (0, 0))) for a in prep]
    o_rw, rw_st = rw_scan(*prep, lp["ln_g"], lp["ln_b"], state_rwkv[layer])
    o_rw = _pad_rows(o_rw[:, 0], tm)
    y = _project_out(x2, o_nsa, o_gla, o_rw, z_mg, lp, tm)
    return (y, _kv_rows(cmp_rows[:nbatch], nbatch, 1), _kv_rows(sel_rows[:nbatch], nbatch, 1),
            _kv_rows(win_rows[:nbatch], nbatch, 1), gla_st, rw_st, _rw_unpadded(z_rw[:nbatch]))


def kernel(x_prompt, x_sample, cache_cmp_kv, cache_sel_kv, cache_win_kv, state_gla, state_rwkv, state_rwkv_shift,
           page_table, norm1_g, w_in, nsa_qk_g, cmp_pos, cmp_w1, cmp_w2, nsa_up, gla_a2, gla_a_b, gla_norm_g, gla_up,
           rw_mu, rw_w0, rw_w2, rw_a0, rw_a2, rw_g2, rw_kk, rw_ka, rw_rk, rw_ln_g, rw_ln_b, rw_up, w_out, norm2_g,
           mlp_w1, mlp_w2):
    p = dict(norm1_g=norm1_g, w_in=w_in, nsa_qk_g=nsa_qk_g, cmp_pos=cmp_pos, cmp_w1=cmp_w1, cmp_w2=cmp_w2, nsa_up=nsa_up,
             gla_a2=gla_a2, gla_a_b=gla_a_b, gla_norm_g=gla_norm_g, gla_up=gla_up, rw_mu=rw_mu, rw_w0=rw_w0, rw_w2=rw_w2,
             rw_a0=rw_a0, rw_a2=rw_a2, rw_g2=rw_g2, rw_kk=rw_kk, rw_ka=rw_ka, rw_rk=rw_rk, rw_ln_g=rw_ln_g,
             rw_ln_b=rw_ln_b, rw_up=rw_up, w_out=w_out, norm2_g=norm2_g, mlp_w1=mlp_w1, mlp_w2=mlp_w2)
    nbatch = x_sample.shape[0]
    caches = (cache_cmp_kv, cache_sel_kv, cache_win_kv, state_gla, state_rwkv, state_rwkv_shift)
    xp = x_prompt
    xs = _pad_rows(x_sample.reshape(nbatch, D_MODEL), SAMPLE_ROWS)
    outs_p, outs_s = [], []
    for l in range(DEPTH):
        lp = _layer_params(l, p)
        xp, *rest_p = _prompt_layer(xp, lp)
        xs, *rest_s = _sample_layer(xs, lp, l, caches, page_table, nbatch)
        outs_p.append(rest_p)
        outs_s.append(rest_s)
    stack = lambda outs, i: jnp.stack([o[i] for o in outs])
    return ((xp, xs[:nbatch].reshape(x_sample.shape))
            + tuple(stack(outs_p, i) for i in range(6)) + tuple(stack(outs_s, i) for i in range(6)))
```

```python
import functools

import numpy as np
import jax
import jax.numpy as jnp
from jax import lax
from jax.experimental import pallas as pl
from jax.experimental.pallas import tpu as pltpu

F32 = jnp.float32
BF16 = jnp.bfloat16
HIGHEST = lax.Precision.HIGHEST

D_MODEL = 2048
DEPTH = 2
PAGE_SIZE = 128
NSA_HEADS = 16
NSA_KV_HEADS = 4
NSA_GROUP = NSA_HEADS // NSA_KV_HEADS
NSA_HD = 64
NSA_BLOCK = 64
NSA_TOPK = 16
NSA_WINDOW = 512
NSA_CMP_HID = 128
GLA_HEADS = 4
GLA_DK = 64
GLA_DV = 128
GLA_GATE_RANK = 16
GLA_TAU = 16.0
GLA_CHUNK = 64
RW_HEADS = 8
RW_HD = 64
RW_DECAY_RANK = 32
RW_A_RANK = 32
RW_G_RANK = 96
RW_GN_EPS = 64e-5
D_FF = 4 * D_MODEL
NORM_EPS = 1e-6

NSA_Q_DIM = NSA_HEADS * NSA_HD
NSA_KV_DIM = NSA_KV_HEADS * NSA_HD
GLA_QK_DIM = GLA_HEADS * GLA_DK
GLA_V_DIM = GLA_HEADS * GLA_DV
RW_DIM = RW_HEADS * RW_HD
RW_SIZES = (RW_DIM, RW_DIM, RW_DIM, RW_DECAY_RANK, RW_A_RANK, RW_G_RANK)
RW_PROJ = sum(RW_SIZES)
IN_SIZES = (NSA_Q_DIM, 2 * NSA_KV_DIM, 2 * NSA_KV_DIM, 2 * NSA_KV_DIM, 3 * NSA_HEADS,
            GLA_QK_DIM, GLA_QK_DIM, GLA_V_DIM, GLA_GATE_RANK, GLA_V_DIM, RW_PROJ, 3 * D_MODEL)

LANES = 128
SUBLANES_BF16 = 16
VMEM_LIMIT = 56 * 1024 * 1024

GATE_W = LANES
NSA_W = NSA_Q_DIM + 6 * NSA_KV_DIM + GATE_W
GLA_A_W = LANES
GLA_W = 2 * GLA_QK_DIM + 2 * GLA_V_DIM + GLA_A_W
RW_LR_W = LANES
RW_W = 3 * RW_DIM + 3 * RW_LR_W
MERGE_W = 3 * D_MODEL


def _params(sem):
    return pltpu.CompilerParams(dimension_semantics=sem, vmem_limit_bytes=VMEM_LIMIT)


def _largest_tile(n, cap):
    best = None
    for t in range(LANES, min(n, cap) + 1, LANES):
        if n % t == 0:
            best = t
    return best if best is not None else n


def _sigmoid(x):
    return 1.0 / (1.0 + jnp.exp(-x))


def _softplus(y):
    return jnp.maximum(y, 0.0) + jnp.log(1.0 + jnp.exp(-jnp.abs(y)))


def _masked_softmax(s, mask):
    s = jnp.where(mask, s, -jnp.inf)
    m = jnp.max(s, axis=-1, keepdims=True)
    m = jnp.where(m > -jnp.inf, m, 0.0)
    e = jnp.where(mask, jnp.exp(s - m), 0.0)
    return e / jnp.maximum(jnp.sum(e, axis=-1, keepdims=True), 1e-30)


def _masked_softmax_rows(s, mask):
    s = jnp.where(mask, s, -jnp.inf)
    m = jnp.max(s, axis=0, keepdims=True)
    m = jnp.where(m > -jnp.inf, m, 0.0)
    e = jnp.where(mask, jnp.exp(s - m), 0.0)
    return e / jnp.maximum(jnp.sum(e, axis=0, keepdims=True), 1e-30)


def _dot(a, b, precision=None):
    return jnp.dot(a, b, preferred_element_type=F32, precision=precision)


def _dot_nt(a, b, precision=None):
    return lax.dot_general(a, b, (((1,), (1,)), ((), ())), preferred_element_type=F32, precision=precision)


def _dot_tn(a, b, precision=None):
    return lax.dot_general(a, b, (((0,), (0,)), ((), ())), preferred_element_type=F32, precision=precision)


def _split2(x):
    hi = x.astype(BF16)
    return hi, (x - hi.astype(F32)).astype(BF16)


def _split3(x):
    hi = x.astype(BF16)
    rest = x - hi.astype(F32)
    mid = rest.astype(BF16)
    return hi, mid, (rest - mid.astype(F32)).astype(BF16)


def _dot3(dot, a, b):
    (a_hi, a_lo), (b_hi, b_lo) = a, b
    return dot(a_hi, b_hi) + dot(a_hi, b_lo) + dot(a_lo, b_hi)


def _rmsnorm_kernel(x_ref, g_ref, o_ref):
    x = x_ref[...]
    y = x * lax.rsqrt(jnp.mean(x * x, axis=-1, keepdims=True) + NORM_EPS) * g_ref[...]
    o_ref[...] = y.astype(o_ref.dtype)


def rmsnorm_rows(x, g, tm):
    m, d = x.shape
    return pl.pallas_call(
        _rmsnorm_kernel,
        grid=(m // tm,),
        in_specs=[pl.BlockSpec((tm, d), lambda i: (i, 0)), pl.BlockSpec((1, d), lambda i: (0, 0))],
        out_specs=pl.BlockSpec((tm, d), lambda i: (i, 0)),
        out_shape=jax.ShapeDtypeStruct((m, d), BF16),
        compiler_params=_params(("parallel",)),
        name="rmsnorm_rows",
    )(x, g.reshape(1, d))


def _mm_kernel(*refs, nk, mode):
    if mode == "residual":
        a_ref, w_ref, r_ref, o_ref = refs[:4]
        scratch = refs[4:]
    else:
        a_ref, w_ref, o_ref = refs[:3]
        scratch = refs[3:]
    part = _dot(a_ref[...], w_ref[...])

    def finish(acc):
        if mode == "relu2":
            acc = jnp.square(jnp.maximum(acc, 0.0))
        elif mode == "residual":
            acc = r_ref[...] + acc
        o_ref[...] = acc.astype(o_ref.dtype)

    if nk == 1:
        finish(part)
    else:
        acc_ref = scratch[0]
        k = pl.program_id(2)

        @pl.when(k == 0)
        def _():
            acc_ref[...] = part

        @pl.when(k > 0)
        def _():
            acc_ref[...] += part

        @pl.when(k == nk - 1)
        def _():
            finish(acc_ref[...])


def matmul(a, w, mode="plain", residual=None, tm=512):
    m, kdim = a.shape
    n = w.shape[1]
    tm = min(tm, m)
    tk = min(kdim, 2048)
    nk = kdim // tk
    tn = _largest_tile(n, 1024 if (mode == "residual" or nk > 1) else 2048)
    in_specs = [pl.BlockSpec((tm, tk), lambda i, j, k: (i, k)), pl.BlockSpec((tk, tn), lambda i, j, k: (k, j))]
    args = [a, w]
    if mode == "residual":
        in_specs.append(pl.BlockSpec((tm, tn), lambda i, j, k: (i, j)))
        args.append(residual)
    out_dtype = BF16 if mode == "relu2" else F32
    return pl.pallas_call(
        functools.partial(_mm_kernel, nk=nk, mode=mode),
        grid=(m // tm, n // tn, nk),
        in_specs=in_specs,
        out_specs=pl.BlockSpec((tm, tn), lambda i, j, k: (i, j)),
        out_shape=jax.ShapeDtypeStruct((m, n), out_dtype),
        scratch_shapes=[pltpu.VMEM((tm, tn), F32)] if nk > 1 else [],
        compiler_params=_params(("parallel", "parallel", "arbitrary")),
        name="matmul_" + mode,
    )(*args)


def _merge_kernel(on_ref, og_ref, or_ref, wn_ref, wg_ref, wr_ref, m0_ref, m1_ref, m2_ref, o_ref):
    acc = _sigmoid(m0_ref[...]) * _dot(on_ref[...].astype(BF16), wn_ref[...])
    acc += _sigmoid(m1_ref[...]) * _dot(og_ref[...].astype(BF16), wg_ref[...])
    acc += _sigmoid(m2_ref[...]) * _dot(or_ref[...].astype(BF16), wr_ref[...])
    o_ref[...] = acc.astype(o_ref.dtype)


def merge_branches(o_nsa, o_gla, o_rw, w_nsa, w_gla, w_rw, z_merge, tm=512, tn=1024):
    m = o_nsa.shape[0]
    tm = min(tm, m)
    nj = D_MODEL // tn
    row = lambda kd: pl.BlockSpec((tm, kd), lambda i, j: (i, 0))
    wsp = lambda kd: pl.BlockSpec((kd, tn), lambda i, j: (0, j))
    gate = lambda b: pl.BlockSpec((tm, tn), lambda i, j: (i, j + b * nj))
    return pl.pallas_call(
        _merge_kernel,
        grid=(m // tm, nj),
        in_specs=[row(NSA_Q_DIM), row(GLA_V_DIM), row(RW_DIM), wsp(NSA_Q_DIM), wsp(GLA_V_DIM), wsp(RW_DIM),
                  gate(0), gate(1), gate(2)],
        out_specs=pl.BlockSpec((tm, tn), lambda i, j: (i, j)),
        out_shape=jax.ShapeDtypeStruct((m, D_MODEL), BF16),
        compiler_params=_params(("parallel", "parallel")),
        name="merge_branches",
    )(o_nsa, o_gla, o_rw, w_nsa, w_gla, w_rw, z_merge, z_merge, z_merge)


def _head_rmsnorm(x, g):
    return x * lax.rsqrt(jnp.mean(x * x, axis=-1, keepdims=True) + NORM_EPS) * g


def _nsa_prep_kernel(z_ref, g_ref, qn_ref, cmp_ref, sel_ref, win_ref, selb_ref, winb_ref, gates_ref):
    hd = NSA_HD
    g_q, g_sel, g_win = g_ref[0:1, :], g_ref[2:3, :], g_ref[3:4, :]
    scale = NSA_HD ** -0.5
    for h in range(NSA_HEADS):
        q = _head_rmsnorm(z_ref[:, h * hd:(h + 1) * hd], g_q)
        qn_ref[:, h * hd:(h + 1) * hd] = (q * scale).astype(BF16)
    c0 = NSA_Q_DIM
    cmp_ref[...] = z_ref[:, c0:c0 + 2 * NSA_KV_DIM]
    for (off, gain, o_ref, ob_ref) in ((c0 + 2 * NSA_KV_DIM, g_sel, sel_ref, selb_ref),
                                       (c0 + 4 * NSA_KV_DIM, g_win, win_ref, winb_ref)):
        for h in range(NSA_KV_HEADS):
            k = _head_rmsnorm(z_ref[:, off + h * hd:off + (h + 1) * hd], gain)
            o_ref[:, h * hd:(h + 1) * hd] = k
            ob_ref[:, h * hd:(h + 1) * hd] = k.astype(BF16)
        v = z_ref[:, off + NSA_KV_DIM:off + 2 * NSA_KV_DIM]
        o_ref[:, NSA_KV_DIM:] = v
        ob_ref[:, NSA_KV_DIM:] = v.astype(BF16)
    gates_ref[...] = _sigmoid(z_ref[:, c0 + 6 * NSA_KV_DIM:])


def nsa_prep(z_nsa, qk_g, tm=256):
    m = z_nsa.shape[0]
    tm = min(tm, m)
    kvw = 2 * NSA_KV_DIM
    row = lambda w: pl.BlockSpec((tm, w), lambda i: (i, 0))
    return pl.pallas_call(
        _nsa_prep_kernel,
        grid=(m // tm,),
        in_specs=[row(NSA_W), pl.BlockSpec((4, NSA_HD), lambda i: (0, 0))],
        out_specs=[row(NSA_Q_DIM), row(kvw), row(kvw), row(kvw), row(kvw), row(kvw), row(GATE_W)],
        out_shape=[jax.ShapeDtypeStruct((m, NSA_Q_DIM), BF16),
                   jax.ShapeDtypeStruct((m, kvw), F32), jax.ShapeDtypeStruct((m, kvw), F32),
                   jax.ShapeDtypeStruct((m, kvw), F32), jax.ShapeDtypeStruct((m, kvw), BF16),
                   jax.ShapeDtypeStruct((m, kvw), BF16), jax.ShapeDtypeStruct((m, GATE_W), F32)],
        compiler_params=_params(("parallel",)),
        name="nsa_prep",
    )(z_nsa, qk_g)


QUARTERS = 2 * NSA_KV_DIM // LANES


def _compress_compute(x_ref, nb, pos_ref, w1_ref, w2_ref, kg_ref, kc_ref, vc_ref, acc_ref):
    acc_ref[...] = jnp.zeros_like(acc_ref)
    row_stride = NSA_BLOCK * QUARTERS

    def body(l, carry):
        for q in range(QUARTERS):
            kv = q // 2
            x = x_ref[pl.ds(l * QUARTERS + q, nb, stride=row_stride), :] + pos_ref[kv, pl.ds(l, 1), :]
            for half in range(2):
                acc_ref[q * 2 + half] += _dot(x[:, half * NSA_HD:(half + 1) * NSA_HD].astype(BF16), w1_ref[kv, l])
        return carry

    lax.fori_loop(0, NSA_BLOCK, body, 0)
    for q in range(QUARTERS):
        kv = q // 2
        for half in range(2):
            g = (q % 2) * 2 + half
            hid = jax.nn.gelu(acc_ref[q * 2 + half], approximate=True)
            s = _dot(hid.astype(BF16), w2_ref[kv])
            if kv == 0:
                kc_ref[:, g * NSA_HD:(g + 1) * NSA_HD] = _head_rmsnorm(s, kg_ref[...])
            else:
                vc_ref[:, g * NSA_HD:(g + 1) * NSA_HD] = s


def _compress_kernel(x_ref, pos_ref, w1_ref, w2_ref, kg_ref, kc_ref, vc_ref, acc_ref, *, nb):
    _compress_compute(x_ref, nb, pos_ref, w1_ref, w2_ref, kg_ref, kc_ref, vc_ref, acc_ref)


def _cmp_weight_specs(nidx):
    zero = lambda n: (lambda *idx: (0,) * n)
    return [pl.BlockSpec((2, NSA_BLOCK, LANES), zero(3)),
            pl.BlockSpec((2, NSA_BLOCK, NSA_HD, NSA_CMP_HID), zero(4)),
            pl.BlockSpec((2, NSA_CMP_HID, NSA_HD), zero(3)),
            pl.BlockSpec((1, NSA_HD), zero(2))]


def compress_rows(rows, cw, nb):
    nblk = rows.shape[0] // NSA_BLOCK
    nb = min(nb, nblk)
    x4 = rows.reshape(nblk * NSA_BLOCK * QUARTERS, LANES)
    out = jax.ShapeDtypeStruct((nblk, NSA_KV_DIM), F32)
    return pl.pallas_call(
        functools.partial(_compress_kernel, nb=nb),
        grid=(nblk // nb,),
        in_specs=[pl.BlockSpec((nb * NSA_BLOCK * QUARTERS, LANES), lambda i: (i, 0))] + _cmp_weight_specs(1),
        out_specs=[pl.BlockSpec((nb, NSA_KV_DIM), lambda i: (i, 0))] * 2,
        out_shape=[out, out],
        scratch_shapes=[pltpu.VMEM((2 * QUARTERS, nb, NSA_CMP_HID), F32)],
        compiler_params=_params(("parallel",)),
        name="compress_rows",
    )(x4, cw["pos2"], cw["w1"], cw["w2"], cw["kg"])


PAGES_PER_STEP = 32
BLOCKS_PER_PAGE = PAGE_SIZE // NSA_BLOCK


SLABS_PER_PAGE = 2 * NSA_KV_HEADS
PAGE_T_ROWS = SLABS_PER_PAGE * NSA_HD
assert PAGE_SIZE == LANES and BLOCKS_PER_PAGE == 2


def _compress_pages_kernel(pt_ref, cache_ref, pos_ref, wd_ref, w2_ref, kg_ref, o_ref, buf, sem, acc_ref, *,
                           layer, npg, n_c, n_steps):
    b, c = pl.program_id(0), pl.program_id(1)
    step = b * n_c + c
    slot = step % 2
    nslab = npg * SLABS_PER_PAGE

    def page_copy(bb, cc, sl, p):
        dst = buf.at[pl.ds(pl.multiple_of((sl * npg + p) * PAGE_T_ROWS, PAGE_T_ROWS), PAGE_T_ROWS), :]
        return pltpu.make_async_copy(cache_ref.at[layer, pt_ref[bb, cc * npg + p]], dst, sem.at[sl])

    @pl.when(step == 0)
    def _():
        for p in range(npg):
            page_copy(0, 0, 0, p).start()

    @pl.when(step + 1 < n_steps)
    def _():
        nxt = step + 1
        for p in range(npg):
            page_copy(nxt // n_c, nxt % n_c, 1 - slot, p).start()

    for p in range(npg):
        page_copy(b, c, slot, p).wait()

    base = slot * (npg * PAGE_T_ROWS)
    acc_ref[...] = jnp.zeros_like(acc_ref)

    def feature(d, carry):
        x = buf[pl.ds(base + d, nslab, stride=NSA_HD), :]
        x = (x.reshape(npg, SLABS_PER_PAGE, LANES) + pos_ref[d][None]).reshape(nslab, LANES)
        acc_ref[...] += _dot(x.astype(BF16), wd_ref[d])
        return carry

    lax.fori_loop(0, NSA_HD, feature, 0)
    is_k = (lax.broadcasted_iota(jnp.int32, (nslab, 1), 0) % SLABS_PER_PAGE) < NSA_KV_HEADS
    half = BLOCKS_PER_PAGE * NSA_CMP_HID
    hid = jax.nn.gelu(jnp.where(is_k, acc_ref[:, :half], acc_ref[:, half:]), approximate=True).astype(BF16)
    k_out = _dot(hid, w2_ref[0])
    v_out = _dot(hid, w2_ref[1])
    k_out = jnp.concatenate([_head_rmsnorm(k_out[:, :NSA_HD], kg_ref[...]), _head_rmsnorm(k_out[:, NSA_HD:], kg_ref[...])],
                            axis=1)
    o_ref[0] = jnp.where(is_k, k_out, v_out)


def compress_pages(cache, layer, page_table, cw):
    nbatch, n_pages = page_table.shape
    npg = min(PAGES_PER_STEP, n_pages)
    n_c = n_pages // npg
    nslab = npg * SLABS_PER_PAGE
    cache_t = jnp.transpose(cache, (0, 1, 3, 4, 5, 2)).reshape(cache.shape[0], cache.shape[1], PAGE_T_ROWS, PAGE_SIZE)
    const = lambda shape: pl.BlockSpec(shape, lambda b, c, pt: (0,) * len(shape))
    grid_spec = pltpu.PrefetchScalarGridSpec(
        num_scalar_prefetch=1,
        grid=(nbatch, n_c),
        in_specs=[pl.BlockSpec(memory_space=pl.ANY), const((NSA_HD, SLABS_PER_PAGE, LANES)),
                  const((NSA_HD, LANES, 2 * BLOCKS_PER_PAGE * NSA_CMP_HID)),
                  const((2, BLOCKS_PER_PAGE * NSA_CMP_HID, BLOCKS_PER_PAGE * NSA_HD)), const((1, NSA_HD))],
        out_specs=pl.BlockSpec((1, nslab, BLOCKS_PER_PAGE * NSA_HD), lambda b, c, pt: (b, c, 0)),
        scratch_shapes=[pltpu.VMEM((2 * npg * PAGE_T_ROWS, PAGE_SIZE), F32), pltpu.SemaphoreType.DMA((2,)),
                        pltpu.VMEM((nslab, 2 * BLOCKS_PER_PAGE * NSA_CMP_HID), F32)],
    )
    out = pl.pallas_call(
        functools.partial(_compress_pages_kernel, layer=layer, npg=npg, n_c=n_c, n_steps=nbatch * n_c),
        grid_spec=grid_spec,
        out_shape=jax.ShapeDtypeStruct((nbatch, n_pages * SLABS_PER_PAGE, BLOCKS_PER_PAGE * NSA_HD), F32),
        compiler_params=_params(("arbitrary", "arbitrary")),
        name="compress_pages",
    )(page_table, cache_t, cw["pos_t"], cw["wd"], cw["w2bd"], cw["kg"])
    out = out.reshape(nbatch, n_pages, 2, NSA_KV_HEADS, BLOCKS_PER_PAGE, NSA_HD).transpose(2, 0, 1, 4, 3, 5)
    out = out.reshape(2, nbatch, n_pages * BLOCKS_PER_PAGE, NSA_KV_DIM)
    return out[0], out[1]


def _select_blocks(score, blk, cur, nblk):
    forced = (blk == 0) | (blk == cur) | (blk == cur - 1)
    score = jnp.where(forced, jnp.inf, score)
    score = jnp.where(blk <= cur, score, -jnp.inf)
    rank = jnp.zeros(score.shape, F32)
    for i in range(nblk):
        row = score[i:i + 1, :]
        ahead = (row > score) | ((row == score) & (blk > i))
        rank = rank + jnp.where(ahead, 1.0, 0.0)
    return (rank < float(min(NSA_TOPK, nblk))) & (blk <= cur)


def _alibi_slope(head):
    return 2.0 ** (-8.0 * (head + 1) / NSA_HEADS)


FAR = 1e38
M_INIT = -1e30


def _nsa_prompt_kernel(q_ref, gates_ref, selb_ref, winb_ref, kc_ref, vc_ref, o_ref, *, tq, t_len, band, kchunk):
    qi = pl.program_id(1)
    q0 = qi * tq
    nblk = t_len // NSA_BLOCK
    hd = NSA_HD
    q_pos = q0 + lax.broadcasted_iota(jnp.int32, (tq, 1), 0)
    q_pos_row = q0 + lax.broadcasted_iota(jnp.int32, (1, tq), 1)
    cur = q_pos_row // NSA_BLOCK
    blk = lax.broadcasted_iota(jnp.int32, (nblk, 1), 0)
    d_cmp = q_pos_row - ((blk + 1) * NSA_BLOCK - 1)
    m_cmp = d_cmp >= 0
    d_cmp_f = d_cmp.astype(F32)
    w0 = pl.multiple_of(jnp.maximum(q0 - NSA_WINDOW, 0), tq)
    d_win = q_pos - (w0 + lax.broadcasted_iota(jnp.int32, (1, band), 1))
    d_win_m = jnp.where((d_win >= 0) & (d_win < NSA_WINDOW), d_win.astype(F32), FAR)
    blk_row = lax.broadcasted_iota(jnp.int32, (nblk, kchunk), 0)
    key_col = lax.broadcasted_iota(jnp.int32, (nblk, kchunk), 1)
    key_lane = lax.broadcasted_iota(jnp.int32, (1, kchunk), 1)
    n_chunks = (q0 + tq + kchunk - 1) // kchunk
    for g in range(NSA_KV_HEADS):
        kc = kc_ref[0, :, g * hd:(g + 1) * hd].astype(BF16)
        vc = vc_ref[0, :, g * hd:(g + 1) * hd].astype(BF16)
        k_win = winb_ref[0, pl.ds(w0, band), g * hd:(g + 1) * hd]
        v_win = winb_ref[0, pl.ds(w0, band), NSA_KV_DIM + g * hd:NSA_KV_DIM + (g + 1) * hd]
        heads = [g * NSA_GROUP + r for r in range(NSA_GROUP)]
        group = range(NSA_GROUP)
        qs = [q_ref[0, :, h * hd:(h + 1) * hd] for h in heads]
        slopes = [_alibi_slope(h) for h in heads]
        p_cmp = [_masked_softmax_rows(_dot_nt(kc, qs[r]) - slopes[r] * d_cmp_f, m_cmp) for r in group]
        o_cmp = [_dot_tn(p_cmp[r].astype(BF16), vc) for r in group]
        sel_f = jnp.where(_select_blocks(sum(p_cmp), blk, cur, nblk), 1.0, 0.0).astype(BF16)

        def sel_chunk(ci, carry):
            k0 = pl.multiple_of(ci * kchunk, kchunk)
            k_c = selb_ref[0, pl.ds(k0, kchunk), g * hd:(g + 1) * hd]
            v_c = selb_ref[0, pl.ds(k0, kchunk), NSA_KV_DIM + g * hd:NSA_KV_DIM + (g + 1) * hd]
            expand = ((k0 + key_col) // NSA_BLOCK == blk_row).astype(BF16)
            d = q_pos - (k0 + key_lane)
            d_m = jnp.where((_dot_tn(sel_f, expand) > 0.5) & (d >= 0), d.astype(F32), FAR)
            s = [_dot_nt(qs[r], k_c) - slopes[r] * d_m for r in group]
            m_new = [jnp.maximum(carry[r][0], jnp.max(s[r], axis=-1, keepdims=True)) for r in group]
            p = [jnp.exp(s[r] - m_new[r]) for r in group]
            alpha = [jnp.exp(carry[r][0] - m_new[r]) for r in group]
            pv = [_dot(p[r].astype(BF16), v_c) for r in group]
            return tuple((m_new[r], alpha[r] * carry[r][1] + jnp.sum(p[r], axis=-1, keepdims=True),
                          alpha[r] * carry[r][2] + pv[r]) for r in group)

        init = tuple((jnp.full((tq, 1), M_INIT, F32), jnp.zeros((tq, 1), F32), jnp.zeros((tq, hd), F32))
                     for _ in group)
        sel_state = lax.fori_loop(0, n_chunks, sel_chunk, init)
        s = [_dot_nt(qs[r], k_win) - slopes[r] * d_win_m for r in group]
        p = [jnp.exp(s[r] - jnp.maximum(jnp.max(s[r], axis=-1, keepdims=True), M_INIT)) for r in group]
        pv = [_dot(p[r].astype(BF16), v_win) for r in group]
        for r, h in enumerate(heads):
            o_sel = sel_state[r][2] / jnp.maximum(sel_state[r][1], 1e-30)
            o_win = pv[r] / jnp.maximum(jnp.sum(p[r], axis=-1, keepdims=True), 1e-30)
            gate = lambda b: gates_ref[0, :, b * NSA_HEADS + h:b * NSA_HEADS + h + 1]
            o = gate(0) * o_cmp[r] + gate(1) * o_sel + gate(2) * o_win
            o_ref[0, :, h * hd:(h + 1) * hd] = o.astype(o_ref.dtype)


def nsa_prompt_attention(qn, gates, sel_b, win_b, k_cmp, v_cmp, tq=128):
    b, t, _ = qn.shape
    tq = min(tq, t)
    band = min(NSA_WINDOW + tq, t)
    nblk = t // NSA_BLOCK
    full = lambda w, n: pl.BlockSpec((1, n, w), lambda i, j: (i, 0, 0))
    tile = lambda w: pl.BlockSpec((1, tq, w), lambda i, j: (i, j, 0))
    return pl.pallas_call(
        functools.partial(_nsa_prompt_kernel, tq=tq, t_len=t, band=band, kchunk=min(256, t)),
        grid=(b, t // tq),
        in_specs=[tile(NSA_Q_DIM), tile(GATE_W), full(2 * NSA_KV_DIM, t), full(2 * NSA_KV_DIM, t),
                  full(NSA_KV_DIM, nblk), full(NSA_KV_DIM, nblk)],
        out_specs=tile(NSA_Q_DIM),
        out_shape=jax.ShapeDtypeStruct((b, t, NSA_Q_DIM), BF16),
        compiler_params=_params(("parallel", "arbitrary")),
        name="nsa_prompt_attention",
    )(qn, gates, sel_b, win_b, k_cmp, v_cmp)


def _head_group_mask(nrows):
    row = lax.broadcasted_iota(jnp.int32, (nrows, 1), 0)
    return [(row // NSA_GROUP) == g for g in range(NSA_KV_HEADS)]


def _slope_column(nrows, head0=0):
    row = lax.broadcasted_iota(jnp.int32, (nrows, 1), 0)
    col = jnp.zeros((nrows, 1), F32)
    for h in range(NSA_HEADS):
        col = jnp.where(row + head0 == h, 2.0 ** (-8.0 * (h + 1) / NSA_HEADS), col)
    return col


def _nsa_sample1_kernel(q_ref, kc_ref, vc_ref, wcache_ref, wnew_ref, ocmp_ref, owin_ref, idx_ref, *,
                        nblk, nblk_pad, past, w_buf):
    hd = NSA_HD
    q = q_ref[0]
    groups = _head_group_mask(NSA_HEADS)
    slope = _slope_column(NSA_HEADS)
    q_pos = past
    cur = q_pos // NSA_BLOCK
    blk = lax.broadcasted_iota(jnp.int32, (1, nblk_pad), 1)
    d_cmp = q_pos - ((blk + 1) * NSA_BLOCK - 1)
    s = jnp.zeros((NSA_HEADS, nblk_pad), F32)
    for g in range(NSA_KV_HEADS):
        s = s + jnp.where(groups[g], _dot_nt(q, kc_ref[0, :, g * hd:(g + 1) * hd]), 0.0)
    p = _masked_softmax(s - slope * d_cmp.astype(F32), (d_cmp >= 0) & (blk < nblk))
    o = jnp.zeros((NSA_HEADS, hd), F32)
    for g in range(NSA_KV_HEADS):
        o = o + jnp.where(groups[g], _dot(p, vc_ref[0, :, g * hd:(g + 1) * hd]), 0.0)
    ocmp_ref[0] = o
    gsel = (lax.broadcasted_iota(jnp.int32, (8, NSA_HEADS), 1) // NSA_GROUP
            == lax.broadcasted_iota(jnp.int32, (8, NSA_HEADS), 0)).astype(F32)
    score = _dot(gsel, p, HIGHEST)
    forced = (blk == 0) | (blk == cur) | (blk == cur - 1)
    score = jnp.where(forced, jnp.inf, score)
    score = jnp.where((blk <= cur) & (blk < nblk), score, -jnp.inf)
    blk_f = blk.astype(F32)
    lane_k = lax.broadcasted_iota(jnp.int32, (1, NSA_TOPK), 1)
    idx = jnp.zeros((8, NSA_TOPK), F32)
    for k in range(NSA_TOPK):
        m = jnp.max(score, axis=-1, keepdims=True)
        first = jnp.min(jnp.where(score == m, blk_f, 1e9), axis=-1, keepdims=True)
        idx = jnp.where(lane_k == k, first, idx)
        score = jnp.where(blk_f == first, -jnp.inf, score)
    idx_ref[0] = idx.astype(jnp.int32)
    i_buf = lax.broadcasted_iota(jnp.int32, (1, w_buf), 1)
    win_pos = past - w_buf + i_buf
    d_win = q_pos - win_pos
    m_win = (d_win >= 0) & (d_win < NSA_WINDOW) & (win_pos >= 0)
    s = jnp.zeros((NSA_HEADS, w_buf), F32)
    k_new = jnp.zeros((NSA_HEADS, hd), F32)
    v_new = jnp.zeros((NSA_HEADS, hd), F32)
    for g in range(NSA_KV_HEADS):
        s = s + jnp.where(groups[g], _dot_nt(q, wcache_ref[0, 0, :, g * hd:(g + 1) * hd]), 0.0)
        k_new = k_new + jnp.where(groups[g], wnew_ref[0, :, g * hd:(g + 1) * hd], 0.0)
        v_new = v_new + jnp.where(groups[g], wnew_ref[0, :, NSA_KV_DIM + g * hd:NSA_KV_DIM + (g + 1) * hd], 0.0)
    s = jnp.where(m_win, s - slope * d_win.astype(F32), -jnp.inf)
    s_new = jnp.sum(q * k_new, axis=-1, keepdims=True)
    m = jnp.maximum(jnp.max(s, axis=-1, keepdims=True), s_new)
    e = jnp.where(m_win, jnp.exp(s - m), 0.0)
    e_new = jnp.exp(s_new - m)
    denom = jnp.maximum(jnp.sum(e, axis=-1, keepdims=True) + e_new, 1e-30)
    o = e_new * v_new
    for g in range(NSA_KV_HEADS):
        o = o + jnp.where(groups[g], _dot(e, wcache_ref[0, 0, :, NSA_KV_DIM + g * hd:NSA_KV_DIM + (g + 1) * hd]), 0.0)
    owin_ref[0] = o / denom


def nsa_sample_attention1(q, k_cmp, v_cmp, cache_win, layer, win_new, nblk, past):
    nbatch = q.shape[0]
    nblk_pad = k_cmp.shape[1]
    w_buf = cache_win.shape[2]
    hq = pl.BlockSpec((1, NSA_HEADS, NSA_HD), lambda b: (b, 0, 0))
    return pl.pallas_call(
        functools.partial(_nsa_sample1_kernel, nblk=nblk, nblk_pad=nblk_pad, past=past, w_buf=w_buf),
        grid=(nbatch,),
        in_specs=[hq, pl.BlockSpec((1, nblk_pad, NSA_KV_DIM), lambda b: (b, 0, 0)),
                  pl.BlockSpec((1, nblk_pad, NSA_KV_DIM), lambda b: (b, 0, 0)),
                  pl.BlockSpec((1, 1, w_buf, 2 * NSA_KV_DIM), lambda b: (layer, b, 0, 0)),
                  pl.BlockSpec((1, 1, 2 * NSA_KV_DIM), lambda b: (b, 0, 0))],
        out_specs=[hq, hq, pl.BlockSpec((1, 8, NSA_TOPK), lambda b: (b, 0, 0))],
        out_shape=[jax.ShapeDtypeStruct((nbatch, NSA_HEADS, NSA_HD), F32),
                   jax.ShapeDtypeStruct((nbatch, NSA_HEADS, NSA_HD), F32),
                   jax.ShapeDtypeStruct((nbatch, 8, NSA_TOPK), jnp.int32)],
        compiler_params=_params(("parallel",)),
        name="nsa_sample_attention1",
    )(q, k_cmp, v_cmp, cache_win, win_new)


def _nsa_sample2_kernel(idx_ref, pt_ref, q_ref, b0_ref, b1_ref, b2_ref, b3_ref, new_ref, ocmp_ref, owin_ref, gates_ref,
                        o_ref, m_scr, l_scr, acc_scr, *, past, n_cache_blk):
    hd = NSA_HD
    b = pl.program_id(0)
    k = pl.program_id(1)
    blocks = (b0_ref, b1_ref, b2_ref, b3_ref)

    @pl.when(k == 0)
    def _():
        m_scr[...] = jnp.full(m_scr.shape, -jnp.inf, F32)
        l_scr[...] = jnp.zeros(l_scr.shape, F32)
        acc_scr[...] = jnp.zeros(acc_scr.shape, F32)

    q_pos = past
    lane = lax.broadcasted_iota(jnp.int32, (1, PAGE_SIZE), 1)
    for g in range(NSA_KV_HEADS):
        j = idx_ref[(b * 8 + g) * NSA_TOPK + k]
        slope = _slope_column(8, head0=g * NSA_GROUP)
        q = q_ref[0, g]
        k_t = blocks[g][0, 0, 0, 0]
        v_t = blocks[g][0, 0, 1, 0]
        d = q_pos - ((j // BLOCKS_PER_PAGE) * PAGE_SIZE + lane)
        in_block = (lane // NSA_BLOCK) == (j % BLOCKS_PER_PAGE)
        valid = (d >= 0) & in_block & (j < n_cache_blk)
        s = jnp.where(valid, _dot(q, k_t) - slope * d.astype(F32), -jnp.inf)
        k_new = new_ref[0, :, g * hd:(g + 1) * hd]
        v_new = new_ref[0, :, NSA_KV_DIM + g * hd:NSA_KV_DIM + (g + 1) * hd]
        new_ok = j == q_pos // NSA_BLOCK
        s_new = jnp.where(new_ok, jnp.sum(q * k_new, axis=-1, keepdims=True), -jnp.inf)
        m_old = m_scr[g]
        m_new = jnp.maximum(jnp.maximum(m_old, jnp.max(s, axis=-1, keepdims=True)), s_new)
        m_safe = jnp.where(m_new > -jnp.inf, m_new, 0.0)
        alpha = jnp.exp(m_old - m_safe)
        p = jnp.where(valid, jnp.exp(s - m_safe), 0.0)
        p_new = jnp.where(new_ok, jnp.exp(s_new - m_safe), 0.0)
        l_scr[g] = alpha * l_scr[g] + jnp.sum(p, axis=-1, keepdims=True) + p_new
        acc_scr[g] = alpha * acc_scr[g] + _dot_nt(p, v_t) + p_new * v_new
        m_scr[g] = m_new

    @pl.when(k == NSA_TOPK - 1)
    def _():
        for g in range(NSA_KV_HEADS):
            o_sel = acc_scr[g] / jnp.maximum(l_scr[g], 1e-30)
            gt = gates_ref[0, g]
            o_ref[0, g] = gt[:, 0:1] * ocmp_ref[0, g] + gt[:, 1:2] * o_sel + gt[:, 2:3] * owin_ref[0, g]


def nsa_sample_attention2(idx, page_table, q8, cache_sel, layer, sel_new, o_cmp8, o_win8, gates8, past):
    nbatch = q8.shape[0]
    n_cache_blk = past // NSA_BLOCK
    cache_t = jnp.transpose(cache_sel, (0, 1, 3, 4, 5, 2))

    def blk_map(g):
        def index_map(b, k, idx_ref, pt_ref):
            j = jnp.minimum(idx_ref[(b * 8 + g) * NSA_TOPK + k], n_cache_blk - 1)
            return (layer, pt_ref[b, j // BLOCKS_PER_PAGE], 0, g, 0, 0)
        return index_map

    g4 = lambda w: pl.BlockSpec((1, NSA_KV_HEADS, 8, w), lambda b, k, i, p: (b, 0, 0, 0))
    grid_spec = pltpu.PrefetchScalarGridSpec(
        num_scalar_prefetch=2,
        grid=(nbatch, NSA_TOPK),
        in_specs=[g4(NSA_HD)] + [pl.BlockSpec((1, 1, 2, 1, NSA_HD, PAGE_SIZE), blk_map(g)) for g in range(NSA_KV_HEADS)]
                 + [pl.BlockSpec((1, 1, 2 * NSA_KV_DIM), lambda b, k, i, p: (b, 0, 0)), g4(NSA_HD), g4(NSA_HD), g4(LANES)],
        out_specs=g4(NSA_HD),
        scratch_shapes=[pltpu.VMEM((NSA_KV_HEADS, 8, 1), F32), pltpu.VMEM((NSA_KV_HEADS, 8, 1), F32),
                        pltpu.VMEM((NSA_KV_HEADS, 8, NSA_HD), F32)],
    )
    return pl.pallas_call(
        functools.partial(_nsa_sample2_kernel, past=past, n_cache_blk=n_cache_blk),
        grid_spec=grid_spec,
        out_shape=jax.ShapeDtypeStruct((nbatch, NSA_KV_HEADS, 8, NSA_HD), F32),
        compiler_params=_params(("parallel", "arbitrary")),
        name="nsa_sample_attention2",
    )(idx, page_table, q8, cache_t, cache_t, cache_t, cache_t, sel_new, o_cmp8, o_win8, gates8)


GLA_Q0, GLA_K0, GLA_V0 = 0, GLA_QK_DIM, 2 * GLA_QK_DIM
GLA_A0 = GLA_V0 + GLA_V_DIM
GLA_G0 = GLA_A0 + GLA_A_W


def _gla_kernel(z_ref, s0_ref, a2_ref, ab_ref, ng_ref, o_ref, st_ref, s_scr, *, tt, t_valid, n_t):
    ti = pl.program_id(1)
    c = GLA_CHUNK

    @pl.when(ti == 0)
    def _():
        for h in range(GLA_HEADS):
            s_scr[h] = s0_ref[0, h].T

    a_lr = z_ref[0, :, GLA_A0:GLA_A0 + GLA_A_W].astype(BF16)
    pre = _dot(a_lr, a2_ref[...]) + ab_ref[...]
    log_a = (jnp.minimum(pre, 0.0) - jnp.log(1.0 + jnp.exp(-jnp.abs(pre)))) / GLA_TAU
    row = ti * tt + lax.broadcasted_iota(jnp.int32, (tt, 1), 0)
    log_a = jnp.where(row < t_valid, log_a, 0.0)
    tri = lax.broadcasted_iota(jnp.int32, (c, c), 0) >= lax.broadcasted_iota(jnp.int32, (c, c), 1)
    tri_f = tri.astype(F32)
    hr = range(GLA_HEADS)
    ksl = [slice(h * GLA_DK, (h + 1) * GLA_DK) for h in hr]
    states = [s_scr[h] for h in hr]
    for ci in range(tt // c):
        rows = slice(ci * c, (ci + 1) * c)
        cum = [_dot(tri_f, log_a[rows, ksl[h]], HIGHEST) for h in hr]
        last = [cum[h][c - 1:c, :] for h in hr]
        q_c = [z_ref[0, rows, GLA_Q0 + h * GLA_DK:GLA_Q0 + (h + 1) * GLA_DK] * (GLA_DK ** -0.5) for h in hr]
        k_c = [z_ref[0, rows, GLA_K0 + h * GLA_DK:GLA_K0 + (h + 1) * GLA_DK] for h in hr]
        v_c = [z_ref[0, rows, GLA_V0 + h * GLA_DV:GLA_V0 + (h + 1) * GLA_DV].astype(BF16) for h in hr]
        q_dec = [(q_c[h] * jnp.exp(cum[h])).astype(BF16) for h in hr]
        k_inv = [(k_c[h] * jnp.exp(-cum[h])).astype(BF16) for h in hr]
        k_dec = [(k_c[h] * jnp.exp(last[h] - cum[h])).astype(BF16) for h in hr]
        att = [jnp.where(tri, _dot_nt(q_dec[h], k_inv[h]), 0.0).astype(BF16) for h in hr]
        o = [_dot(att[h], v_c[h]) + _dot_nt(q_dec[h], states[h].astype(BF16)) for h in hr]
        states = [states[h] * jnp.exp(last[h]) + _dot_tn(v_c[h], k_dec[h]) for h in hr]
        for h in hr:
            on = o[h] * lax.rsqrt(jnp.mean(o[h] * o[h], axis=-1, keepdims=True) + NORM_EPS) * ng_ref[...]
            gate = z_ref[0, rows, GLA_G0 + h * GLA_DV:GLA_G0 + (h + 1) * GLA_DV]
            o_ref[0, rows, h * GLA_DV:(h + 1) * GLA_DV] = (on * (gate * _sigmoid(gate))).astype(o_ref.dtype)
    for h in hr:
        s_scr[h] = states[h]

    @pl.when(ti == n_t - 1)
    def _():
        for h in range(GLA_HEADS):
            st_ref[0, h] = s_scr[h].T


def gla_mixer(z_gla, s0, a2p, ab, ng, t_valid):
    b, t, _ = z_gla.shape
    tt = min(t, 512)
    n_t = t // tt
    return pl.pallas_call(
        functools.partial(_gla_kernel, tt=tt, t_valid=t_valid, n_t=n_t),
        grid=(b, n_t),
        in_specs=[pl.BlockSpec((1, tt, GLA_W), lambda i, j: (i, j, 0)),
                  pl.BlockSpec((1, GLA_HEADS, GLA_DK, GLA_DV), lambda i, j: (i, 0, 0, 0)),
                  pl.BlockSpec((GLA_A_W, GLA_QK_DIM), lambda i, j: (0, 0)),
                  pl.BlockSpec((1, GLA_QK_DIM), lambda i, j: (0, 0)),
                  pl.BlockSpec((1, GLA_DV), lambda i, j: (0, 0))],
        out_specs=[pl.BlockSpec((1, tt, GLA_V_DIM), lambda i, j: (i, j, 0)),
                   pl.BlockSpec((1, GLA_HEADS, GLA_DK, GLA_DV), lambda i, j: (i, 0, 0, 0))],
        out_shape=[jax.ShapeDtypeStruct((b, t, GLA_V_DIM), BF16),
                   jax.ShapeDtypeStruct((b, GLA_HEADS, GLA_DK, GLA_DV), F32)],
        scratch_shapes=[pltpu.VMEM((GLA_HEADS, GLA_DV, GLA_DK), F32)],
        compiler_params=_params(("parallel", "arbitrary")),
        name="gla_mixer",
    )(z_gla, s0, a2p, ab, ng)


RW_R0, RW_K0, RW_V0 = 0, RW_DIM, 2 * RW_DIM
RW_WL0 = 3 * RW_DIM
RW_AL0 = RW_WL0 + RW_LR_W
RW_GL0 = RW_AL0 + RW_LR_W


def _rw_prep_kernel(z_ref, sh_ref, mu_ref, w0_ref, w2_ref, a0_ref, a2_ref, g2_ref, kk_ref, ka_ref, rk_ref, seg_ref,
                    r_ref, dec_ref, k_ref, v_ref, nkk_ref, kka_ref, g_ref, bonus_ref):
    z = z_ref[...]
    xm = z + (sh_ref[...] - z) * mu_ref[...]
    r = xm[:, RW_R0:RW_R0 + RW_DIM]
    k = xm[:, RW_K0:RW_K0 + RW_DIM]
    v = xm[:, RW_V0:RW_V0 + RW_DIM]
    wl = xm[:, RW_WL0:RW_WL0 + RW_LR_W]
    al = xm[:, RW_AL0:RW_AL0 + RW_LR_W]
    gl = xm[:, RW_GL0:RW_GL0 + RW_LR_W]
    w = -_softplus(-(w0_ref[...] + _dot(jnp.tanh(wl).astype(BF16), w2_ref[...]))) - 0.5
    a = _sigmoid(a0_ref[...] + _dot(al.astype(BF16), a2_ref[...]))
    kk = k * kk_ref[...]
    norm = jnp.sqrt(_dot(kk * kk, seg_ref[...], HIGHEST))
    kk = kk / jnp.maximum(norm, 1e-12)
    k2 = k * (1.0 + (a - 1.0) * ka_ref[...])
    r_ref[...] = r
    dec_ref[...] = -jnp.exp(w)
    k_ref[...] = k2
    v_ref[...] = v
    nkk_ref[...] = -kk
    kka_ref[...] = kk * a
    g_ref[...] = _dot(_sigmoid(gl).astype(BF16), g2_ref[...])
    bonus_ref[...] = _dot(r * k2 * rk_ref[...], seg_ref[...], HIGHEST) * v


def rw_prep(z_rw, shifted, rp, tm=512):
    m = z_rw.shape[0]
    tm = min(tm, m)
    row = lambda w: pl.BlockSpec((tm, w), lambda i: (i, 0))
    const = lambda r, w: pl.BlockSpec((r, w), lambda i: (0, 0))
    out = jax.ShapeDtypeStruct((m, RW_DIM), F32)
    return pl.pallas_call(
        _rw_prep_kernel,
        grid=(m // tm,),
        in_specs=[row(RW_W), row(RW_W), const(1, RW_W), const(1, RW_DIM), const(RW_LR_W, RW_DIM), const(1, RW_DIM),
                  const(RW_LR_W, RW_DIM), const(RW_LR_W, RW_DIM), const(1, RW_DIM), const(1, RW_DIM), const(1, RW_DIM),
                  const(RW_DIM, RW_DIM)],
        out_specs=[row(RW_DIM)] * 8,
        out_shape=[out] * 8,
        compiler_params=_params(("parallel",)),
        name="rw_prep",
    )(z_rw, shifted, rp["mu"], rp["w0"], rp["w2"], rp["a0"], rp["a2"], rp["g2"], rp["kk"], rp["ka"], rp["rk"], rp["seg"])


RW_CHUNK = 64
RW_CHUNK_LOG2 = 6


def _rw_scan_kernel(r_ref, dec_ref, k_ref, v_ref, nkk_ref, kka_ref, g_ref, bonus_ref, lng_ref, lnb_ref, s0_ref,
                    o_ref, st_ref, s_scr, *, tt, n_t):
    ti = pl.program_id(1)

    @pl.when(ti == 0)
    def _():
        s_scr[...] = s0_ref[0]

    c = RW_CHUNK
    row_i = lax.broadcasted_iota(jnp.int32, (c, c), 0)
    col_i = lax.broadcasted_iota(jnp.int32, (c, c), 1)
    lower = row_i >= col_i
    strict = row_i > col_i
    lower_b = lower.astype(BF16)
    eye_f = (row_i == col_i).astype(F32)
    heads = [slice(h * RW_HD, (h + 1) * RW_HD) for h in range(RW_HEADS)]
    sp = _split2
    mm = functools.partial(_dot3, _dot)
    mm_nt = functools.partial(_dot3, _dot_nt)
    mm_tn = functools.partial(_dot3, _dot_tn)

    def chunk(i, carry):
        rows = pl.ds(pl.multiple_of(i * c, c), c)
        st = []
        for h, hs in enumerate(heads):
            lw = dec_ref[0, rows, hs]
            cum = sum(_dot(lower_b, part) for part in _split3(lw))
            g_inv = jnp.exp(-cum)
            d = {"g_last": jnp.exp(cum[c - 1:c, :]), "s0_f": s_scr[h],
                 "at_f": kka_ref[0, rows, hs] * g_inv, "kt_f": k_ref[0, rows, hs] * g_inv,
                 "bt": sp(nkk_ref[0, rows, hs] * jnp.exp(cum - lw)), "rt": sp(r_ref[0, rows, hs] * jnp.exp(cum)),
                 "v": sp(v_ref[0, rows, hs]),
                 "tail": (lng_ref[:, hs], lnb_ref[:, hs], bonus_ref[0, rows, hs], g_ref[0, rows, hs])}
            d["at"], d["kt"], d["s0"] = sp(d["at_f"]), sp(d["kt_f"]), sp(d["s0_f"])
            st.append(d)
        for d in st:
            d["ma"] = jnp.where(strict, mm_nt(d["bt"], d["at"]), 0.0)
            d["mk"] = sp(jnp.where(strict, mm_nt(d["bt"], d["kt"]), 0.0))
            d["na"] = sp(jnp.where(lower, mm_nt(d["rt"], d["at"]), 0.0))
            d["nk"] = sp(jnp.where(lower, mm_nt(d["rt"], d["kt"]), 0.0))
            d["inv"] = eye_f + d["ma"]
            d["power"] = sp(d["ma"])
            d["rhs"] = mm_nt(d["bt"], d["s0"]) + mm(d["mk"], d["v"])
        for _ in range(RW_CHUNK_LOG2 - 1):
            for d in st:
                d["power"] = sp(mm(d["power"], d["power"]))
                d["inv"] = d["inv"] + mm(sp(d["inv"]), d["power"])
        outs = []
        for d in st:
            u = sp(mm(sp(d["inv"]), sp(d["rhs"])))
            o = mm_nt(d["rt"], d["s0"]) + mm(d["na"], u) + mm(d["nk"], d["v"])
            s_new = (d["s0_f"] * d["g_last"] + mm_tn(u, sp(d["at_f"] * d["g_last"]))
                     + mm_tn(d["v"], sp(d["kt_f"] * d["g_last"])))
            ln_g, ln_b, bonus, gate = d["tail"]
            mean = jnp.mean(o, axis=-1, keepdims=True)
            cen = o - mean
            var = jnp.mean(cen * cen, axis=-1, keepdims=True)
            y = cen * lax.rsqrt(var + RW_GN_EPS) * ln_g + ln_b
            outs.append((s_new, (y + bonus) * gate))
        for h, hs in enumerate(heads):
            s_scr[h] = outs[h][0]
            o_ref[0, rows, hs] = outs[h][1]
        return carry

    lax.fori_loop(0, tt // c, chunk, 0)

    @pl.when(ti == n_t - 1)
    def _():
        st_ref[0] = s_scr[...]


def rw_scan(r, dec, k, v, nkk, kka, g, bonus, ln_g, ln_b, s0):
    b, t, _ = r.shape
    tt = min(t, 256)
    n_t = t // tt
    tile = pl.BlockSpec((1, tt, RW_DIM), lambda i, j: (i, j, 0))
    vec = pl.BlockSpec((1, RW_DIM), lambda i, j: (0, 0))
    st = pl.BlockSpec((1, RW_HEADS, RW_HD, RW_HD), lambda i, j: (i, 0, 0, 0))
    return pl.pallas_call(
        functools.partial(_rw_scan_kernel, tt=tt, n_t=n_t),
        grid=(b, n_t),
        in_specs=[tile] * 8 + [vec, vec, st],
        out_specs=[tile, st],
        out_shape=[jax.ShapeDtypeStruct((b, t, RW_DIM), F32), jax.ShapeDtypeStruct(s0.shape, F32)],
        scratch_shapes=[pltpu.VMEM((RW_HEADS, RW_HD, RW_HD), F32)],
        compiler_params=_params(("parallel", "arbitrary")),
        name="rw_scan",
    )(r, dec, k, v, nkk, kka, g, bonus, ln_g, ln_b, s0)


def _pad_last(w, width):
    return jnp.pad(w, [(0, 0)] * (w.ndim - 1) + [(0, width - w.shape[-1])])


def _pad_rows(w, rows):
    return jnp.pad(w, [(0, rows - w.shape[0])] + [(0, 0)] * (w.ndim - 1))


def _split_last(z, sizes):
    return jnp.split(z, [int(c) for c in np.cumsum(np.array(sizes))[:-1]], axis=-1)


def _rw_padded(cols):
    r, k, v, wl, al, gl = _split_last(cols, RW_SIZES)
    return jnp.concatenate([r, k, v, _pad_last(wl, RW_LR_W), _pad_last(al, RW_LR_W), _pad_last(gl, RW_LR_W)], axis=-1)


def _rw_unpadded(cols):
    return jnp.concatenate([cols[..., :RW_WL0], cols[..., RW_WL0:RW_WL0 + RW_DECAY_RANK],
                            cols[..., RW_AL0:RW_AL0 + RW_A_RANK], cols[..., RW_GL0:RW_GL0 + RW_G_RANK]], axis=-1)


def _layer_params(l, p):
    q, cmp_, sel, win, gate, gq, gk, gv, ga, gg, rw, mg = _split_last(p["w_in"][l], IN_SIZES)
    seg = np.kron(np.eye(RW_HEADS, dtype=np.float32), np.ones((RW_HD, RW_HD), np.float32))
    row = lambda v: v.reshape(1, -1)
    eye2 = jnp.eye(BLOCKS_PER_PAGE, dtype=F32)
    w1 = p["cmp_w1"][l].reshape(2, NSA_BLOCK, NSA_HD, NSA_CMP_HID)
    wd = jnp.einsum("ab,kldj->dalkbj", eye2, w1).reshape(NSA_HD, BLOCKS_PER_PAGE * NSA_BLOCK,
                                                          2 * BLOCKS_PER_PAGE * NSA_CMP_HID)
    w2bd = jnp.einsum("ab,kje->kajbe", eye2, p["cmp_w2"][l]).reshape(2, BLOCKS_PER_PAGE * NSA_CMP_HID,
                                                                      BLOCKS_PER_PAGE * NSA_HD)
    pos_dl = jnp.tile(jnp.swapaxes(p["cmp_pos"][l], 1, 2), (1, 1, BLOCKS_PER_PAGE))
    pos_t = jnp.swapaxes(jnp.repeat(pos_dl, NSA_KV_HEADS, axis=0), 0, 1)
    return {
        "norm1_g": p["norm1_g"][l], "norm2_g": p["norm2_g"][l],
        "w_nsa": jnp.concatenate([q, cmp_, sel, win, _pad_last(gate, GATE_W)], axis=1).astype(BF16),
        "w_gla": jnp.concatenate([gq, gk, gv, _pad_last(ga, GLA_A_W), gg], axis=1).astype(BF16),
        "w_rw": _rw_padded(rw).astype(BF16),
        "w_mg": mg.astype(BF16),
        "qk_g": p["nsa_qk_g"][l],
        "cw": {"pos2": jnp.concatenate([p["cmp_pos"][l]] * 2, axis=-1),
               "w1": p["cmp_w1"][l].reshape(2, NSA_BLOCK, NSA_HD, NSA_CMP_HID).astype(BF16),
               "w2": p["cmp_w2"][l].astype(BF16),
               "kg": row(p["nsa_qk_g"][l, 1]),
               "pos_t": pos_t, "wd": wd.astype(BF16), "w2bd": w2bd.astype(BF16)},
        "gla_a2": _pad_rows(p["gla_a2"][l], GLA_A_W).astype(BF16),
        "gla_ab": row(p["gla_a_b"][l]), "gla_ng": row(p["gla_norm_g"][l]),
        "rp": {"mu": row(_rw_padded(p["rw_mu"][l])), "w0": row(p["rw_w0"][l]),
               "w2": _pad_rows(p["rw_w2"][l], RW_LR_W).astype(BF16), "a0": row(p["rw_a0"][l]),
               "a2": _pad_rows(p["rw_a2"][l], RW_LR_W).astype(BF16),
               "g2": _pad_rows(p["rw_g2"][l], RW_LR_W).astype(BF16),
               "kk": row(p["rw_kk"][l]), "ka": row(p["rw_ka"][l]), "rk": row(p["rw_rk"][l]), "seg": jnp.asarray(seg)},
        "ln_g": row(p["rw_ln_g"][l]), "ln_b": row(p["rw_ln_b"][l]),
        "nsa_up": p["nsa_up"][l].astype(BF16), "gla_up": p["gla_up"][l].astype(BF16), "rw_up": p["rw_up"][l].astype(BF16),
        "w_out": p["w_out"][l].astype(BF16), "mlp_w1": p["mlp_w1"][l].astype(BF16), "mlp_w2": p["mlp_w2"][l].astype(BF16),
    }


def _project_in(x2, lp, tm):
    hn = rmsnorm_rows(x2, lp["norm1_g"], tm)
    return tuple(matmul(hn, lp[w], tm=tm) for w in ("w_nsa", "w_gla", "w_rw", "w_mg"))


def _project_out(x2, o_nsa, o_gla, o_rw, z_mg, lp, tm):
    merged = merge_branches(o_nsa, o_gla, o_rw, lp["nsa_up"], lp["gla_up"], lp["rw_up"], z_mg, tm=tm)
    x1 = matmul(merged, lp["w_out"], "residual", x2, tm=tm)
    h2 = rmsnorm_rows(x1, lp["norm2_g"], tm)
    u = matmul(h2, lp["mlp_w1"], "relu2", tm=tm)
    return matmul(u, lp["mlp_w2"], "residual", x1, tm=tm)


def _kv_rows(rows, b, t):
    return rows.reshape(b, t, 2, NSA_KV_HEADS, NSA_HD)


def _prompt_layer(x, lp):
    b, t, d = x.shape
    m = b * t
    tm = 512
    x2 = x.reshape(m, d)
    z_nsa, z_gla, z_rw, z_mg = _project_in(x2, lp, tm)
    qn, cmp_rows, sel_rows, win_rows, sel_b, win_b, gates = nsa_prep(z_nsa, lp["qk_g"])
    nblk = t // NSA_BLOCK
    k_cmp, v_cmp = compress_rows(cmp_rows, lp["cw"], nb=32)
    per_b = lambda a: a.reshape(b, t, a.shape[-1])
    o_nsa = nsa_prompt_attention(per_b(qn), per_b(gates), per_b(sel_b), per_b(win_b),
                                 k_cmp.reshape(b, nblk, NSA_KV_DIM), v_cmp.reshape(b, nblk, NSA_KV_DIM))
    o_gla, gla_st = gla_mixer(per_b(z_gla), jnp.zeros((b, GLA_HEADS, GLA_DK, GLA_DV), F32),
                              lp["gla_a2"], lp["gla_ab"], lp["gla_ng"], t_valid=t)
    z_rw3 = per_b(z_rw)
    shifted = jnp.concatenate([jnp.zeros((b, 1, RW_W), F32), z_rw3[:, :-1]], axis=1).reshape(m, RW_W)
    prep = rw_prep(z_rw, shifted, lp["rp"])
    o_rw, rw_st = rw_scan(*(per_b(a) for a in prep), lp["ln_g"], lp["ln_b"],
                          jnp.zeros((b, RW_HEADS, RW_HD, RW_HD), F32))
    y = _project_out(x2, o_nsa.reshape(m, -1), o_gla.reshape(m, -1), o_rw.reshape(m, -1), z_mg, lp, tm)
    w_keep = min(NSA_WINDOW, t)
    return (y.reshape(b, t, d), _kv_rows(cmp_rows, b, t), _kv_rows(sel_rows, b, t), _kv_rows(win_rows, b, t)[:, t - w_keep:],
            gla_st, rw_st, _rw_unpadded(z_rw3[:, -1]))


SAMPLE_ROWS = SUBLANES_BF16


def _sample_layer(x2, lp, layer, caches, page_table, nbatch):
    cache_cmp, cache_sel, cache_win, state_gla, state_rwkv, state_shift = caches
    tm = SAMPLE_ROWS
    past = page_table.shape[1] * PAGE_SIZE
    nblk = -(-(past + 1) // NSA_BLOCK)
    z_nsa, z_gla, z_rw, z_mg = _project_in(x2, lp, tm)
    qn, cmp_rows, sel_rows, win_rows, _, _, gates = nsa_prep(z_nsa, lp["qk_g"])
    kc_past, vc_past = compress_pages(cache_cmp, layer, page_table, lp["cw"])
    last_blk = jnp.pad(cmp_rows[:nbatch, None, :], ((0, 0), (0, NSA_BLOCK - 1), (0, 0))).reshape(nbatch * NSA_BLOCK, -1)
    kc_last, vc_last = compress_rows(last_blk, lp["cw"], nb=nbatch)
    nblk_pad = -(-nblk // 8) * 8
    with_last = lambda past_s, last_s: jnp.pad(jnp.concatenate([past_s, last_s[:, None, :]], axis=1),
                                               ((0, 0), (0, nblk_pad - nblk), (0, 0)))
    q16 = qn[:nbatch].astype(F32).reshape(nbatch, NSA_HEADS, NSA_HD)
    o_cmp, o_win, idx = nsa_sample_attention1(
        q16, with_last(kc_past, kc_last), with_last(vc_past, vc_last),
        cache_win.reshape(cache_win.shape[0], nbatch, cache_win.shape[2], 2 * NSA_KV_DIM), layer,
        win_rows[:nbatch, None, :], nblk, past)
    grp8 = lambda a: jnp.pad(a.reshape(nbatch, NSA_KV_HEADS, NSA_GROUP, a.shape[-1]),
                             ((0, 0), (0, 0), (0, 8 - NSA_GROUP), (0, 0)))
    gates3 = gates[:nbatch, :3 * NSA_HEADS].reshape(nbatch, 3, NSA_HEADS).transpose(0, 2, 1)
    o8 = nsa_sample_attention2(idx.reshape(-1), page_table, grp8(q16), cache_sel, layer, sel_rows[:nbatch, None, :],
                               grp8(o_cmp), grp8(o_win), grp8(_pad_last(gates3, LANES)), past)
    o_nsa = _pad_rows(o8[:, :, :NSA_GROUP].reshape(nbatch, NSA_Q_DIM), tm)
    z_gla_c = jnp.pad(z_gla[:nbatch, None, :], ((0, 0), (0, GLA_CHUNK - 1), (0, 0)))
    o_gla, gla_st = gla_mixer(z_gla_c, state_gla[layer], lp["gla_a2"], lp["gla_ab"], lp["gla_ng"], t_valid=1)
    o_gla = _pad_rows(o_gla[:, 0], tm)
    shifted = _pad_rows(_rw_padded(state_shift[layer]), tm)
    prep = rw_prep(z_rw, shifted, lp["rp"])
    prep = [jnp.pad(a[:nbatch, None, :], ((0, 0), (0, RW_CHUNK - 1), (0, 0))) for a in prep]
    o_rw, rw_st = rw_scan(*prep, lp["ln_g"], lp["ln_b"], state_rwkv[layer])
    o_rw = _pad_rows(o_rw[:, 0], tm)
    y = _project_out(x2, o_nsa, o_gla, o_rw, z_mg, lp, tm)
    return (y, _kv_rows(cmp_rows[:nbatch], nbatch, 1), _kv_rows(sel_rows[:nbatch], nbatch, 1),
            _kv_rows(win_rows[:nbatch], nbatch, 1), gla_st, rw_st, _rw_unpadded(z_rw[:nbatch]))


def kernel(x_prompt, x_sample, cache_cmp_kv, cache_sel_kv, cache_win_kv, state_gla, state_rwkv, state_rwkv_shift,
           page_table, norm1_g, w_in, nsa_qk_g, cmp_pos, cmp_w1, cmp_w2, nsa_up, gla_a2, gla_a_b, gla_norm_g, gla_up,
           rw_mu, rw_w0, rw_w2, rw_a0, rw_a2, rw_g2, rw_kk, rw_ka, rw_rk, rw_ln_g, rw_ln_b, rw_up, w_out, norm2_g,
           mlp_w1, mlp_w2):
    p = dict(norm1_g=norm1_g, w_in=w_in, nsa_qk_g=nsa_qk_g, cmp_pos=cmp_pos, cmp_w1=cmp_w1, cmp_w2=cmp_w2, nsa_up=nsa_up,
             gla_a2=gla_a2, gla_a_b=gla_a_b, gla_norm_g=gla_norm_g, gla_up=gla_up, rw_mu=rw_mu, rw_w0=rw_w0, rw_w2=rw_w2,
             rw_a0=rw_a0, rw_a2=rw_a2, rw_g2=rw_g2, rw_kk=rw_kk, rw_ka=rw_ka, rw_rk=rw_rk, rw_ln_g=rw_ln_g,
             rw_ln_b=rw_ln_b, rw_up=rw_up, w_out=w_out, norm2_g=norm2_g, mlp_w1=mlp_w1, mlp_w2=mlp_w2)
    nbatch = x_sample.shape[0]
    caches = (cache_cmp_kv, cache_sel_kv, cache_win_kv, state_gla, state_rwkv, state_rwkv_shift)
    xp = x_prompt
    xs = _pad_rows(x_sample.reshape(nbatch, D_MODEL), SAMPLE_ROWS)
    outs_p, outs_s = [], []
    for l in range(DEPTH):
        lp = _layer_params(l, p)
        xp, *rest_p = _prompt_layer(xp, lp)
        xs, *rest_s = _sample_layer(xs, lp, l, caches, page_table, nbatch)
        outs_p.append(rest_p)
        outs_s.append(rest_s)
    stack = lambda outs, i: jnp.stack([o[i] for o in outs])
    return ((xp, xs[:nbatch].reshape(x_sample.shape))
            + tuple(stack(outs_p, i) for i in range(6)) + tuple(stack(outs_s, i) for i in range(6)))
```

```python
import functools

import numpy as np
import jax
import jax.numpy as jnp
from jax import lax
from jax.experimental import pallas as pl
from jax.experimental.pallas import tpu as pltpu

F32 = jnp.float32
BF16 = jnp.bfloat16
HIGHEST = lax.Precision.HIGHEST

D_MODEL = 2048
DEPTH = 2
PAGE_SIZE = 128
NSA_HEADS = 16
NSA_KV_HEADS = 4
NSA_GROUP = NSA_HEADS // NSA_KV_HEADS
NSA_HD = 64
NSA_BLOCK = 64
NSA_TOPK = 16
NSA_WINDOW = 512
NSA_CMP_HID = 128
GLA_HEADS = 4
GLA_DK = 64
GLA_DV = 128
GLA_GATE_RANK = 16
GLA_TAU = 16.0
GLA_CHUNK = 64
RW_HEADS = 8
RW_HD = 64
RW_DECAY_RANK = 32
RW_A_RANK = 32
RW_G_RANK = 96
RW_GN_EPS = 64e-5
D_FF = 4 * D_MODEL
NORM_EPS = 1e-6

NSA_Q_DIM = NSA_HEADS * NSA_HD
NSA_KV_DIM = NSA_KV_HEADS * NSA_HD
GLA_QK_DIM = GLA_HEADS * GLA_DK
GLA_V_DIM = GLA_HEADS * GLA_DV
RW_DIM = RW_HEADS * RW_HD
RW_SIZES = (RW_DIM, RW_DIM, RW_DIM, RW_DECAY_RANK, RW_A_RANK, RW_G_RANK)
RW_PROJ = sum(RW_SIZES)
IN_SIZES = (NSA_Q_DIM, 2 * NSA_KV_DIM, 2 * NSA_KV_DIM, 2 * NSA_KV_DIM, 3 * NSA_HEADS,
            GLA_QK_DIM, GLA_QK_DIM, GLA_V_DIM, GLA_GATE_RANK, GLA_V_DIM, RW_PROJ, 3 * D_MODEL)

LANES = 128
SUBLANES_BF16 = 16
VMEM_LIMIT = 56 * 1024 * 1024

GATE_W = LANES
NSA_W = NSA_Q_DIM + 6 * NSA_KV_DIM + GATE_W
GLA_A_W = LANES
GLA_W = 2 * GLA_QK_DIM + 2 * GLA_V_DIM + GLA_A_W
RW_LR_W = LANES
RW_W = 3 * RW_DIM + 3 * RW_LR_W
MERGE_W = 3 * D_MODEL


def _params(sem):
    return pltpu.CompilerParams(dimension_semantics=sem, vmem_limit_bytes=VMEM_LIMIT)


def _largest_tile(n, cap):
    best = None
    for t in range(LANES, min(n, cap) + 1, LANES):
        if n % t == 0:
            best = t
    return best if best is not None else n


def _sigmoid(x):
    return 1.0 / (1.0 + jnp.exp(-x))


def _softplus(y):
    return jnp.maximum(y, 0.0) + jnp.log(1.0 + jnp.exp(-jnp.abs(y)))


def _masked_softmax(s, mask):
    s = jnp.where(mask, s, -jnp.inf)
    m = jnp.max(s, axis=-1, keepdims=True)
    m = jnp.where(m > -jnp.inf, m, 0.0)
    e = jnp.where(mask, jnp.exp(s - m), 0.0)
    return e / jnp.maximum(jnp.sum(e, axis=-1, keepdims=True), 1e-30)


def _masked_softmax_rows(s, mask):
    s = jnp.where(mask, s, -jnp.inf)
    m = jnp.max(s, axis=0, keepdims=True)
    m = jnp.where(m > -jnp.inf, m, 0.0)
    e = jnp.where(mask, jnp.exp(s - m), 0.0)
    return e / jnp.maximum(jnp.sum(e, axis=0, keepdims=True), 1e-30)


def _dot(a, b, precision=None):
    return jnp.dot(a, b, preferred_element_type=F32, precision=precision)


def _dot_nt(a, b, precision=None):
    return lax.dot_general(a, b, (((1,), (1,)), ((), ())), preferred_element_type=F32, precision=precision)


def _dot_tn(a, b, precision=None):
    return lax.dot_general(a, b, (((0,), (0,)), ((), ())), preferred_element_type=F32, precision=precision)


def _split2(x):
    hi = x.astype(BF16)
    return hi, (x - hi.astype(F32)).astype(BF16)


def _split3(x):
    hi = x.astype(BF16)
    rest = x - hi.astype(F32)
    mid = rest.astype(BF16)
    return hi, mid, (rest - mid.astype(F32)).astype(BF16)


def _dot3(dot, a, b):
    (a_hi, a_lo), (b_hi, b_lo) = a, b
    return dot(a_hi, b_hi) + dot(a_hi, b_lo) + dot(a_lo, b_hi)


def _rmsnorm_kernel(x_ref, g_ref, o_ref):
    x = x_ref[...]
    y = x * lax.rsqrt(jnp.mean(x * x, axis=-1, keepdims=True) + NORM_EPS) * g_ref[...]
    o_ref[...] = y.astype(o_ref.dtype)


def rmsnorm_rows(x, g, tm):
    m, d = x.shape
    return pl.pallas_call(
        _rmsnorm_kernel,
        grid=(m // tm,),
        in_specs=[pl.BlockSpec((tm, d), lambda i: (i, 0)), pl.BlockSpec((1, d), lambda i: (0, 0))],
        out_specs=pl.BlockSpec((tm, d), lambda i: (i, 0)),
        out_shape=jax.ShapeDtypeStruct((m, d), BF16),
        compiler_params=_params(("parallel",)),
        name="rmsnorm_rows",
    )(x, g.reshape(1, d))


def _mm_kernel(*refs, nk, mode):
    if mode == "residual":
        a_ref, w_ref, r_ref, o_ref = refs[:4]
        scratch = refs[4:]
    else:
        a_ref, w_ref, o_ref = refs[:3]
        scratch = refs[3:]
    part = _dot(a_ref[...], w_ref[...])

    def finish(acc):
        if mode == "relu2":
            acc = jnp.square(jnp.maximum(acc, 0.0))
        elif mode == "residual":
            acc = r_ref[...] + acc
        o_ref[...] = acc.astype(o_ref.dtype)

    if nk == 1:
        finish(part)
    else:
        acc_ref = scratch[0]
        k = pl.program_id(2)

        @pl.when(k == 0)
        def _():
            acc_ref[...] = part

        @pl.when(k > 0)
        def _():
            acc_ref[...] += part

        @pl.when(k == nk - 1)
        def _():
            finish(acc_ref[...])


def matmul(a, w, mode="plain", residual=None, tm=512):
    m, kdim = a.shape
    n = w.shape[1]
    tm = min(tm, m)
    tk = min(kdim, 2048)
    nk = kdim // tk
    tn = _largest_tile(n, 1024 if (mode == "residual" or nk > 1) else 2048)
    in_specs = [pl.BlockSpec((tm, tk), lambda i, j, k: (i, k)), pl.BlockSpec((tk, tn), lambda i, j, k: (k, j))]
    args = [a, w]
    if mode == "residual":
        in_specs.append(pl.BlockSpec((tm, tn), lambda i, j, k: (i, j)))
        args.append(residual)
    out_dtype = BF16 if mode == "relu2" else F32
    return pl.pallas_call(
        functools.partial(_mm_kernel, nk=nk, mode=mode),
        grid=(m // tm, n // tn, nk),
        in_specs=in_specs,
        out_specs=pl.BlockSpec((tm, tn), lambda i, j, k: (i, j)),
        out_shape=jax.ShapeDtypeStruct((m, n), out_dtype),
        scratch_shapes=[pltpu.VMEM((tm, tn), F32)] if nk > 1 else [],
        compiler_params=_params(("parallel", "parallel", "arbitrary")),
        name="matmul_" + mode,
    )(*args)


def _merge_kernel(on_ref, og_ref, or_ref, wn_ref, wg_ref, wr_ref, m0_ref, m1_ref, m2_ref, o_ref):
    acc = _sigmoid(m0_ref[...]) * _dot(on_ref[...].astype(BF16), wn_ref[...])
    acc += _sigmoid(m1_ref[...]) * _dot(og_ref[...].astype(BF16), wg_ref[...])
    acc += _sigmoid(m2_ref[...]) * _dot(or_ref[...].astype(BF16), wr_ref[...])
    o_ref[...] = acc.astype(o_ref.dtype)


def merge_branches(o_nsa, o_gla, o_rw, w_nsa, w_gla, w_rw, z_merge, tm=512, tn=1024):
    m = o_nsa.shape[0]
    tm = min(tm, m)
    nj = D_MODEL // tn
    row = lambda kd: pl.BlockSpec((tm, kd), lambda i, j: (i, 0))
    wsp = lambda kd: pl.BlockSpec((kd, tn), lambda i, j: (0, j))
    gate = lambda b: pl.BlockSpec((tm, tn), lambda i, j: (i, j + b * nj))
    return pl.pallas_call(
        _merge_kernel,
        grid=(m // tm, nj),
        in_specs=[row(NSA_Q_DIM), row(GLA_V_DIM), row(RW_DIM), wsp(NSA_Q_DIM), wsp(GLA_V_DIM), wsp(RW_DIM),
                  gate(0), gate(1), gate(2)],
        out_specs=pl.BlockSpec((tm, tn), lambda i, j: (i, j)),
        out_shape=jax.ShapeDtypeStruct((m, D_MODEL), BF16),
        compiler_params=_params(("parallel", "parallel")),
        name="merge_branches",
    )(o_nsa, o_gla, o_rw, w_nsa, w_gla, w_rw, z_merge, z_merge, z_merge)


def _head_rmsnorm(x, g):
    return x * lax.rsqrt(jnp.mean(x * x, axis=-1, keepdims=True) + NORM_EPS) * g


def _nsa_prep_kernel(z_ref, g_ref, qn_ref, cmp_ref, sel_ref, win_ref, selb_ref, winb_ref, gates_ref):
    hd = NSA_HD
    g_q, g_sel, g_win = g_ref[0:1, :], g_ref[2:3, :], g_ref[3:4, :]
    scale = NSA_HD ** -0.5
    for h in range(NSA_HEADS):
        q = _head_rmsnorm(z_ref[:, h * hd:(h + 1) * hd], g_q)
        qn_ref[:, h * hd:(h + 1) * hd] = (q * scale).astype(BF16)
    c0 = NSA_Q_DIM
    cmp_ref[...] = z_ref[:, c0:c0 + 2 * NSA_KV_DIM]
    for (off, gain, o_ref, ob_ref) in ((c0 + 2 * NSA_KV_DIM, g_sel, sel_ref, selb_ref),
                                       (c0 + 4 * NSA_KV_DIM, g_win, win_ref, winb_ref)):
        for h in range(NSA_KV_HEADS):
            k = _head_rmsnorm(z_ref[:, off + h * hd:off + (h + 1) * hd], gain)
            o_ref[:, h * hd:(h + 1) * hd] = k
            ob_ref[:, h * hd:(h + 1) * hd] = k.astype(BF16)
        v = z_ref[:, off + NSA_KV_DIM:off + 2 * NSA_KV_DIM]
        o_ref[:, NSA_KV_DIM:] = v
        ob_ref[:, NSA_KV_DIM:] = v.astype(BF16)
    gates_ref[...] = _sigmoid(z_ref[:, c0 + 6 * NSA_KV_DIM:])


def nsa_prep(z_nsa, qk_g, tm=256):
    m = z_nsa.shape[0]
    tm = min(tm, m)
    kvw = 2 * NSA_KV_DIM
    row = lambda w: pl.BlockSpec((tm, w), lambda i: (i, 0))
    return pl.pallas_call(
        _nsa_prep_kernel,
        grid=(m // tm,),
        in_specs=[row(NSA_W), pl.BlockSpec((4, NSA_HD), lambda i: (0, 0))],
        out_specs=[row(NSA_Q_DIM), row(kvw), row(kvw), row(kvw), row(kvw), row(kvw), row(GATE_W)],
        out_shape=[jax.ShapeDtypeStruct((m, NSA_Q_DIM), BF16),
                   jax.ShapeDtypeStruct((m, kvw), F32), jax.ShapeDtypeStruct((m, kvw), F32),
                   jax.ShapeDtypeStruct((m, kvw), F32), jax.ShapeDtypeStruct((m, kvw), BF16),
                   jax.ShapeDtypeStruct((m, kvw), BF16), jax.ShapeDtypeStruct((m, GATE_W), F32)],
        compiler_params=_params(("parallel",)),
        name="nsa_prep",
    )(z_nsa, qk_g)


QUARTERS = 2 * NSA_KV_DIM // LANES


def _compress_compute(x_ref, nb, pos_ref, w1_ref, w2_ref, kg_ref, kc_ref, vc_ref, acc_ref):
    acc_ref[...] = jnp.zeros_like(acc_ref)
    row_stride = NSA_BLOCK * QUARTERS

    def body(l, carry):
        for q in range(QUARTERS):
            kv = q // 2
            x = x_ref[pl.ds(l * QUARTERS + q, nb, stride=row_stride), :] + pos_ref[kv, pl.ds(l, 1), :]
            for half in range(2):
                acc_ref[q * 2 + half] += _dot(x[:, half * NSA_HD:(half + 1) * NSA_HD].astype(BF16), w1_ref[kv, l])
        return carry

    lax.fori_loop(0, NSA_BLOCK, body, 0)
    for q in range(QUARTERS):
        kv = q // 2
        for half in range(2):
            g = (q % 2) * 2 + half
            hid = jax.nn.gelu(acc_ref[q * 2 + half], approximate=True)
            s = _dot(hid.astype(BF16), w2_ref[kv])
            if kv == 0:
                kc_ref[:, g * NSA_HD:(g + 1) * NSA_HD] = _head_rmsnorm(s, kg_ref[...])
            else:
                vc_ref[:, g * NSA_HD:(g + 1) * NSA_HD] = s


def _compress_kernel(x_ref, pos_ref, w1_ref, w2_ref, kg_ref, kc_ref, vc_ref, acc_ref, *, nb):
    _compress_compute(x_ref, nb, pos_ref, w1_ref, w2_ref, kg_ref, kc_ref, vc_ref, acc_ref)


def _cmp_weight_specs(nidx):
    zero = lambda n: (lambda *idx: (0,) * n)
    return [pl.BlockSpec((2, NSA_BLOCK, LANES), zero(3)),
            pl.BlockSpec((2, NSA_BLOCK, NSA_HD, NSA_CMP_HID), zero(4)),
            pl.BlockSpec((2, NSA_CMP_HID, NSA_HD), zero(3)),
            pl.BlockSpec((1, NSA_HD), zero(2))]


def compress_rows(rows, cw, nb):
    nblk = rows.shape[0] // NSA_BLOCK
    nb = min(nb, nblk)
    x4 = rows.reshape(nblk * NSA_BLOCK * QUARTERS, LANES)
    out = jax.ShapeDtypeStruct((nblk, NSA_KV_DIM), F32)
    return pl.pallas_call(
        functools.partial(_compress_kernel, nb=nb),
        grid=(nblk // nb,),
        in_specs=[pl.BlockSpec((nb * NSA_BLOCK * QUARTERS, LANES), lambda i: (i, 0))] + _cmp_weight_specs(1),
        out_specs=[pl.BlockSpec((nb, NSA_KV_DIM), lambda i: (i, 0))] * 2,
        out_shape=[out, out],
        scratch_shapes=[pltpu.VMEM((2 * QUARTERS, nb, NSA_CMP_HID), F32)],
        compiler_params=_params(("parallel",)),
        name="compress_rows",
    )(x4, cw["pos2"], cw["w1"], cw["w2"], cw["kg"])


PAGES_PER_STEP = 32
BLOCKS_PER_PAGE = PAGE_SIZE // NSA_BLOCK


SLABS_PER_PAGE = 2 * NSA_KV_HEADS
PAGE_T_ROWS = SLABS_PER_PAGE * NSA_HD
CMP_FEATURES_PER_DOT = 8
assert PAGE_SIZE == LANES and BLOCKS_PER_PAGE == 2


def _compress_pages_kernel(pt_ref, cache_ref, pos_ref, wd_ref, w2_ref, kg_ref, o_ref, buf, sem, acc_ref, *,
                           layer, npg, n_c, n_steps):
    b, c = pl.program_id(0), pl.program_id(1)
    step = b * n_c + c
    slot = step % 2
    nslab = npg * SLABS_PER_PAGE

    def page_copy(bb, cc, sl, p):
        dst = buf.at[pl.ds(pl.multiple_of((sl * npg + p) * PAGE_T_ROWS, PAGE_T_ROWS), PAGE_T_ROWS), :]
        return pltpu.make_async_copy(cache_ref.at[layer, pt_ref[bb, cc * npg + p]], dst, sem.at[sl])

    @pl.when(step == 0)
    def _():
        for p in range(npg):
            page_copy(0, 0, 0, p).start()

    @pl.when(step + 1 < n_steps)
    def _():
        nxt = step + 1
        for p in range(npg):
            page_copy(nxt // n_c, nxt % n_c, 1 - slot, p).start()

    for p in range(npg):
        page_copy(b, c, slot, p).wait()

    base = slot * (npg * PAGE_T_ROWS)
    acc_ref[...] = jnp.zeros_like(acc_ref)

    def features(i, carry):
        xs = []
        for u in range(CMP_FEATURES_PER_DOT):
            d = i * CMP_FEATURES_PER_DOT + u
            x = buf[pl.ds(base + d, nslab, stride=NSA_HD), :]
            x = (x.reshape(npg, SLABS_PER_PAGE, LANES) + pos_ref[d][None]).reshape(nslab, LANES)
            xs.append(x.astype(BF16))
        k0 = pl.multiple_of(i * (CMP_FEATURES_PER_DOT * LANES), CMP_FEATURES_PER_DOT * LANES)
        acc_ref[...] += _dot(jnp.concatenate(xs, axis=1), wd_ref[pl.ds(k0, CMP_FEATURES_PER_DOT * LANES), :])
        return carry

    lax.fori_loop(0, NSA_HD // CMP_FEATURES_PER_DOT, features, 0)
    is_k = (lax.broadcasted_iota(jnp.int32, (nslab, 1), 0) % SLABS_PER_PAGE) < NSA_KV_HEADS
    half = BLOCKS_PER_PAGE * NSA_CMP_HID
    hid = jax.nn.gelu(jnp.where(is_k, acc_ref[:, :half], acc_ref[:, half:]), approximate=True).astype(BF16)
    k_out = _dot(hid, w2_ref[0])
    v_out = _dot(hid, w2_ref[1])
    k_out = jnp.concatenate([_head_rmsnorm(k_out[:, :NSA_HD], kg_ref[...]), _head_rmsnorm(k_out[:, NSA_HD:], kg_ref[...])],
                            axis=1)
    o_ref[0] = jnp.where(is_k, k_out, v_out)


def compress_pages(cache, layer, page_table, cw):
    nbatch, n_pages = page_table.shape
    npg = min(PAGES_PER_STEP, n_pages)
    n_c = n_pages // npg
    nslab = npg * SLABS_PER_PAGE
    cache_t = jnp.transpose(cache, (0, 1, 3, 4, 5, 2)).reshape(cache.shape[0], cache.shape[1], PAGE_T_ROWS, PAGE_SIZE)
    const = lambda shape: pl.BlockSpec(shape, lambda b, c, pt: (0,) * len(shape))
    grid_spec = pltpu.PrefetchScalarGridSpec(
        num_scalar_prefetch=1,
        grid=(nbatch, n_c),
        in_specs=[pl.BlockSpec(memory_space=pl.ANY), const((NSA_HD, SLABS_PER_PAGE, LANES)),
                  const((NSA_HD * LANES, 2 * BLOCKS_PER_PAGE * NSA_CMP_HID)),
                  const((2, BLOCKS_PER_PAGE * NSA_CMP_HID, BLOCKS_PER_PAGE * NSA_HD)), const((1, NSA_HD))],
        out_specs=pl.BlockSpec((1, nslab, BLOCKS_PER_PAGE * NSA_HD), lambda b, c, pt: (b, c, 0)),
        scratch_shapes=[pltpu.VMEM((2 * npg * PAGE_T_ROWS, PAGE_SIZE), F32), pltpu.SemaphoreType.DMA((2,)),
                        pltpu.VMEM((nslab, 2 * BLOCKS_PER_PAGE * NSA_CMP_HID), F32)],
    )
    out = pl.pallas_call(
        functools.partial(_compress_pages_kernel, layer=layer, npg=npg, n_c=n_c, n_steps=nbatch * n_c),
        grid_spec=grid_spec,
        out_shape=jax.ShapeDtypeStruct((nbatch, n_pages * SLABS_PER_PAGE, BLOCKS_PER_PAGE * NSA_HD), F32),
        compiler_params=_params(("arbitrary", "arbitrary")),
        name="compress_pages",
    )(page_table, cache_t, cw["pos_t"], cw["wd"], cw["w2bd"], cw["kg"])
    out = out.reshape(nbatch, n_pages, 2, NSA_KV_HEADS, BLOCKS_PER_PAGE, NSA_HD).transpose(2, 0, 1, 4, 3, 5)
    out = out.reshape(2, nbatch, n_pages * BLOCKS_PER_PAGE, NSA_KV_DIM)
    return out[0], out[1]


def _select_blocks(score, blk, cur, nblk):
    forced = (blk == 0) | (blk == cur) | (blk == cur - 1)
    score = jnp.where(forced, jnp.inf, score)
    score = jnp.where(blk <= cur, score, -jnp.inf)
    rank = jnp.zeros(score.shape, F32)
    for i in range(nblk):
        row = score[i:i + 1, :]
        ahead = (row > score) | ((row == score) & (blk > i))
        rank = rank + jnp.where(ahead, 1.0, 0.0)
    return (rank < float(min(NSA_TOPK, nblk))) & (blk <= cur)


def _alibi_slope(head):
    return 2.0 ** (-8.0 * (head + 1) / NSA_HEADS)


FAR = 1e38
M_INIT = -1e30


def _nsa_prompt_kernel(q_ref, gates_ref, selb_ref, winb_ref, kc_ref, vc_ref, o_ref, *, tq, t_len, band, kchunk):
    qi = pl.program_id(1)
    q0 = qi * tq
    nblk = t_len // NSA_BLOCK
    hd = NSA_HD
    q_pos = q0 + lax.broadcasted_iota(jnp.int32, (tq, 1), 0)
    q_pos_row = q0 + lax.broadcasted_iota(jnp.int32, (1, tq), 1)
    cur = q_pos_row // NSA_BLOCK
    blk = lax.broadcasted_iota(jnp.int32, (nblk, 1), 0)
    d_cmp = q_pos_row - ((blk + 1) * NSA_BLOCK - 1)
    m_cmp = d_cmp >= 0
    d_cmp_f = d_cmp.astype(F32)
    w0 = pl.multiple_of(jnp.maximum(q0 - NSA_WINDOW, 0), tq)
    d_win = q_pos - (w0 + lax.broadcasted_iota(jnp.int32, (1, band), 1))
    d_win_m = jnp.where((d_win >= 0) & (d_win < NSA_WINDOW), d_win.astype(F32), FAR)
    blk_row = lax.broadcasted_iota(jnp.int32, (nblk, kchunk), 0)
    key_col = lax.broadcasted_iota(jnp.int32, (nblk, kchunk), 1)
    key_lane = lax.broadcasted_iota(jnp.int32, (1, kchunk), 1)
    n_chunks = (q0 + tq + kchunk - 1) // kchunk
    for g in range(NSA_KV_HEADS):
        kc = kc_ref[0, :, g * hd:(g + 1) * hd].astype(BF16)
        vc = vc_ref[0, :, g * hd:(g + 1) * hd].astype(BF16)
        k_win = winb_ref[0, pl.ds(w0, band), g * hd:(g + 1) * hd]
        v_win = winb_ref[0, pl.ds(w0, band), NSA_KV_DIM + g * hd:NSA_KV_DIM + (g + 1) * hd]
        heads = [g * NSA_GROUP + r for r in range(NSA_GROUP)]
        group = range(NSA_GROUP)
        qs = [q_ref[0, :, h * hd:(h + 1) * hd] for h in heads]
        slopes = [_alibi_slope(h) for h in heads]
        p_cmp = [_masked_softmax_rows(_dot_nt(kc, qs[r]) - slopes[r] * d_cmp_f, m_cmp) for r in group]
        o_cmp = [_dot_tn(p_cmp[r].astype(BF16), vc) for r in group]
        sel_f = jnp.where(_select_blocks(sum(p_cmp), blk, cur, nblk), 1.0, 0.0).astype(BF16)

        def sel_chunk(ci, carry):
            k0 = pl.multiple_of(ci * kchunk, kchunk)
            k_c = selb_ref[0, pl.ds(k0, kchunk), g * hd:(g + 1) * hd]
            v_c = selb_ref[0, pl.ds(k0, kchunk), NSA_KV_DIM + g * hd:NSA_KV_DIM + (g + 1) * hd]
            expand = ((k0 + key_col) // NSA_BLOCK == blk_row).astype(BF16)
            d = q_pos - (k0 + key_lane)
            d_m = jnp.where((_dot_tn(sel_f, expand) > 0.5) & (d >= 0), d.astype(F32), FAR)
            s = [_dot_nt(qs[r], k_c) - slopes[r] * d_m for r in group]
            m_new = [jnp.maximum(carry[r][0], jnp.max(s[r], axis=-1, keepdims=True)) for r in group]
            p = [jnp.exp(s[r] - m_new[r]) for r in group]
            alpha = [jnp.exp(carry[r][0] - m_new[r]) for r in group]
            pv = [_dot(p[r].astype(BF16), v_c) for r in group]
            return tuple((m_new[r], alpha[r] * carry[r][1] + jnp.sum(p[r], axis=-1, keepdims=True),
                          alpha[r] * carry[r][2] + pv[r]) for r in group)

        init = tuple((jnp.full((tq, 1), M_INIT, F32), jnp.zeros((tq, 1), F32), jnp.zeros((tq, hd), F32))
                     for _ in group)
        sel_state = lax.fori_loop(0, n_chunks, sel_chunk, init)
        s = [_dot_nt(qs[r], k_win) - slopes[r] * d_win_m for r in group]
        p = [jnp.exp(s[r] - jnp.maximum(jnp.max(s[r], axis=-1, keepdims=True), M_INIT)) for r in group]
        pv = [_dot(p[r].astype(BF16), v_win) for r in group]
        for r, h in enumerate(heads):
            o_sel = sel_state[r][2] / jnp.maximum(sel_state[r][1], 1e-30)
            o_win = pv[r] / jnp.maximum(jnp.sum(p[r], axis=-1, keepdims=True), 1e-30)
            gate = lambda b: gates_ref[0, :, b * NSA_HEADS + h:b * NSA_HEADS + h + 1]
            o = gate(0) * o_cmp[r] + gate(1) * o_sel + gate(2) * o_win
            o_ref[0, :, h * hd:(h + 1) * hd] = o.astype(o_ref.dtype)


def nsa_prompt_attention(qn, gates, sel_b, win_b, k_cmp, v_cmp, tq=128):
    b, t, _ = qn.shape
    tq = min(tq, t)
    band = min(NSA_WINDOW + tq, t)
    nblk = t // NSA_BLOCK
    full = lambda w, n: pl.BlockSpec((1, n, w), lambda i, j: (i, 0, 0))
    tile = lambda w: pl.BlockSpec((1, tq, w), lambda i, j: (i, j, 0))
    return pl.pallas_call(
        functools.partial(_nsa_prompt_kernel, tq=tq, t_len=t, band=band, kchunk=min(512, t)),
        grid=(b, t // tq),
        in_specs=[tile(NSA_Q_DIM), tile(GATE_W), full(2 * NSA_KV_DIM, t), full(2 * NSA_KV_DIM, t),
                  full(NSA_KV_DIM, nblk), full(NSA_KV_DIM, nblk)],
        out_specs=tile(NSA_Q_DIM),
        out_shape=jax.ShapeDtypeStruct((b, t, NSA_Q_DIM), BF16),
        compiler_params=_params(("parallel", "arbitrary")),
        name="nsa_prompt_attention",
    )(qn, gates, sel_b, win_b, k_cmp, v_cmp)


def _head_group_mask(nrows):
    row = lax.broadcasted_iota(jnp.int32, (nrows, 1), 0)
    return [(row // NSA_GROUP) == g for g in range(NSA_KV_HEADS)]


def _slope_column(nrows, head0=0):
    row = lax.broadcasted_iota(jnp.int32, (nrows, 1), 0)
    col = jnp.zeros((nrows, 1), F32)
    for h in range(NSA_HEADS):
        col = jnp.where(row + head0 == h, 2.0 ** (-8.0 * (h + 1) / NSA_HEADS), col)
    return col


def _nsa_sample1_kernel(q_ref, kc_ref, vc_ref, wcache_ref, wnew_ref, ocmp_ref, owin_ref, idx_ref, *,
                        nblk, nblk_pad, past, w_buf):
    hd = NSA_HD
    q = q_ref[0]
    groups = _head_group_mask(NSA_HEADS)
    slope = _slope_column(NSA_HEADS)
    q_pos = past
    cur = q_pos // NSA_BLOCK
    blk = lax.broadcasted_iota(jnp.int32, (1, nblk_pad), 1)
    d_cmp = q_pos - ((blk + 1) * NSA_BLOCK - 1)
    s = jnp.zeros((NSA_HEADS, nblk_pad), F32)
    for g in range(NSA_KV_HEADS):
        s = s + jnp.where(groups[g], _dot_nt(q, kc_ref[0, :, g * hd:(g + 1) * hd]), 0.0)
    p = _masked_softmax(s - slope * d_cmp.astype(F32), (d_cmp >= 0) & (blk < nblk))
    o = jnp.zeros((NSA_HEADS, hd), F32)
    for g in range(NSA_KV_HEADS):
        o = o + jnp.where(groups[g], _dot(p, vc_ref[0, :, g * hd:(g + 1) * hd]), 0.0)
    ocmp_ref[0] = o
    gsel = (lax.broadcasted_iota(jnp.int32, (8, NSA_HEADS), 1) // NSA_GROUP
            == lax.broadcasted_iota(jnp.int32, (8, NSA_HEADS), 0)).astype(F32)
    score = _dot(gsel, p, HIGHEST)
    forced = (blk == 0) | (blk == cur) | (blk == cur - 1)
    score = jnp.where(forced, jnp.inf, score)
    score = jnp.where((blk <= cur) & (blk < nblk), score, -jnp.inf)
    blk_f = blk.astype(F32)
    lane_k = lax.broadcasted_iota(jnp.int32, (1, NSA_TOPK), 1)
    idx = jnp.zeros((8, NSA_TOPK), F32)
    for k in range(NSA_TOPK):
        m = jnp.max(score, axis=-1, keepdims=True)
        first = jnp.min(jnp.where(score == m, blk_f, 1e9), axis=-1, keepdims=True)
        idx = jnp.where(lane_k == k, first, idx)
        score = jnp.where(blk_f == first, -jnp.inf, score)
    idx_ref[0] = idx.astype(jnp.int32)
    i_buf = lax.broadcasted_iota(jnp.int32, (1, w_buf), 1)
    win_pos = past - w_buf + i_buf
    d_win = q_pos - win_pos
    m_win = (d_win >= 0) & (d_win < NSA_WINDOW) & (win_pos >= 0)
    s = jnp.zeros((NSA_HEADS, w_buf), F32)
    k_new = jnp.zeros((NSA_HEADS, hd), F32)
    v_new = jnp.zeros((NSA_HEADS, hd), F32)
    for g in range(NSA_KV_HEADS):
        s = s + jnp.where(groups[g], _dot_nt(q, wcache_ref[0, 0, :, g * hd:(g + 1) * hd]), 0.0)
        k_new = k_new + jnp.where(groups[g], wnew_ref[0, :, g * hd:(g + 1) * hd], 0.0)
        v_new = v_new + jnp.where(groups[g], wnew_ref[0, :, NSA_KV_DIM + g * hd:NSA_KV_DIM + (g + 1) * hd], 0.0)
    s = jnp.where(m_win, s - slope * d_win.astype(F32), -jnp.inf)
    s_new = jnp.sum(q * k_new, axis=-1, keepdims=True)
    m = jnp.maximum(jnp.max(s, axis=-1, keepdims=True), s_new)
    e = jnp.where(m_win, jnp.exp(s - m), 0.0)
    e_new = jnp.exp(s_new - m)
    denom = jnp.maximum(jnp.sum(e, axis=-1, keepdims=True) + e_new, 1e-30)
    o = e_new * v_new
    for g in range(NSA_KV_HEADS):
        o = o + jnp.where(groups[g], _dot(e, wcache_ref[0, 0, :, NSA_KV_DIM + g * hd:NSA_KV_DIM + (g + 1) * hd]), 0.0)
    owin_ref[0] = o / denom


def nsa_sample_attention1(q, k_cmp, v_cmp, cache_win, layer, win_new, nblk, past):
    nbatch = q.shape[0]
    nblk_pad = k_cmp.shape[1]
    w_buf = cache_win.shape[2]
    hq = pl.BlockSpec((1, NSA_HEADS, NSA_HD), lambda b: (b, 0, 0))
    return pl.pallas_call(
        functools.partial(_nsa_sample1_kernel, nblk=nblk, nblk_pad=nblk_pad, past=past, w_buf=w_buf),
        grid=(nbatch,),
        in_specs=[hq, pl.BlockSpec((1, nblk_pad, NSA_KV_DIM), lambda b: (b, 0, 0)),
                  pl.BlockSpec((1, nblk_pad, NSA_KV_DIM), lambda b: (b, 0, 0)),
                  pl.BlockSpec((1, 1, w_buf, 2 * NSA_KV_DIM), lambda b: (layer, b, 0, 0)),
                  pl.BlockSpec((1, 1, 2 * NSA_KV_DIM), lambda b: (b, 0, 0))],
        out_specs=[hq, hq, pl.BlockSpec((1, 8, NSA_TOPK), lambda b: (b, 0, 0))],
        out_shape=[jax.ShapeDtypeStruct((nbatch, NSA_HEADS, NSA_HD), F32),
                   jax.ShapeDtypeStruct((nbatch, NSA_HEADS, NSA_HD), F32),
                   jax.ShapeDtypeStruct((nbatch, 8, NSA_TOPK), jnp.int32)],
        compiler_params=_params(("parallel",)),
        name="nsa_sample_attention1",
    )(q, k_cmp, v_cmp, cache_win, win_new)


def _nsa_sample2_kernel(idx_ref, pt_ref, q_ref, b0_ref, b1_ref, b2_ref, b3_ref, new_ref, ocmp_ref, owin_ref, gates_ref,
                        o_ref, m_scr, l_scr, acc_scr, *, past, n_cache_blk):
    hd = NSA_HD
    b = pl.program_id(0)
    k = pl.program_id(1)
    blocks = (b0_ref, b1_ref, b2_ref, b3_ref)

    @pl.when(k == 0)
    def _():
        m_scr[...] = jnp.full(m_scr.shape, -jnp.inf, F32)
        l_scr[...] = jnp.zeros(l_scr.shape, F32)
        acc_scr[...] = jnp.zeros(acc_scr.shape, F32)

    q_pos = past
    lane = lax.broadcasted_iota(jnp.int32, (1, PAGE_SIZE), 1)
    for g in range(NSA_KV_HEADS):
        j = idx_ref[(b * 8 + g) * NSA_TOPK + k]
        slope = _slope_column(8, head0=g * NSA_GROUP)
        q = q_ref[0, g]
        k_t = blocks[g][0, 0, 0, 0]
        v_t = blocks[g][0, 0, 1, 0]
        d = q_pos - ((j // BLOCKS_PER_PAGE) * PAGE_SIZE + lane)
        in_block = (lane // NSA_BLOCK) == (j % BLOCKS_PER_PAGE)
        valid = (d >= 0) & in_block & (j < n_cache_blk)
        s = jnp.where(valid, _dot(q, k_t) - slope * d.astype(F32), -jnp.inf)
        k_new = new_ref[0, :, g * hd:(g + 1) * hd]
        v_new = new_ref[0, :, NSA_KV_DIM + g * hd:NSA_KV_DIM + (g + 1) * hd]
        new_ok = j == q_pos // NSA_BLOCK
        s_new = jnp.where(new_ok, jnp.sum(q * k_new, axis=-1, keepdims=True), -jnp.inf)
        m_old = m_scr[g]
        m_new = jnp.maximum(jnp.maximum(m_old, jnp.max(s, axis=-1, keepdims=True)), s_new)
        m_safe = jnp.where(m_new > -jnp.inf, m_new, 0.0)
        alpha = jnp.exp(m_old - m_safe)
        p = jnp.where(valid, jnp.exp(s - m_safe), 0.0)
        p_new = jnp.where(new_ok, jnp.exp(s_new - m_safe), 0.0)
        l_scr[g] = alpha * l_scr[g] + jnp.sum(p, axis=-1, keepdims=True) + p_new
        acc_scr[g] = alpha * acc_scr[g] + _dot_nt(p, v_t) + p_new * v_new
        m_scr[g] = m_new

    @pl.when(k == NSA_TOPK - 1)
    def _():
        for g in range(NSA_KV_HEADS):
            o_sel = acc_scr[g] / jnp.maximum(l_scr[g], 1e-30)
            gt = gates_ref[0, g]
            o_ref[0, g] = gt[:, 0:1] * ocmp_ref[0, g] + gt[:, 1:2] * o_sel + gt[:, 2:3] * owin_ref[0, g]


def nsa_sample_attention2(idx, page_table, q8, cache_sel, layer, sel_new, o_cmp8, o_win8, gates8, past):
    nbatch = q8.shape[0]
    n_cache_blk = past // NSA_BLOCK
    cache_t = jnp.transpose(cache_sel, (0, 1, 3, 4, 5, 2))

    def blk_map(g):
        def index_map(b, k, idx_ref, pt_ref):
            j = jnp.minimum(idx_ref[(b * 8 + g) * NSA_TOPK + k], n_cache_blk - 1)
            return (layer, pt_ref[b, j // BLOCKS_PER_PAGE], 0, g, 0, 0)
        return index_map

    g4 = lambda w: pl.BlockSpec((1, NSA_KV_HEADS, 8, w), lambda b, k, i, p: (b, 0, 0, 0))
    grid_spec = pltpu.PrefetchScalarGridSpec(
        num_scalar_prefetch=2,
        grid=(nbatch, NSA_TOPK),
        in_specs=[g4(NSA_HD)] + [pl.BlockSpec((1, 1, 2, 1, NSA_HD, PAGE_SIZE), blk_map(g)) for g in range(NSA_KV_HEADS)]
                 + [pl.BlockSpec((1, 1, 2 * NSA_KV_DIM), lambda b, k, i, p: (b, 0, 0)), g4(NSA_HD), g4(NSA_HD), g4(LANES)],
        out_specs=g4(NSA_HD),
        scratch_shapes=[pltpu.VMEM((NSA_KV_HEADS, 8, 1), F32), pltpu.VMEM((NSA_KV_HEADS, 8, 1), F32),
                        pltpu.VMEM((NSA_KV_HEADS, 8, NSA_HD), F32)],
    )
    return pl.pallas_call(
        functools.partial(_nsa_sample2_kernel, past=past, n_cache_blk=n_cache_blk),
        grid_spec=grid_spec,
        out_shape=jax.ShapeDtypeStruct((nbatch, NSA_KV_HEADS, 8, NSA_HD), F32),
        compiler_params=_params(("parallel", "arbitrary")),
        name="nsa_sample_attention2",
    )(idx, page_table, q8, cache_t, cache_t, cache_t, cache_t, sel_new, o_cmp8, o_win8, gates8)


GLA_Q0, GLA_K0, GLA_V0 = 0, GLA_QK_DIM, 2 * GLA_QK_DIM
GLA_A0 = GLA_V0 + GLA_V_DIM
GLA_G0 = GLA_A0 + GLA_A_W


def _gla_kernel(z_ref, s0_ref, a2_ref, ab_ref, ng_ref, o_ref, st_ref, s_scr, *, tt, t_valid, n_t):
    ti = pl.program_id(1)
    c = GLA_CHUNK

    @pl.when(ti == 0)
    def _():
        for h in range(GLA_HEADS):
            s_scr[h] = s0_ref[0, h].T

    a_lr = z_ref[0, :, GLA_A0:GLA_A0 + GLA_A_W].astype(BF16)
    pre = _dot(a_lr, a2_ref[...]) + ab_ref[...]
    log_a = (jnp.minimum(pre, 0.0) - jnp.log(1.0 + jnp.exp(-jnp.abs(pre)))) / GLA_TAU
    row = ti * tt + lax.broadcasted_iota(jnp.int32, (tt, 1), 0)
    log_a = jnp.where(row < t_valid, log_a, 0.0)
    tri = lax.broadcasted_iota(jnp.int32, (c, c), 0) >= lax.broadcasted_iota(jnp.int32, (c, c), 1)
    tri_f = tri.astype(F32)
    hr = range(GLA_HEADS)
    ksl = [slice(h * GLA_DK, (h + 1) * GLA_DK) for h in hr]
    states = [s_scr[h] for h in hr]
    for ci in range(tt // c):
        rows = slice(ci * c, (ci + 1) * c)
        cum = [_dot(tri_f, log_a[rows, ksl[h]], HIGHEST) for h in hr]
        last = [cum[h][c - 1:c, :] for h in hr]
        q_c = [z_ref[0, rows, GLA_Q0 + h * GLA_DK:GLA_Q0 + (h + 1) * GLA_DK] * (GLA_DK ** -0.5) for h in hr]
        k_c = [z_ref[0, rows, GLA_K0 + h * GLA_DK:GLA_K0 + (h + 1) * GLA_DK] for h in hr]
        v_c = [z_ref[0, rows, GLA_V0 + h * GLA_DV:GLA_V0 + (h + 1) * GLA_DV].astype(BF16) for h in hr]
        q_dec = [(q_c[h] * jnp.exp(cum[h])).astype(BF16) for h in hr]
        k_inv = [(k_c[h] * jnp.exp(-cum[h])).astype(BF16) for h in hr]
        k_dec = [(k_c[h] * jnp.exp(last[h] - cum[h])).astype(BF16) for h in hr]
        att = [jnp.where(tri, _dot_nt(q_dec[h], k_inv[h]), 0.0).astype(BF16) for h in hr]
        o = [_dot(att[h], v_c[h]) + _dot_nt(q_dec[h], states[h].astype(BF16)) for h in hr]
        states = [states[h] * jnp.exp(last[h]) + _dot_tn(v_c[h], k_dec[h]) for h in hr]
        for h in hr:
            on = o[h] * lax.rsqrt(jnp.mean(o[h] * o[h], axis=-1, keepdims=True) + NORM_EPS) * ng_ref[...]
            gate = z_ref[0, rows, GLA_G0 + h * GLA_DV:GLA_G0 + (h + 1) * GLA_DV]
            o_ref[0, rows, h * GLA_DV:(h + 1) * GLA_DV] = (on * (gate * _sigmoid(gate))).astype(o_ref.dtype)
    for h in hr:
        s_scr[h] = states[h]

    @pl.when(ti == n_t - 1)
    def _():
        for h in range(GLA_HEADS):
            st_ref[0, h] = s_scr[h].T


def gla_mixer(z_gla, s0, a2p, ab, ng, t_valid):
    b, t, _ = z_gla.shape
    tt = min(t, 512)
    n_t = t // tt
    return pl.pallas_call(
        functools.partial(_gla_kernel, tt=tt, t_valid=t_valid, n_t=n_t),
        grid=(b, n_t),
        in_specs=[pl.BlockSpec((1, tt, GLA_W), lambda i, j: (i, j, 0)),
                  pl.BlockSpec((1, GLA_HEADS, GLA_DK, GLA_DV), lambda i, j: (i, 0, 0, 0)),
                  pl.BlockSpec((GLA_A_W, GLA_QK_DIM), lambda i, j: (0, 0)),
                  pl.BlockSpec((1, GLA_QK_DIM), lambda i, j: (0, 0)),
                  pl.BlockSpec((1, GLA_DV), lambda i, j: (0, 0))],
        out_specs=[pl.BlockSpec((1, tt, GLA_V_DIM), lambda i, j: (i, j, 0)),
                   pl.BlockSpec((1, GLA_HEADS, GLA_DK, GLA_DV), lambda i, j: (i, 0, 0, 0))],
        out_shape=[jax.ShapeDtypeStruct((b, t, GLA_V_DIM), BF16),
                   jax.ShapeDtypeStruct((b, GLA_HEADS, GLA_DK, GLA_DV), F32)],
        scratch_shapes=[pltpu.VMEM((GLA_HEADS, GLA_DV, GLA_DK), F32)],
        compiler_params=_params(("parallel", "arbitrary")),
        name="gla_mixer",
    )(z_gla, s0, a2p, ab, ng)


RW_R0, RW_K0, RW_V0 = 0, RW_DIM, 2 * RW_DIM
RW_WL0 = 3 * RW_DIM
RW_AL0 = RW_WL0 + RW_LR_W
RW_GL0 = RW_AL0 + RW_LR_W


def _rw_prep_kernel(z_ref, sh_ref, mu_ref, w0_ref, w2_ref, a0_ref, a2_ref, g2_ref, kk_ref, ka_ref, rk_ref, seg_ref,
                    r_ref, dec_ref, k_ref, v_ref, nkk_ref, kka_ref, g_ref, bonus_ref):
    z = z_ref[...]
    xm = z + (sh_ref[...] - z) * mu_ref[...]
    r = xm[:, RW_R0:RW_R0 + RW_DIM]
    k = xm[:, RW_K0:RW_K0 + RW_DIM]
    v = xm[:, RW_V0:RW_V0 + RW_DIM]
    wl = xm[:, RW_WL0:RW_WL0 + RW_LR_W]
    al = xm[:, RW_AL0:RW_AL0 + RW_LR_W]
    gl = xm[:, RW_GL0:RW_GL0 + RW_LR_W]
    w = -_softplus(-(w0_ref[...] + _dot(jnp.tanh(wl).astype(BF16), w2_ref[...]))) - 0.5
    a = _sigmoid(a0_ref[...] + _dot(al.astype(BF16), a2_ref[...]))
    kk = k * kk_ref[...]
    norm = jnp.sqrt(_dot(kk * kk, seg_ref[...], HIGHEST))
    kk = kk / jnp.maximum(norm, 1e-12)
    k2 = k * (1.0 + (a - 1.0) * ka_ref[...])
    r_ref[...] = r
    dec_ref[...] = -jnp.exp(w)
    k_ref[...] = k2
    v_ref[...] = v
    nkk_ref[...] = -kk
    kka_ref[...] = kk * a
    g_ref[...] = _dot(_sigmoid(gl).astype(BF16), g2_ref[...])
    bonus_ref[...] = _dot(r * k2 * rk_ref[...], seg_ref[...], HIGHEST) * v


def rw_prep(z_rw, shifted, rp, tm=512):
    m = z_rw.shape[0]
    tm = min(tm, m)
    row = lambda w: pl.BlockSpec((tm, w), lambda i: (i, 0))
    const = lambda r, w: pl.BlockSpec((r, w), lambda i: (0, 0))
    out = jax.ShapeDtypeStruct((m, RW_DIM), F32)
    return pl.pallas_call(
        _rw_prep_kernel,
        grid=(m // tm,),
        in_specs=[row(RW_W), row(RW_W), const(1, RW_W), const(1, RW_DIM), const(RW_LR_W, RW_DIM), const(1, RW_DIM),
                  const(RW_LR_W, RW_DIM), const(RW_LR_W, RW_DIM), const(1, RW_DIM), const(1, RW_DIM), const(1, RW_DIM),
                  const(RW_DIM, RW_DIM)],
        out_specs=[row(RW_DIM)] * 8,
        out_shape=[out] * 8,
        compiler_params=_params(("parallel",)),
        name="rw_prep",
    )(z_rw, shifted, rp["mu"], rp["w0"], rp["w2"], rp["a0"], rp["a2"], rp["g2"], rp["kk"], rp["ka"], rp["rk"], rp["seg"])


RW_CHUNK = 64
RW_CHUNK_LOG2 = 6


def _rw_scan_kernel(r_ref, dec_ref, k_ref, v_ref, nkk_ref, kka_ref, g_ref, bonus_ref, lng_ref, lnb_ref, s0_ref,
                    o_ref, st_ref, s_scr, *, tt, n_t):
    ti = pl.program_id(1)

    @pl.when(ti == 0)
    def _():
        s_scr[...] = s0_ref[0]

    c = RW_CHUNK
    row_i = lax.broadcasted_iota(jnp.int32, (c, c), 0)
    col_i = lax.broadcasted_iota(jnp.int32, (c, c), 1)
    lower = row_i >= col_i
    strict = row_i > col_i
    lower_b = lower.astype(BF16)
    eye_f = (row_i == col_i).astype(F32)
    heads = [slice(h * RW_HD, (h + 1) * RW_HD) for h in range(RW_HEADS)]
    sp = _split2
    mm = functools.partial(_dot3, _dot)
    mm_nt = functools.partial(_dot3, _dot_nt)
    mm_tn = functools.partial(_dot3, _dot_tn)

    def chunk(i, carry):
        rows = pl.ds(pl.multiple_of(i * c, c), c)
        st = []
        for h, hs in enumerate(heads):
            lw = dec_ref[0, rows, hs]
            cum = sum(_dot(lower_b, part) for part in _split3(lw))
            g_inv = jnp.exp(-cum)
            d = {"g_last": jnp.exp(cum[c - 1:c, :]), "s0_f": s_scr[h],
                 "at_f": kka_ref[0, rows, hs] * g_inv, "kt_f": k_ref[0, rows, hs] * g_inv,
                 "bt": sp(nkk_ref[0, rows, hs] * jnp.exp(cum - lw)), "rt": sp(r_ref[0, rows, hs] * jnp.exp(cum)),
                 "v": sp(v_ref[0, rows, hs]),
                 "tail": (lng_ref[:, hs], lnb_ref[:, hs], bonus_ref[0, rows, hs], g_ref[0, rows, hs])}
            d["at"], d["kt"], d["s0"] = sp(d["at_f"]), sp(d["kt_f"]), sp(d["s0_f"])
            st.append(d)
        for d in st:
            d["ma"] = jnp.where(strict, mm_nt(d["bt"], d["at"]), 0.0)
            d["mk"] = sp(jnp.where(strict, mm_nt(d["bt"], d["kt"]), 0.0))
            d["na"] = sp(jnp.where(lower, mm_nt(d["rt"], d["at"]), 0.0))
            d["nk"] = sp(jnp.where(lower, mm_nt(d["rt"], d["kt"]), 0.0))
            d["inv"] = eye_f + d["ma"]
            d["power"] = sp(d["ma"])
            d["rhs"] = mm_nt(d["bt"], d["s0"]) + mm(d["mk"], d["v"])
        for _ in range(RW_CHUNK_LOG2 - 1):
            for d in st:
                d["power"] = sp(mm(d["power"], d["power"]))
                d["inv"] = d["inv"] + mm(sp(d["inv"]), d["power"])
        outs = []
        for d in st:
            u = sp(mm(sp(d["inv"]), sp(d["rhs"])))
            o = mm_nt(d["rt"], d["s0"]) + mm(d["na"], u) + mm(d["nk"], d["v"])
            s_new = (d["s0_f"] * d["g_last"] + mm_tn(u, sp(d["at_f"] * d["g_last"]))
                     + mm_tn(d["v"], sp(d["kt_f"] * d["g_last"])))
            ln_g, ln_b, bonus, gate = d["tail"]
            mean = jnp.mean(o, axis=-1, keepdims=True)
            cen = o - mean
            var = jnp.mean(cen * cen, axis=-1, keepdims=True)
            y = cen * lax.rsqrt(var + RW_GN_EPS) * ln_g + ln_b
            outs.append((s_new, (y + bonus) * gate))
        for h, hs in enumerate(heads):
            s_scr[h] = outs[h][0]
            o_ref[0, rows, hs] = outs[h][1]
        return carry

    lax.fori_loop(0, tt // c, chunk, 0)

    @pl.when(ti == n_t - 1)
    def _():
        st_ref[0] = s_scr[...]


def rw_scan(r, dec, k, v, nkk, kka, g, bonus, ln_g, ln_b, s0):
    b, t, _ = r.shape
    tt = min(t, 256)
    n_t = t // tt
    tile = pl.BlockSpec((1, tt, RW_DIM), lambda i, j: (i, j, 0))
    vec = pl.BlockSpec((1, RW_DIM), lambda i, j: (0, 0))
    st = pl.BlockSpec((1, RW_HEADS, RW_HD, RW_HD), lambda i, j: (i, 0, 0, 0))
    return pl.pallas_call(
        functools.partial(_rw_scan_kernel, tt=tt, n_t=n_t),
        grid=(b, n_t),
        in_specs=[tile] * 8 + [vec, vec, st],
        out_specs=[tile, st],
        out_shape=[jax.ShapeDtypeStruct((b, t, RW_DIM), F32), jax.ShapeDtypeStruct(s0.shape, F32)],
        scratch_shapes=[pltpu.VMEM((RW_HEADS, RW_HD, RW_HD), F32)],
        compiler_params=_params(("parallel", "arbitrary")),
        name="rw_scan",
    )(r, dec, k, v, nkk, kka, g, bonus, ln_g, ln_b, s0)


def _pad_last(w, width):
    return jnp.pad(w, [(0, 0)] * (w.ndim - 1) + [(0, width - w.shape[-1])])


def _pad_rows(w, rows):
    return jnp.pad(w, [(0, rows - w.shape[0])] + [(0, 0)] * (w.ndim - 1))


def _split_last(z, sizes):
    return jnp.split(z, [int(c) for c in np.cumsum(np.array(sizes))[:-1]], axis=-1)


def _rw_padded(cols):
    r, k, v, wl, al, gl = _split_last(cols, RW_SIZES)
    return jnp.concatenate([r, k, v, _pad_last(wl, RW_LR_W), _pad_last(al, RW_LR_W), _pad_last(gl, RW_LR_W)], axis=-1)


def _rw_unpadded(cols):
    return jnp.concatenate([cols[..., :RW_WL0], cols[..., RW_WL0:RW_WL0 + RW_DECAY_RANK],
                            cols[..., RW_AL0:RW_AL0 + RW_A_RANK], cols[..., RW_GL0:RW_GL0 + RW_G_RANK]], axis=-1)


def _layer_params(l, p):
    q, cmp_, sel, win, gate, gq, gk, gv, ga, gg, rw, mg = _split_last(p["w_in"][l], IN_SIZES)
    seg = np.kron(np.eye(RW_HEADS, dtype=np.float32), np.ones((RW_HD, RW_HD), np.float32))
    row = lambda v: v.reshape(1, -1)
    eye2 = jnp.eye(BLOCKS_PER_PAGE, dtype=F32)
    w1 = p["cmp_w1"][l].reshape(2, NSA_BLOCK, NSA_HD, NSA_CMP_HID)
    wd = jnp.einsum("ab,kldj->dalkbj", eye2, w1).reshape(NSA_HD * BLOCKS_PER_PAGE * NSA_BLOCK,
                                                          2 * BLOCKS_PER_PAGE * NSA_CMP_HID)
    w2bd = jnp.einsum("ab,kje->kajbe", eye2, p["cmp_w2"][l]).reshape(2, BLOCKS_PER_PAGE * NSA_CMP_HID,
                                                                      BLOCKS_PER_PAGE * NSA_HD)
    pos_dl = jnp.tile(jnp.swapaxes(p["cmp_pos"][l], 1, 2), (1, 1, BLOCKS_PER_PAGE))
    pos_t = jnp.swapaxes(jnp.repeat(pos_dl, NSA_KV_HEADS, axis=0), 0, 1)
    return {
        "norm1_g": p["norm1_g"][l], "norm2_g": p["norm2_g"][l],
        "w_nsa": jnp.concatenate([q, cmp_, sel, win, _pad_last(gate, GATE_W)], axis=1).astype(BF16),
        "w_gla": jnp.concatenate([gq, gk, gv, _pad_last(ga, GLA_A_W), gg], axis=1).astype(BF16),
        "w_rw": _rw_padded(rw).astype(BF16),
        "w_mg": mg.astype(BF16),
        "qk_g": p["nsa_qk_g"][l],
        "cw": {"pos2": jnp.concatenate([p["cmp_pos"][l]] * 2, axis=-1),
               "w1": p["cmp_w1"][l].reshape(2, NSA_BLOCK, NSA_HD, NSA_CMP_HID).astype(BF16),
               "w2": p["cmp_w2"][l].astype(BF16),
               "kg": row(p["nsa_qk_g"][l, 1]),
               "pos_t": pos_t, "wd": wd.astype(BF16), "w2bd": w2bd.astype(BF16)},
        "gla_a2": _pad_rows(p["gla_a2"][l], GLA_A_W).astype(BF16),
        "gla_ab": row(p["gla_a_b"][l]), "gla_ng": row(p["gla_norm_g"][l]),
        "rp": {"mu": row(_rw_padded(p["rw_mu"][l])), "w0": row(p["rw_w0"][l]),
               "w2": _pad_rows(p["rw_w2"][l], RW_LR_W).astype(BF16), "a0": row(p["rw_a0"][l]),
               "a2": _pad_rows(p["rw_a2"][l], RW_LR_W).astype(BF16),
               "g2": _pad_rows(p["rw_g2"][l], RW_LR_W).astype(BF16),
               "kk": row(p["rw_kk"][l]), "ka": row(p["rw_ka"][l]), "rk": row(p["rw_rk"][l]), "seg": jnp.asarray(seg)},
        "ln_g": row(p["rw_ln_g"][l]), "ln_b": row(p["rw_ln_b"][l]),
        "nsa_up": p["nsa_up"][l].astype(BF16), "gla_up": p["gla_up"][l].astype(BF16), "rw_up": p["rw_up"][l].astype(BF16),
        "w_out": p["w_out"][l].astype(BF16), "mlp_w1": p["mlp_w1"][l].astype(BF16), "mlp_w2": p["mlp_w2"][l].astype(BF16),
    }


def _project_in(x2, lp, tm):
    hn = rmsnorm_rows(x2, lp["norm1_g"], tm)
    return tuple(matmul(hn, lp[w], tm=tm) for w in ("w_nsa", "w_gla", "w_rw", "w_mg"))


def _project_out(x2, o_nsa, o_gla, o_rw, z_mg, lp, tm):
    merged = merge_branches(o_nsa, o_gla, o_rw, lp["nsa_up"], lp["gla_up"], lp["rw_up"], z_mg, tm=tm)
    x1 = matmul(merged, lp["w_out"], "residual", x2, tm=tm)
    h2 = rmsnorm_rows(x1, lp["norm2_g"], tm)
    u = matmul(h2, lp["mlp_w1"], "relu2", tm=tm)
    return matmul(u, lp["mlp_w2"], "residual", x1, tm=tm)


def _kv_rows(rows, b, t):
    return rows.reshape(b, t, 2, NSA_KV_HEADS, NSA_HD)


def _prompt_layer(x, lp):
    b, t, d = x.shape
    m = b * t
    tm = 512
    x2 = x.reshape(m, d)
    z_nsa, z_gla, z_rw, z_mg = _project_in(x2, lp, tm)
    qn, cmp_rows, sel_rows, win_rows, sel_b, win_b, gates = nsa_prep(z_nsa, lp["qk_g"])
    nblk = t // NSA_BLOCK
    k_cmp, v_cmp = compress_rows(cmp_rows, lp["cw"], nb=32)
    per_b = lambda a: a.reshape(b, t, a.shape[-1])
    o_nsa = nsa_prompt_attention(per_b(qn), per_b(gates), per_b(sel_b), per_b(win_b),
                                 k_cmp.reshape(b, nblk, NSA_KV_DIM), v_cmp.reshape(b, nblk, NSA_KV_DIM))
    o_gla, gla_st = gla_mixer(per_b(z_gla), jnp.zeros((b, GLA_HEADS, GLA_DK, GLA_DV), F32),
                              lp["gla_a2"], lp["gla_ab"], lp["gla_ng"], t_valid=t)
    z_rw3 = per_b(z_rw)
    shifted = jnp.concatenate([jnp.zeros((b, 1, RW_W), F32), z_rw3[:, :-1]], axis=1).reshape(m, RW_W)
    prep = rw_prep(z_rw, shifted, lp["rp"])
    o_rw, rw_st = rw_scan(*(per_b(a) for a in prep), lp["ln_g"], lp["ln_b"],
                          jnp.zeros((b, RW_HEADS, RW_HD, RW_HD), F32))
    y = _project_out(x2, o_nsa.reshape(m, -1), o_gla.reshape(m, -1), o_rw.reshape(m, -1), z_mg, lp, tm)
    w_keep = min(NSA_WINDOW, t)
    return (y.reshape(b, t, d), _kv_rows(cmp_rows, b, t), _kv_rows(sel_rows, b, t), _kv_rows(win_rows, b, t)[:, t - w_keep:],
            gla_st, rw_st, _rw_unpadded(z_rw3[:, -1]))


SAMPLE_ROWS = SUBLANES_BF16


def _sample_layer(x2, lp, layer, caches, page_table, nbatch):
    cache_cmp, cache_sel, cache_win, state_gla, state_rwkv, state_shift = caches
    tm = SAMPLE_ROWS
    past = page_table.shape[1] * PAGE_SIZE
    nblk = -(-(past + 1) // NSA_BLOCK)
    z_nsa, z_gla, z_rw, z_mg = _project_in(x2, lp, tm)
    qn, cmp_rows, sel_rows, win_rows, _, _, gates = nsa_prep(z_nsa, lp["qk_g"])
    kc_past, vc_past = compress_pages(cache_cmp, layer, page_table, lp["cw"])
    last_blk = jnp.pad(cmp_rows[:nbatch, None, :], ((0, 0), (0, NSA_BLOCK - 1), (0, 0))).reshape(nbatch * NSA_BLOCK, -1)
    kc_last, vc_last = compress_rows(last_blk, lp["cw"], nb=nbatch)
    nblk_pad = -(-nblk // 8) * 8
    with_last = lambda past_s, last_s: jnp.pad(jnp.concatenate([past_s, last_s[:, None, :]], axis=1),
                                               ((0, 0), (0, nblk_pad - nblk), (0, 0)))
    q16 = qn[:nbatch].astype(F32).reshape(nbatch, NSA_HEADS, NSA_HD)
    o_cmp, o_win, idx = nsa_sample_attention1(
        q16, with_last(kc_past, kc_last), with_last(vc_past, vc_last),
        cache_win.reshape(cache_win.shape[0], nbatch, cache_win.shape[2], 2 * NSA_KV_DIM), layer,
        win_rows[:nbatch, None, :], nblk, past)
    grp8 = lambda a: jnp.pad(a.reshape(nbatch, NSA_KV_HEADS, NSA_GROUP, a.shape[-1]),
                             ((0, 0), (0, 0), (0, 8 - NSA_GROUP), (0, 0)))
    gates3 = gates[:nbatch, :3 * NSA_HEADS].reshape(nbatch, 3, NSA_HEADS).transpose(0, 2, 1)
    o8 = nsa_sample_attention2(idx.reshape(-1), page_table, grp8(q16), cache_sel, layer, sel_rows[:nbatch, None, :],
                               grp8(o_cmp), grp8(o_win), grp8(_pad_last(gates3, LANES)), past)
    o_nsa = _pad_rows(o8[:, :, :NSA_GROUP].reshape(nbatch, NSA_Q_DIM), tm)
    z_gla_c = jnp.pad(z_gla[:nbatch, None, :], ((0, 0), (0, GLA_CHUNK - 1), (0, 0)))
    o_gla, gla_st = gla_mixer(z_gla_c, state_gla[layer], lp["gla_a2"], lp["gla_ab"], lp["gla_ng"], t_valid=1)
    o_gla = _pad_rows(o_gla[:, 0], tm)
    shifted = _pad_rows(_rw_padded(state_shift[layer]), tm)
    prep = rw_prep(z_rw, shifted, lp["rp"])
    prep = [jnp.pad(a[:nbatch, None, :], ((0, 0), (0, RW_CHUNK - 1), (0, 0))) for a in prep]
    o_rw, rw_st = rw_scan(*prep, lp["ln_g"], lp["ln_b"], state_rwkv[layer])
    o_rw = _pad_rows(o_rw[:, 0], tm)
    y = _project_out(x2, o_nsa, o_gla, o_rw, z_mg, lp, tm)
    return (y, _kv_rows(cmp_rows[:nbatch], nbatch, 1), _kv_rows(sel_rows[:nbatch], nbatch, 1),
            _kv_rows(win_rows[:nbatch], nbatch, 1), gla_st, rw_st, _rw_unpadded(z_rw[:nbatch]))


def kernel(x_prompt, x_sample, cache_cmp_kv, cache_sel_kv, cache_win_kv, state_gla, state_rwkv, state_rwkv_shift,
           page_table, norm1_g, w_in, nsa_qk_g, cmp_pos, cmp_w1, cmp_w2, nsa_up, gla_a2, gla_a_b, gla_norm_g, gla_up,
           rw_mu, rw_w0, rw_w2, rw_a0, rw_a2, rw_g2, rw_kk, rw_ka, rw_rk, rw_ln_g, rw_ln_b, rw_up, w_out, norm2_g,
           mlp_w1, mlp_w2):
    p = dict(norm1_g=norm1_g, w_in=w_in, nsa_qk_g=nsa_qk_g, cmp_pos=cmp_pos, cmp_w1=cmp_w1, cmp_w2=cmp_w2, nsa_up=nsa_up,
             gla_a2=gla_a2, gla_a_b=gla_a_b, gla_norm_g=gla_norm_g, gla_up=gla_up, rw_mu=rw_mu, rw_w0=rw_w0, rw_w2=rw_w2,
             rw_a0=rw_a0, rw_a2=rw_a2, rw_g2=rw_g2, rw_kk=rw_kk, rw_ka=rw_ka, rw_rk=rw_rk, rw_ln_g=rw_ln_g,
             rw_ln_b=rw_ln_b, rw_up=rw_up, w_out=w_out, norm2_g=norm2_g, mlp_w1=mlp_w1, mlp_w2=mlp_w2)
    nbatch = x_sample.shape[0]
    caches = (cache_cmp_kv, cache_sel_kv, cache_win_kv, state_gla, state_rwkv, state_rwkv_shift)
    xp = x_prompt
    xs = _pad_rows(x_sample.reshape(nbatch, D_MODEL), SAMPLE_ROWS)
    outs_p, outs_s = [], []
    for l in range(DEPTH):
        lp = _layer_params(l, p)
        xp, *rest_p = _prompt_layer(xp, lp)
        xs, *rest_s = _sample_layer(xs, lp, l, caches, page_table, nbatch)
        outs_p.append(rest_p)
        outs_s.append(rest_s)
    stack = lambda outs, i: jnp.stack([o[i] for o in outs])
    return ((xp, xs[:nbatch].reshape(x_sample.shape))
            + tuple(stack(outs_p, i) for i in range(6)) + tuple(stack(outs_s, i) for i in range(6)))
```

```python
import functools

import numpy as np
import jax
import jax.numpy as jnp
from jax import lax
from jax.experimental import pallas as pl
from jax.experimental.pallas import tpu as pltpu

F32 = jnp.float32
BF16 = jnp.bfloat16
HIGHEST = lax.Precision.HIGHEST

D_MODEL = 2048
DEPTH = 2
PAGE_SIZE = 128
NSA_HEADS = 16
NSA_KV_HEADS = 4
NSA_GROUP = NSA_HEADS // NSA_KV_HEADS
NSA_HD = 64
NSA_BLOCK = 64
NSA_TOPK = 16
NSA_WINDOW = 512
NSA_CMP_HID = 128
GLA_HEADS = 4
GLA_DK = 64
GLA_DV = 128
GLA_GATE_RANK = 16
GLA_TAU = 16.0
GLA_CHUNK = 64
RW_HEADS = 8
RW_HD = 64
RW_DECAY_RANK = 32
RW_A_RANK = 32
RW_G_RANK = 96
RW_GN_EPS = 64e-5
D_FF = 4 * D_MODEL
NORM_EPS = 1e-6

NSA_Q_DIM = NSA_HEADS * NSA_HD
NSA_KV_DIM = NSA_KV_HEADS * NSA_HD
GLA_QK_DIM = GLA_HEADS * GLA_DK
GLA_V_DIM = GLA_HEADS * GLA_DV
RW_DIM = RW_HEADS * RW_HD
RW_SIZES = (RW_DIM, RW_DIM, RW_DIM, RW_DECAY_RANK, RW_A_RANK, RW_G_RANK)
RW_PROJ = sum(RW_SIZES)
IN_SIZES = (NSA_Q_DIM, 2 * NSA_KV_DIM, 2 * NSA_KV_DIM, 2 * NSA_KV_DIM, 3 * NSA_HEADS,
            GLA_QK_DIM, GLA_QK_DIM, GLA_V_DIM, GLA_GATE_RANK, GLA_V_DIM, RW_PROJ, 3 * D_MODEL)

LANES = 128
SUBLANES_BF16 = 16
VMEM_LIMIT = 56 * 1024 * 1024

GATE_W = LANES
NSA_W = NSA_Q_DIM + 6 * NSA_KV_DIM + GATE_W
GLA_A_W = LANES
GLA_W = 2 * GLA_QK_DIM + 2 * GLA_V_DIM + GLA_A_W
RW_LR_W = LANES
RW_W = 3 * RW_DIM + 3 * RW_LR_W
MERGE_W = 3 * D_MODEL


def _params(sem):
    return pltpu.CompilerParams(dimension_semantics=sem, vmem_limit_bytes=VMEM_LIMIT)


def _largest_tile(n, cap):
    best = None
    for t in range(LANES, min(n, cap) + 1, LANES):
        if n % t == 0:
            best = t
    return best if best is not None else n


def _sigmoid(x):
    return 1.0 / (1.0 + jnp.exp(-x))


def _softplus(y):
    return jnp.maximum(y, 0.0) + jnp.log(1.0 + jnp.exp(-jnp.abs(y)))


def _masked_softmax(s, mask):
    s = jnp.where(mask, s, -jnp.inf)
    m = jnp.max(s, axis=-1, keepdims=True)
    m = jnp.where(m > -jnp.inf, m, 0.0)
    e = jnp.where(mask, jnp.exp(s - m), 0.0)
    return e / jnp.maximum(jnp.sum(e, axis=-1, keepdims=True), 1e-30)


def _masked_softmax_rows(s, mask):
    s = jnp.where(mask, s, -jnp.inf)
    m = jnp.max(s, axis=0, keepdims=True)
    m = jnp.where(m > -jnp.inf, m, 0.0)
    e = jnp.where(mask, jnp.exp(s - m), 0.0)
    return e / jnp.maximum(jnp.sum(e, axis=0, keepdims=True), 1e-30)


def _dot(a, b, precision=None):
    return jnp.dot(a, b, preferred_element_type=F32, precision=precision)


def _dot_nt(a, b, precision=None):
    return lax.dot_general(a, b, (((1,), (1,)), ((), ())), preferred_element_type=F32, precision=precision)


def _dot_tn(a, b, precision=None):
    return lax.dot_general(a, b, (((0,), (0,)), ((), ())), preferred_element_type=F32, precision=precision)


def _split2(x):
    hi = x.astype(BF16)
    return hi, (x - hi.astype(F32)).astype(BF16)


def _split3(x):
    hi = x.astype(BF16)
    rest = x - hi.astype(F32)
    mid = rest.astype(BF16)
    return hi, mid, (rest - mid.astype(F32)).astype(BF16)


def _dot3(dot, a, b):
    (a_hi, a_lo), (b_hi, b_lo) = a, b
    return dot(a_hi, b_hi) + dot(a_hi, b_lo) + dot(a_lo, b_hi)


def _rmsnorm_kernel(x_ref, g_ref, o_ref):
    x = x_ref[...]
    y = x * lax.rsqrt(jnp.mean(x * x, axis=-1, keepdims=True) + NORM_EPS) * g_ref[...]
    o_ref[...] = y.astype(o_ref.dtype)


def rmsnorm_rows(x, g, tm):
    m, d = x.shape
    return pl.pallas_call(
        _rmsnorm_kernel,
        grid=(m // tm,),
        in_specs=[pl.BlockSpec((tm, d), lambda i: (i, 0)), pl.BlockSpec((1, d), lambda i: (0, 0))],
        out_specs=pl.BlockSpec((tm, d), lambda i: (i, 0)),
        out_shape=jax.ShapeDtypeStruct((m, d), BF16),
        compiler_params=_params(("parallel",)),
        name="rmsnorm_rows",
    )(x, g.reshape(1, d))


def _mm_kernel(*refs, nk, mode):
    if mode == "residual":
        a_ref, w_ref, r_ref, o_ref = refs[:4]
        scratch = refs[4:]
    else:
        a_ref, w_ref, o_ref = refs[:3]
        scratch = refs[3:]
    part = _dot(a_ref[...], w_ref[...])

    def finish(acc):
        if mode == "relu2":
            acc = jnp.square(jnp.maximum(acc, 0.0))
        elif mode == "residual":
            acc = r_ref[...] + acc
        o_ref[...] = acc.astype(o_ref.dtype)

    if nk == 1:
        finish(part)
    else:
        acc_ref = scratch[0]
        k = pl.program_id(2)

        @pl.when(k == 0)
        def _():
            acc_ref[...] = part

        @pl.when(k > 0)
        def _():
            acc_ref[...] += part

        @pl.when(k == nk - 1)
        def _():
            finish(acc_ref[...])


def matmul(a, w, mode="plain", residual=None, tm=512):
    m, kdim = a.shape
    n = w.shape[1]
    tm = min(tm, m)
    tk = min(kdim, 2048)
    nk = kdim // tk
    tn = _largest_tile(n, 1024 if (mode == "residual" or nk > 1) else 2048)
    in_specs = [pl.BlockSpec((tm, tk), lambda i, j, k: (i, k)), pl.BlockSpec((tk, tn), lambda i, j, k: (k, j))]
    args = [a, w]
    if mode == "residual":
        in_specs.append(pl.BlockSpec((tm, tn), lambda i, j, k: (i, j)))
        args.append(residual)
    out_dtype = BF16 if mode == "relu2" else F32
    return pl.pallas_call(
        functools.partial(_mm_kernel, nk=nk, mode=mode),
        grid=(m // tm, n // tn, nk),
        in_specs=in_specs,
        out_specs=pl.BlockSpec((tm, tn), lambda i, j, k: (i, j)),
        out_shape=jax.ShapeDtypeStruct((m, n), out_dtype),
        scratch_shapes=[pltpu.VMEM((tm, tn), F32)] if nk > 1 else [],
        compiler_params=_params(("parallel", "parallel", "arbitrary")),
        name="matmul_" + mode,
    )(*args)


def _merge_kernel(on_ref, og_ref, or_ref, wn_ref, wg_ref, wr_ref, m0_ref, m1_ref, m2_ref, o_ref):
    acc = _sigmoid(m0_ref[...]) * _dot(on_ref[...].astype(BF16), wn_ref[...])
    acc += _sigmoid(m1_ref[...]) * _dot(og_ref[...].astype(BF16), wg_ref[...])
    acc += _sigmoid(m2_ref[...]) * _dot(or_ref[...].astype(BF16), wr_ref[...])
    o_ref[...] = acc.astype(o_ref.dtype)


def merge_branches(o_nsa, o_gla, o_rw, w_nsa, w_gla, w_rw, z_merge, tm=512, tn=1024):
    m = o_nsa.shape[0]
    tm = min(tm, m)
    nj = D_MODEL // tn
    row = lambda kd: pl.BlockSpec((tm, kd), lambda i, j: (i, 0))
    wsp = lambda kd: pl.BlockSpec((kd, tn), lambda i, j: (0, j))
    gate = lambda b: pl.BlockSpec((tm, tn), lambda i, j: (i, j + b * nj))
    return pl.pallas_call(
        _merge_kernel,
        grid=(m // tm, nj),
        in_specs=[row(NSA_Q_DIM), row(GLA_V_DIM), row(RW_DIM), wsp(NSA_Q_DIM), wsp(GLA_V_DIM), wsp(RW_DIM),
                  gate(0), gate(1), gate(2)],
        out_specs=pl.BlockSpec((tm, tn), lambda i, j: (i, j)),
        out_shape=jax.ShapeDtypeStruct((m, D_MODEL), BF16),
        compiler_params=_params(("parallel", "parallel")),
        name="merge_branches",
    )(o_nsa, o_gla, o_rw, w_nsa, w_gla, w_rw, z_merge, z_merge, z_merge)


def _head_rmsnorm(x, g):
    return x * lax.rsqrt(jnp.mean(x * x, axis=-1, keepdims=True) + NORM_EPS) * g


def _nsa_prep_kernel(z_ref, g_ref, qn_ref, cmp_ref, sel_ref, win_ref, selb_ref, winb_ref, gates_ref):
    hd = NSA_HD
    g_q, g_sel, g_win = g_ref[0:1, :], g_ref[2:3, :], g_ref[3:4, :]
    scale = NSA_HD ** -0.5
    for h in range(NSA_HEADS):
        q = _head_rmsnorm(z_ref[:, h * hd:(h + 1) * hd], g_q)
        qn_ref[:, h * hd:(h + 1) * hd] = (q * scale).astype(BF16)
    c0 = NSA_Q_DIM
    cmp_ref[...] = z_ref[:, c0:c0 + 2 * NSA_KV_DIM]
    for (off, gain, o_ref, ob_ref) in ((c0 + 2 * NSA_KV_DIM, g_sel, sel_ref, selb_ref),
                                       (c0 + 4 * NSA_KV_DIM, g_win, win_ref, winb_ref)):
        for h in range(NSA_KV_HEADS):
            k = _head_rmsnorm(z_ref[:, off + h * hd:off + (h + 1) * hd], gain)
            o_ref[:, h * hd:(h + 1) * hd] = k
            ob_ref[:, h * hd:(h + 1) * hd] = k.astype(BF16)
        v = z_ref[:, off + NSA_KV_DIM:off + 2 * NSA_KV_DIM]
        o_ref[:, NSA_KV_DIM:] = v
        ob_ref[:, NSA_KV_DIM:] = v.astype(BF16)
    gates_ref[...] = _sigmoid(z_ref[:, c0 + 6 * NSA_KV_DIM:])


def nsa_prep(z_nsa, qk_g, tm=256):
    m = z_nsa.shape[0]
    tm = min(tm, m)
    kvw = 2 * NSA_KV_DIM
    row = lambda w: pl.BlockSpec((tm, w), lambda i: (i, 0))
    return pl.pallas_call(
        _nsa_prep_kernel,
        grid=(m // tm,),
        in_specs=[row(NSA_W), pl.BlockSpec((4, NSA_HD), lambda i: (0, 0))],
        out_specs=[row(NSA_Q_DIM), row(kvw), row(kvw), row(kvw), row(kvw), row(kvw), row(GATE_W)],
        out_shape=[jax.ShapeDtypeStruct((m, NSA_Q_DIM), BF16),
                   jax.ShapeDtypeStruct((m, kvw), F32), jax.ShapeDtypeStruct((m, kvw), F32),
                   jax.ShapeDtypeStruct((m, kvw), F32), jax.ShapeDtypeStruct((m, kvw), BF16),
                   jax.ShapeDtypeStruct((m, kvw), BF16), jax.ShapeDtypeStruct((m, GATE_W), F32)],
        compiler_params=_params(("parallel",)),
        name="nsa_prep",
    )(z_nsa, qk_g)


QUARTERS = 2 * NSA_KV_DIM // LANES


def _compress_compute(x_ref, nb, pos_ref, w1_ref, w2_ref, kg_ref, kc_ref, vc_ref, acc_ref):
    acc_ref[...] = jnp.zeros_like(acc_ref)
    row_stride = NSA_BLOCK * QUARTERS

    def body(l, carry):
        for q in range(QUARTERS):
            kv = q // 2
            x = x_ref[pl.ds(l * QUARTERS + q, nb, stride=row_stride), :] + pos_ref[kv, pl.ds(l, 1), :]
            for half in range(2):
                acc_ref[q * 2 + half] += _dot(x[:, half * NSA_HD:(half + 1) * NSA_HD].astype(BF16), w1_ref[kv, l])
        return carry

    lax.fori_loop(0, NSA_BLOCK, body, 0)
    for q in range(QUARTERS):
        kv = q // 2
        for half in range(2):
            g = (q % 2) * 2 + half
            hid = jax.nn.gelu(acc_ref[q * 2 + half], approximate=True)
            s = _dot(hid.astype(BF16), w2_ref[kv])
            if kv == 0:
                kc_ref[:, g * NSA_HD:(g + 1) * NSA_HD] = _head_rmsnorm(s, kg_ref[...])
            else:
                vc_ref[:, g * NSA_HD:(g + 1) * NSA_HD] = s


def _compress_kernel(x_ref, pos_ref, w1_ref, w2_ref, kg_ref, kc_ref, vc_ref, acc_ref, *, nb):
    _compress_compute(x_ref, nb, pos_ref, w1_ref, w2_ref, kg_ref, kc_ref, vc_ref, acc_ref)


def _cmp_weight_specs(nidx):
    zero = lambda n: (lambda *idx: (0,) * n)
    return [pl.BlockSpec((2, NSA_BLOCK, LANES), zero(3)),
            pl.BlockSpec((2, NSA_BLOCK, NSA_HD, NSA_CMP_HID), zero(4)),
            pl.BlockSpec((2, NSA_CMP_HID, NSA_HD), zero(3)),
            pl.BlockSpec((1, NSA_HD), zero(2))]


def compress_rows(rows, cw, nb):
    nblk = rows.shape[0] // NSA_BLOCK
    nb = min(nb, nblk)
    x4 = rows.reshape(nblk * NSA_BLOCK * QUARTERS, LANES)
    out = jax.ShapeDtypeStruct((nblk, NSA_KV_DIM), F32)
    return pl.pallas_call(
        functools.partial(_compress_kernel, nb=nb),
        grid=(nblk // nb,),
        in_specs=[pl.BlockSpec((nb * NSA_BLOCK * QUARTERS, LANES), lambda i: (i, 0))] + _cmp_weight_specs(1),
        out_specs=[pl.BlockSpec((nb, NSA_KV_DIM), lambda i: (i, 0))] * 2,
        out_shape=[out, out],
        scratch_shapes=[pltpu.VMEM((2 * QUARTERS, nb, NSA_CMP_HID), F32)],
        compiler_params=_params(("parallel",)),
        name="compress_rows",
    )(x4, cw["pos2"], cw["w1"], cw["w2"], cw["kg"])


PAGES_PER_STEP = 32
BLOCKS_PER_PAGE = PAGE_SIZE // NSA_BLOCK


SLABS_PER_PAGE = 2 * NSA_KV_HEADS
PAGE_T_ROWS = SLABS_PER_PAGE * NSA_HD
CMP_FEATURES_PER_DOT = 8
assert PAGE_SIZE == LANES and BLOCKS_PER_PAGE == 2


def _compress_pages_kernel(pt_ref, cache_ref, pos_ref, wd_ref, w2_ref, kg_ref, o_ref, buf, sem, acc_ref, *,
                           layer, npg, n_c, n_steps):
    b, c = pl.program_id(0), pl.program_id(1)
    step = b * n_c + c
    slot = step % 2
    nslab = npg * SLABS_PER_PAGE

    def page_copy(bb, cc, sl, p):
        dst = buf.at[pl.ds(pl.multiple_of((sl * npg + p) * PAGE_T_ROWS, PAGE_T_ROWS), PAGE_T_ROWS), :]
        return pltpu.make_async_copy(cache_ref.at[layer, pt_ref[bb, cc * npg + p]], dst, sem.at[sl])

    @pl.when(step == 0)
    def _():
        for p in range(npg):
            page_copy(0, 0, 0, p).start()

    @pl.when(step + 1 < n_steps)
    def _():
        nxt = step + 1
        for p in range(npg):
            page_copy(nxt // n_c, nxt % n_c, 1 - slot, p).start()

    for p in range(npg):
        page_copy(b, c, slot, p).wait()

    base = slot * (npg * PAGE_T_ROWS)
    acc_ref[...] = jnp.zeros_like(acc_ref)

    def features(i, carry):
        xs = []
        for u in range(CMP_FEATURES_PER_DOT):
            d = i * CMP_FEATURES_PER_DOT + u
            x = buf[pl.ds(base + d, nslab, stride=NSA_HD), :]
            x = (x.reshape(npg, SLABS_PER_PAGE, LANES) + pos_ref[d][None]).reshape(nslab, LANES)
            xs.append(x.astype(BF16))
        k0 = pl.multiple_of(i * (CMP_FEATURES_PER_DOT * LANES), CMP_FEATURES_PER_DOT * LANES)
        acc_ref[...] += _dot(jnp.concatenate(xs, axis=1), wd_ref[pl.ds(k0, CMP_FEATURES_PER_DOT * LANES), :])
        return carry

    lax.fori_loop(0, NSA_HD // CMP_FEATURES_PER_DOT, features, 0)
    is_k = (lax.broadcasted_iota(jnp.int32, (nslab, 1), 0) % SLABS_PER_PAGE) < NSA_KV_HEADS
    half = BLOCKS_PER_PAGE * NSA_CMP_HID
    hid = jax.nn.gelu(jnp.where(is_k, acc_ref[:, :half], acc_ref[:, half:]), approximate=True).astype(BF16)
    k_out = _dot(hid, w2_ref[0])
    v_out = _dot(hid, w2_ref[1])
    k_out = jnp.concatenate([_head_rmsnorm(k_out[:, :NSA_HD], kg_ref[...]), _head_rmsnorm(k_out[:, NSA_HD:], kg_ref[...])],
                            axis=1)
    o_ref[0] = jnp.where(is_k, k_out, v_out)


def compress_pages(cache, layer, page_table, cw):
    nbatch, n_pages = page_table.shape
    npg = min(PAGES_PER_STEP, n_pages)
    n_c = n_pages // npg
    nslab = npg * SLABS_PER_PAGE
    cache_t = jnp.transpose(cache, (0, 1, 3, 4, 5, 2)).reshape(cache.shape[0], cache.shape[1], PAGE_T_ROWS, PAGE_SIZE)
    const = lambda shape: pl.BlockSpec(shape, lambda b, c, pt: (0,) * len(shape))
    grid_spec = pltpu.PrefetchScalarGridSpec(
        num_scalar_prefetch=1,
        grid=(nbatch, n_c),
        in_specs=[pl.BlockSpec(memory_space=pl.ANY), const((NSA_HD, SLABS_PER_PAGE, LANES)),
                  const((NSA_HD * LANES, 2 * BLOCKS_PER_PAGE * NSA_CMP_HID)),
                  const((2, BLOCKS_PER_PAGE * NSA_CMP_HID, BLOCKS_PER_PAGE * NSA_HD)), const((1, NSA_HD))],
        out_specs=pl.BlockSpec((1, nslab, BLOCKS_PER_PAGE * NSA_HD), lambda b, c, pt: (b, c, 0)),
        scratch_shapes=[pltpu.VMEM((2 * npg * PAGE_T_ROWS, PAGE_SIZE), F32), pltpu.SemaphoreType.DMA((2,)),
                        pltpu.VMEM((nslab, 2 * BLOCKS_PER_PAGE * NSA_CMP_HID), F32)],
    )
    out = pl.pallas_call(
        functools.partial(_compress_pages_kernel, layer=layer, npg=npg, n_c=n_c, n_steps=nbatch * n_c),
        grid_spec=grid_spec,
        out_shape=jax.ShapeDtypeStruct((nbatch, n_pages * SLABS_PER_PAGE, BLOCKS_PER_PAGE * NSA_HD), F32),
        compiler_params=_params(("arbitrary", "arbitrary")),
        name="compress_pages",
    )(page_table, cache_t, cw["pos_t"], cw["wd"], cw["w2bd"], cw["kg"])
    out = out.reshape(nbatch, n_pages, 2, NSA_KV_HEADS, BLOCKS_PER_PAGE, NSA_HD).transpose(2, 0, 1, 4, 3, 5)
    out = out.reshape(2, nbatch, n_pages * BLOCKS_PER_PAGE, NSA_KV_DIM)
    return out[0], out[1]


def _select_blocks(score, blk, cur, nblk):
    forced = (blk == 0) | (blk == cur) | (blk == cur - 1)
    score = jnp.where(forced, jnp.inf, score)
    score = jnp.where(blk <= cur, score, -jnp.inf)
    rank = jnp.zeros(score.shape, F32)
    for i in range(nblk):
        row = score[i:i + 1, :]
        ahead = (row > score) | ((row == score) & (blk > i))
        rank = rank + jnp.where(ahead, 1.0, 0.0)
    return (rank < float(min(NSA_TOPK, nblk))) & (blk <= cur)


def _alibi_slope(head):
    return 2.0 ** (-8.0 * (head + 1) / NSA_HEADS)


FAR = 1e38
M_INIT = -1e30


def _nsa_prompt_kernel(q_ref, gates_ref, selb_ref, winb_ref, kc_ref, vc_ref, o_ref, *, tq, t_len, band, kchunk):
    qi = pl.program_id(1)
    q0 = qi * tq
    nblk = t_len // NSA_BLOCK
    hd = NSA_HD
    q_pos = q0 + lax.broadcasted_iota(jnp.int32, (tq, 1), 0)
    q_pos_row = q0 + lax.broadcasted_iota(jnp.int32, (1, tq), 1)
    cur = q_pos_row // NSA_BLOCK
    blk = lax.broadcasted_iota(jnp.int32, (nblk, 1), 0)
    d_cmp = q_pos_row - ((blk + 1) * NSA_BLOCK - 1)
    m_cmp = d_cmp >= 0
    d_cmp_f = d_cmp.astype(F32)
    w0 = pl.multiple_of(jnp.maximum(q0 - NSA_WINDOW, 0), tq)
    d_win = q_pos - (w0 + lax.broadcasted_iota(jnp.int32, (1, band), 1))
    d_win_m = jnp.where((d_win >= 0) & (d_win < NSA_WINDOW), d_win.astype(F32), FAR)
    blk_row = lax.broadcasted_iota(jnp.int32, (nblk, kchunk), 0)
    key_col = lax.broadcasted_iota(jnp.int32, (nblk, kchunk), 1)
    key_lane = lax.broadcasted_iota(jnp.int32, (1, kchunk), 1)
    n_chunks = (q0 + tq + kchunk - 1) // kchunk
    for g in range(NSA_KV_HEADS):
        kc = kc_ref[0, :, g * hd:(g + 1) * hd].astype(BF16)
        vc = vc_ref[0, :, g * hd:(g + 1) * hd].astype(BF16)
        k_win = winb_ref[0, pl.ds(w0, band), g * hd:(g + 1) * hd]
        v_win = winb_ref[0, pl.ds(w0, band), NSA_KV_DIM + g * hd:NSA_KV_DIM + (g + 1) * hd]
        heads = [g * NSA_GROUP + r for r in range(NSA_GROUP)]
        group = range(NSA_GROUP)
        qs = [q_ref[0, :, h * hd:(h + 1) * hd] for h in heads]
        slopes = [_alibi_slope(h) for h in heads]
        p_cmp = [_masked_softmax_rows(_dot_nt(kc, qs[r]) - slopes[r] * d_cmp_f, m_cmp) for r in group]
        o_cmp = [_dot_tn(p_cmp[r].astype(BF16), vc) for r in group]
        sel_f = jnp.where(_select_blocks(sum(p_cmp), blk, cur, nblk), 1.0, 0.0).astype(BF16)

        def sel_chunk(ci, carry):
            k0 = pl.multiple_of(ci * kchunk, kchunk)
            k_c = selb_ref[0, pl.ds(k0, kchunk), g * hd:(g + 1) * hd]
            v_c = selb_ref[0, pl.ds(k0, kchunk), NSA_KV_DIM + g * hd:NSA_KV_DIM + (g + 1) * hd]
            expand = ((k0 + key_col) // NSA_BLOCK == blk_row).astype(BF16)
            d = q_pos - (k0 + key_lane)
            d_m = jnp.where((_dot_tn(sel_f, expand) > 0.5) & (d >= 0), d.astype(F32), FAR)
            s = [_dot_nt(qs[r], k_c) - slopes[r] * d_m for r in group]
            m_new = [jnp.maximum(carry[r][0], jnp.max(s[r], axis=-1, keepdims=True)) for r in group]
            p = [jnp.exp(s[r] - m_new[r]) for r in group]
            alpha = [jnp.exp(carry[r][0] - m_new[r]) for r in group]
            pv = [_dot(p[r].astype(BF16), v_c) for r in group]
            return tuple((m_new[r], alpha[r] * carry[r][1] + jnp.sum(p[r], axis=-1, keepdims=True),
                          alpha[r] * carry[r][2] + pv[r]) for r in group)

        init = tuple((jnp.full((tq, 1), M_INIT, F32), jnp.zeros((tq, 1), F32), jnp.zeros((tq, hd), F32))
                     for _ in group)
        sel_state = lax.fori_loop(0, n_chunks, sel_chunk, init)
        s = [_dot_nt(qs[r], k_win) - slopes[r] * d_win_m for r in group]
        p = [jnp.exp(s[r] - jnp.maximum(jnp.max(s[r], axis=-1, keepdims=True), M_INIT)) for r in group]
        pv = [_dot(p[r].astype(BF16), v_win) for r in group]
        for r, h in enumerate(heads):
            o_sel = sel_state[r][2] / jnp.maximum(sel_state[r][1], 1e-30)
            o_win = pv[r] / jnp.maximum(jnp.sum(p[r], axis=-1, keepdims=True), 1e-30)
            gate = lambda b: gates_ref[0, :, b * NSA_HEADS + h:b * NSA_HEADS + h + 1]
            o = gate(0) * o_cmp[r] + gate(1) * o_sel + gate(2) * o_win
            o_ref[0, :, h * hd:(h + 1) * hd] = o.astype(o_ref.dtype)


def nsa_prompt_attention(qn, gates, sel_b, win_b, k_cmp, v_cmp, tq=128):
    b, t, _ = qn.shape
    tq = min(tq, t)
    band = min(NSA_WINDOW + tq, t)
    nblk = t // NSA_BLOCK
    full = lambda w, n: pl.BlockSpec((1, n, w), lambda i, j: (i, 0, 0))
    tile = lambda w: pl.BlockSpec((1, tq, w), lambda i, j: (i, j, 0))
    return pl.pallas_call(
        functools.partial(_nsa_prompt_kernel, tq=tq, t_len=t, band=band, kchunk=min(512, t)),
        grid=(b, t // tq),
        in_specs=[tile(NSA_Q_DIM), tile(GATE_W), full(2 * NSA_KV_DIM, t), full(2 * NSA_KV_DIM, t),
                  full(NSA_KV_DIM, nblk), full(NSA_KV_DIM, nblk)],
        out_specs=tile(NSA_Q_DIM),
        out_shape=jax.ShapeDtypeStruct((b, t, NSA_Q_DIM), BF16),
        compiler_params=_params(("parallel", "arbitrary")),
        name="nsa_prompt_attention",
    )(qn, gates, sel_b, win_b, k_cmp, v_cmp)


def _head_group_mask(nrows):
    row = lax.broadcasted_iota(jnp.int32, (nrows, 1), 0)
    return [(row // NSA_GROUP) == g for g in range(NSA_KV_HEADS)]


def _slope_column(nrows, head0=0):
    row = lax.broadcasted_iota(jnp.int32, (nrows, 1), 0)
    col = jnp.zeros((nrows, 1), F32)
    for h in range(NSA_HEADS):
        col = jnp.where(row + head0 == h, 2.0 ** (-8.0 * (h + 1) / NSA_HEADS), col)
    return col


def _nsa_sample1_kernel(q_ref, kc_ref, vc_ref, wcache_ref, wnew_ref, ocmp_ref, owin_ref, idx_ref, *,
                        nblk, nblk_pad, past, w_buf):
    hd = NSA_HD
    q = q_ref[0]
    groups = _head_group_mask(NSA_HEADS)
    slope = _slope_column(NSA_HEADS)
    q_pos = past
    cur = q_pos // NSA_BLOCK
    blk = lax.broadcasted_iota(jnp.int32, (1, nblk_pad), 1)
    d_cmp = q_pos - ((blk + 1) * NSA_BLOCK - 1)
    s = jnp.zeros((NSA_HEADS, nblk_pad), F32)
    for g in range(NSA_KV_HEADS):
        s = s + jnp.where(groups[g], _dot_nt(q, kc_ref[0, :, g * hd:(g + 1) * hd]), 0.0)
    p = _masked_softmax(s - slope * d_cmp.astype(F32), (d_cmp >= 0) & (blk < nblk))
    o = jnp.zeros((NSA_HEADS, hd), F32)
    for g in range(NSA_KV_HEADS):
        o = o + jnp.where(groups[g], _dot(p, vc_ref[0, :, g * hd:(g + 1) * hd]), 0.0)
    ocmp_ref[0] = o
    gsel = (lax.broadcasted_iota(jnp.int32, (8, NSA_HEADS), 1) // NSA_GROUP
            == lax.broadcasted_iota(jnp.int32, (8, NSA_HEADS), 0)).astype(F32)
    score = _dot(gsel, p, HIGHEST)
    forced = (blk == 0) | (blk == cur) | (blk == cur - 1)
    score = jnp.where(forced, jnp.inf, score)
    score = jnp.where((blk <= cur) & (blk < nblk), score, -jnp.inf)
    blk_f = blk.astype(F32)
    lane_k = lax.broadcasted_iota(jnp.int32, (1, NSA_TOPK), 1)
    idx = jnp.zeros((8, NSA_TOPK), F32)
    for k in range(NSA_TOPK):
        m = jnp.max(score, axis=-1, keepdims=True)
        first = jnp.min(jnp.where(score == m, blk_f, 1e9), axis=-1, keepdims=True)
        idx = jnp.where(lane_k == k, first, idx)
        score = jnp.where(blk_f == first, -jnp.inf, score)
    idx_ref[0] = idx.astype(jnp.int32)
    i_buf = lax.broadcasted_iota(jnp.int32, (1, w_buf), 1)
    win_pos = past - w_buf + i_buf
    d_win = q_pos - win_pos
    m_win = (d_win >= 0) & (d_win < NSA_WINDOW) & (win_pos >= 0)
    s = jnp.zeros((NSA_HEADS, w_buf), F32)
    k_new = jnp.zeros((NSA_HEADS, hd), F32)
    v_new = jnp.zeros((NSA_HEADS, hd), F32)
    for g in range(NSA_KV_HEADS):
        s = s + jnp.where(groups[g], _dot_nt(q, wcache_ref[0, 0, :, g * hd:(g + 1) * hd]), 0.0)
        k_new = k_new + jnp.where(groups[g], wnew_ref[0, :, g * hd:(g + 1) * hd], 0.0)
        v_new = v_new + jnp.where(groups[g], wnew_ref[0, :, NSA_KV_DIM + g * hd:NSA_KV_DIM + (g + 1) * hd], 0.0)
    s = jnp.where(m_win, s - slope * d_win.astype(F32), -jnp.inf)
    s_new = jnp.sum(q * k_new, axis=-1, keepdims=True)
    m = jnp.maximum(jnp.max(s, axis=-1, keepdims=True), s_new)
    e = jnp.where(m_win, jnp.exp(s - m), 0.0)
    e_new = jnp.exp(s_new - m)
    denom = jnp.maximum(jnp.sum(e, axis=-1, keepdims=True) + e_new, 1e-30)
    o = e_new * v_new
    for g in range(NSA_KV_HEADS):
        o = o + jnp.where(groups[g], _dot(e, wcache_ref[0, 0, :, NSA_KV_DIM + g * hd:NSA_KV_DIM + (g + 1) * hd]), 0.0)
    owin_ref[0] = o / denom


def nsa_sample_attention1(q, k_cmp, v_cmp, cache_win, layer, win_new, nblk, past):
    nbatch = q.shape[0]
    nblk_pad = k_cmp.shape[1]
    w_buf = cache_win.shape[2]
    hq = pl.BlockSpec((1, NSA_HEADS, NSA_HD), lambda b: (b, 0, 0))
    return pl.pallas_call(
        functools.partial(_nsa_sample1_kernel, nblk=nblk, nblk_pad=nblk_pad, past=past, w_buf=w_buf),
        grid=(nbatch,),
        in_specs=[hq, pl.BlockSpec((1, nblk_pad, NSA_KV_DIM), lambda b: (b, 0, 0)),
                  pl.BlockSpec((1, nblk_pad, NSA_KV_DIM), lambda b: (b, 0, 0)),
                  pl.BlockSpec((1, 1, w_buf, 2 * NSA_KV_DIM), lambda b: (layer, b, 0, 0)),
                  pl.BlockSpec((1, 1, 2 * NSA_KV_DIM), lambda b: (b, 0, 0))],
        out_specs=[hq, hq, pl.BlockSpec((1, 8, NSA_TOPK), lambda b: (b, 0, 0))],
        out_shape=[jax.ShapeDtypeStruct((nbatch, NSA_HEADS, NSA_HD), F32),
                   jax.ShapeDtypeStruct((nbatch, NSA_HEADS, NSA_HD), F32),
                   jax.ShapeDtypeStruct((nbatch, 8, NSA_TOPK), jnp.int32)],
        compiler_params=_params(("parallel",)),
        name="nsa_sample_attention1",
    )(q, k_cmp, v_cmp, cache_win, win_new)


def _nsa_sample2_kernel(idx_ref, pt_ref, q_ref, b0_ref, b1_ref, b2_ref, b3_ref, new_ref, ocmp_ref, owin_ref, gates_ref,
                        o_ref, m_scr, l_scr, acc_scr, *, past, n_cache_blk):
    hd = NSA_HD
    b = pl.program_id(0)
    k = pl.program_id(1)
    blocks = (b0_ref, b1_ref, b2_ref, b3_ref)

    @pl.when(k == 0)
    def _():
        m_scr[...] = jnp.full(m_scr.shape, -jnp.inf, F32)
        l_scr[...] = jnp.zeros(l_scr.shape, F32)
        acc_scr[...] = jnp.zeros(acc_scr.shape, F32)

    q_pos = past
    lane = lax.broadcasted_iota(jnp.int32, (1, PAGE_SIZE), 1)
    for g in range(NSA_KV_HEADS):
        j = idx_ref[(b * 8 + g) * NSA_TOPK + k]
        slope = _slope_column(8, head0=g * NSA_GROUP)
        q = q_ref[0, g]
        k_t = blocks[g][0, 0, 0, 0]
        v_t = blocks[g][0, 0, 1, 0]
        d = q_pos - ((j // BLOCKS_PER_PAGE) * PAGE_SIZE + lane)
        in_block = (lane // NSA_BLOCK) == (j % BLOCKS_PER_PAGE)
        valid = (d >= 0) & in_block & (j < n_cache_blk)
        s = jnp.where(valid, _dot(q, k_t) - slope * d.astype(F32), -jnp.inf)
        k_new = new_ref[0, :, g * hd:(g + 1) * hd]
        v_new = new_ref[0, :, NSA_KV_DIM + g * hd:NSA_KV_DIM + (g + 1) * hd]
        new_ok = j == q_pos // NSA_BLOCK
        s_new = jnp.where(new_ok, jnp.sum(q * k_new, axis=-1, keepdims=True), -jnp.inf)
        m_old = m_scr[g]
        m_new = jnp.maximum(jnp.maximum(m_old, jnp.max(s, axis=-1, keepdims=True)), s_new)
        m_safe = jnp.where(m_new > -jnp.inf, m_new, 0.0)
        alpha = jnp.exp(m_old - m_safe)
        p = jnp.where(valid, jnp.exp(s - m_safe), 0.0)
        p_new = jnp.where(new_ok, jnp.exp(s_new - m_safe), 0.0)
        l_scr[g] = alpha * l_scr[g] + jnp.sum(p, axis=-1, keepdims=True) + p_new
        acc_scr[g] = alpha * acc_scr[g] + _dot_nt(p, v_t) + p_new * v_new
        m_scr[g] = m_new

    @pl.when(k == NSA_TOPK - 1)
    def _():
        for g in range(NSA_KV_HEADS):
            o_sel = acc_scr[g] / jnp.maximum(l_scr[g], 1e-30)
            gt = gates_ref[0, g]
            o_ref[0, g] = gt[:, 0:1] * ocmp_ref[0, g] + gt[:, 1:2] * o_sel + gt[:, 2:3] * owin_ref[0, g]


def nsa_sample_attention2(idx, page_table, q8, cache_sel, layer, sel_new, o_cmp8, o_win8, gates8, past):
    nbatch = q8.shape[0]
    n_cache_blk = past // NSA_BLOCK
    cache_t = jnp.transpose(cache_sel, (0, 1, 3, 4, 5, 2))

    def blk_map(g):
        def index_map(b, k, idx_ref, pt_ref):
            j = jnp.minimum(idx_ref[(b * 8 + g) * NSA_TOPK + k], n_cache_blk - 1)
            return (layer, pt_ref[b, j // BLOCKS_PER_PAGE], 0, g, 0, 0)
        return index_map

    g4 = lambda w: pl.BlockSpec((1, NSA_KV_HEADS, 8, w), lambda b, k, i, p: (b, 0, 0, 0))
    grid_spec = pltpu.PrefetchScalarGridSpec(
        num_scalar_prefetch=2,
        grid=(nbatch, NSA_TOPK),
        in_specs=[g4(NSA_HD)] + [pl.BlockSpec((1, 1, 2, 1, NSA_HD, PAGE_SIZE), blk_map(g)) for g in range(NSA_KV_HEADS)]
                 + [pl.BlockSpec((1, 1, 2 * NSA_KV_DIM), lambda b, k, i, p: (b, 0, 0)), g4(NSA_HD), g4(NSA_HD), g4(LANES)],
        out_specs=g4(NSA_HD),
        scratch_shapes=[pltpu.VMEM((NSA_KV_HEADS, 8, 1), F32), pltpu.VMEM((NSA_KV_HEADS, 8, 1), F32),
                        pltpu.VMEM((NSA_KV_HEADS, 8, NSA_HD), F32)],
    )
    return pl.pallas_call(
        functools.partial(_nsa_sample2_kernel, past=past, n_cache_blk=n_cache_blk),
        grid_spec=grid_spec,
        out_shape=jax.ShapeDtypeStruct((nbatch, NSA_KV_HEADS, 8, NSA_HD), F32),
        compiler_params=_params(("parallel", "arbitrary")),
        name="nsa_sample_attention2",
    )(idx, page_table, q8, cache_t, cache_t, cache_t, cache_t, sel_new, o_cmp8, o_win8, gates8)


GLA_Q0, GLA_K0, GLA_V0 = 0, GLA_QK_DIM, 2 * GLA_QK_DIM
GLA_A0 = GLA_V0 + GLA_V_DIM
GLA_G0 = GLA_A0 + GLA_A_W


def _gla_kernel(z_ref, s0_ref, a2_ref, ab_ref, ng_ref, o_ref, st_ref, s_scr, *, tt, t_valid, n_t):
    ti = pl.program_id(1)
    c = GLA_CHUNK

    @pl.when(ti == 0)
    def _():
        for h in range(GLA_HEADS):
            s_scr[h] = s0_ref[0, h].T

    a_lr = z_ref[0, :, GLA_A0:GLA_A0 + GLA_A_W].astype(BF16)
    pre = _dot(a_lr, a2_ref[...]) + ab_ref[...]
    log_a = (jnp.minimum(pre, 0.0) - jnp.log(1.0 + jnp.exp(-jnp.abs(pre)))) / GLA_TAU
    row = ti * tt + lax.broadcasted_iota(jnp.int32, (tt, 1), 0)
    log_a = jnp.where(row < t_valid, log_a, 0.0)
    tri = lax.broadcasted_iota(jnp.int32, (c, c), 0) >= lax.broadcasted_iota(jnp.int32, (c, c), 1)
    tri_f = tri.astype(F32)
    hr = range(GLA_HEADS)
    ksl = [slice(h * GLA_DK, (h + 1) * GLA_DK) for h in hr]
    states = [s_scr[h] for h in hr]
    for ci in range(tt // c):
        rows = slice(ci * c, (ci + 1) * c)
        cum = [_dot(tri_f, log_a[rows, ksl[h]], HIGHEST) for h in hr]
        last = [cum[h][c - 1:c, :] for h in hr]
        q_c = [z_ref[0, rows, GLA_Q0 + h * GLA_DK:GLA_Q0 + (h + 1) * GLA_DK] * (GLA_DK ** -0.5) for h in hr]
        k_c = [z_ref[0, rows, GLA_K0 + h * GLA_DK:GLA_K0 + (h + 1) * GLA_DK] for h in hr]
        v_c = [z_ref[0, rows, GLA_V0 + h * GLA_DV:GLA_V0 + (h + 1) * GLA_DV].astype(BF16) for h in hr]
        q_dec = [(q_c[h] * jnp.exp(cum[h])).astype(BF16) for h in hr]
        k_inv = [(k_c[h] * jnp.exp(-cum[h])).astype(BF16) for h in hr]
        k_dec = [(k_c[h] * jnp.exp(last[h] - cum[h])).astype(BF16) for h in hr]
        att = [jnp.where(tri, _dot_nt(q_dec[h], k_inv[h]), 0.0).astype(BF16) for h in hr]
        o = [_dot(att[h], v_c[h]) + _dot_nt(q_dec[h], states[h].astype(BF16)) for h in hr]
        states = [states[h] * jnp.exp(last[h]) + _dot_tn(v_c[h], k_dec[h]) for h in hr]
        for h in hr:
            on = o[h] * lax.rsqrt(jnp.mean(o[h] * o[h], axis=-1, keepdims=True) + NORM_EPS) * ng_ref[...]
            gate = z_ref[0, rows, GLA_G0 + h * GLA_DV:GLA_G0 + (h + 1) * GLA_DV]
            o_ref[0, rows, h * GLA_DV:(h + 1) * GLA_DV] = (on * (gate * _sigmoid(gate))).astype(o_ref.dtype)
    for h in hr:
        s_scr[h] = states[h]

    @pl.when(ti == n_t - 1)
    def _():
        for h in range(GLA_HEADS):
            st_ref[0, h] = s_scr[h].T


def gla_mixer(z_gla, s0, a2p, ab, ng, t_valid):
    b, t, _ = z_gla.shape
    tt = min(t, 512)
    n_t = t // tt
    return pl.pallas_call(
        functools.partial(_gla_kernel, tt=tt, t_valid=t_valid, n_t=n_t),
        grid=(b, n_t),
        in_specs=[pl.BlockSpec((1, tt, GLA_W), lambda i, j: (i, j, 0)),
                  pl.BlockSpec((1, GLA_HEADS, GLA_DK, GLA_DV), lambda i, j: (i, 0, 0, 0)),
                  pl.BlockSpec((GLA_A_W, GLA_QK_DIM), lambda i, j: (0, 0)),
                  pl.BlockSpec((1, GLA_QK_DIM), lambda i, j: (0, 0)),
                  pl.BlockSpec((1, GLA_DV), lambda i, j: (0, 0))],
        out_specs=[pl.BlockSpec((1, tt, GLA_V_DIM), lambda i, j: (i, j, 0)),
                   pl.BlockSpec((1, GLA_HEADS, GLA_DK, GLA_DV), lambda i, j: (i, 0, 0, 0))],
        out_shape=[jax.ShapeDtypeStruct((b, t, GLA_V_DIM), BF16),
                   jax.ShapeDtypeStruct((b, GLA_HEADS, GLA_DK, GLA_DV), F32)],
        scratch_shapes=[pltpu.VMEM((GLA_HEADS, GLA_DV, GLA_DK), F32)],
        compiler_params=_params(("parallel", "arbitrary")),
        name="gla_mixer",
    )(z_gla, s0, a2p, ab, ng)


RW_R0, RW_K0, RW_V0 = 0, RW_DIM, 2 * RW_DIM
RW_WL0 = 3 * RW_DIM
RW_AL0 = RW_WL0 + RW_LR_W
RW_GL0 = RW_AL0 + RW_LR_W


def _rw_prep_kernel(z_ref, sh_ref, mu_ref, w0_ref, w2_ref, a0_ref, a2_ref, g2_ref, kk_ref, ka_ref, rk_ref, seg_ref,
                    r_ref, dec_ref, k_ref, v_ref, nkk_ref, kka_ref, g_ref, bonus_ref):
    z = z_ref[...]
    xm = z + (sh_ref[...] - z) * mu_ref[...]
    r = xm[:, RW_R0:RW_R0 + RW_DIM]
    k = xm[:, RW_K0:RW_K0 + RW_DIM]
    v = xm[:, RW_V0:RW_V0 + RW_DIM]
    wl = xm[:, RW_WL0:RW_WL0 + RW_LR_W]
    al = xm[:, RW_AL0:RW_AL0 + RW_LR_W]
    gl = xm[:, RW_GL0:RW_GL0 + RW_LR_W]
    w = -_softplus(-(w0_ref[...] + _dot(jnp.tanh(wl).astype(BF16), w2_ref[...]))) - 0.5
    a = _sigmoid(a0_ref[...] + _dot(al.astype(BF16), a2_ref[...]))
    kk = k * kk_ref[...]
    norm = jnp.sqrt(_dot(kk * kk, seg_ref[...], HIGHEST))
    kk = kk / jnp.maximum(norm, 1e-12)
    k2 = k * (1.0 + (a - 1.0) * ka_ref[...])
    r_ref[...] = r
    dec_ref[...] = -jnp.exp(w)
    k_ref[...] = k2
    v_ref[...] = v
    nkk_ref[...] = -kk
    kka_ref[...] = kk * a
    g_ref[...] = _dot(_sigmoid(gl).astype(BF16), g2_ref[...])
    bonus_ref[...] = _dot(r * k2 * rk_ref[...], seg_ref[...], HIGHEST) * v


def rw_prep(z_rw, shifted, rp, tm=512):
    m = z_rw.shape[0]
    tm = min(tm, m)
    row = lambda w: pl.BlockSpec((tm, w), lambda i: (i, 0))
    const = lambda r, w: pl.BlockSpec((r, w), lambda i: (0, 0))
    out = jax.ShapeDtypeStruct((m, RW_DIM), F32)
    return pl.pallas_call(
        _rw_prep_kernel,
        grid=(m // tm,),
        in_specs=[row(RW_W), row(RW_W), const(1, RW_W), const(1, RW_DIM), const(RW_LR_W, RW_DIM), const(1, RW_DIM),
                  const(RW_LR_W, RW_DIM), const(RW_LR_W, RW_DIM), const(1, RW_DIM), const(1, RW_DIM), const(1, RW_DIM),
                  const(RW_DIM, RW_DIM)],
        out_specs=[row(RW_DIM)] * 8,
        out_shape=[out] * 8,
        compiler_params=_params(("parallel",)),
        name="rw_prep",
    )(z_rw, shifted, rp["mu"], rp["w0"], rp["w2"], rp["a0"], rp["a2"], rp["g2"], rp["kk"], rp["ka"], rp["rk"], rp["seg"])


RW_CHUNK = 64
RW_CHUNK_LOG2 = 6


def _rw_scan_kernel(r_ref, dec_ref, k_ref, v_ref, nkk_ref, kka_ref, g_ref, bonus_ref, lng_ref, lnb_ref, s0_ref,
                    o_ref, st_ref, s_scr, *, tt, n_t):
    ti = pl.program_id(1)

    @pl.when(ti == 0)
    def _():
        s_scr[...] = s0_ref[0]

    c = RW_CHUNK
    row_i = lax.broadcasted_iota(jnp.int32, (c, c), 0)
    col_i = lax.broadcasted_iota(jnp.int32, (c, c), 1)
    lower = row_i >= col_i
    strict = row_i > col_i
    lower_b = lower.astype(BF16)
    eye_f = (row_i == col_i).astype(F32)
    heads = [slice(h * RW_HD, (h + 1) * RW_HD) for h in range(RW_HEADS)]
    sp = _split2
    mm = functools.partial(_dot3, _dot)
    mm_nt = functools.partial(_dot3, _dot_nt)
    mm_tn = functools.partial(_dot3, _dot_tn)

    def chunk(i, carry):
        rows = pl.ds(pl.multiple_of(i * c, c), c)
        st = []
        for h, hs in enumerate(heads):
            lw = dec_ref[0, rows, hs]
            cum = sum(_dot(lower_b, part) for part in _split3(lw))
            g_inv = jnp.exp(-cum)
            d = {"g_last": jnp.exp(cum[c - 1:c, :]), "s0_f": s_scr[h],
                 "at_f": kka_ref[0, rows, hs] * g_inv, "kt_f": k_ref[0, rows, hs] * g_inv,
                 "bt": sp(nkk_ref[0, rows, hs] * jnp.exp(cum - lw)), "rt": sp(r_ref[0, rows, hs] * jnp.exp(cum)),
                 "v": sp(v_ref[0, rows, hs]),
                 "tail": (lng_ref[:, hs], lnb_ref[:, hs], bonus_ref[0, rows, hs], g_ref[0, rows, hs])}
            d["at"], d["kt"], d["s0"] = sp(d["at_f"]), sp(d["kt_f"]), sp(d["s0_f"])
            st.append(d)
        for d in st:
            d["ma"] = jnp.where(strict, mm_nt(d["bt"], d["at"]), 0.0)
            d["mk"] = sp(jnp.where(strict, mm_nt(d["bt"], d["kt"]), 0.0))
            d["na"] = sp(jnp.where(lower, mm_nt(d["rt"], d["at"]), 0.0))
            d["nk"] = sp(jnp.where(lower, mm_nt(d["rt"], d["kt"]), 0.0))
            d["inv"] = eye_f + d["ma"]
            d["power"] = sp(d["ma"])
            d["rhs"] = _dot_nt(d["bt"][0], d["s0"][0]) + _dot(d["mk"][0], d["v"][0])
        for _ in range(RW_CHUNK_LOG2 - 1):
            for d in st:
                d["power"] = sp(mm(d["power"], d["power"]))
                d["inv"] = d["inv"] + mm(sp(d["inv"]), d["power"])
        outs = []
        for d in st:
            u = mm(sp(d["inv"]), sp(d["rhs"])).astype(BF16)
            o = _dot_nt(d["rt"][0], d["s0"][0]) + _dot(d["na"][0], u) + _dot(d["nk"][0], d["v"][0])
            s_new = (d["s0_f"] * d["g_last"] + _dot_tn(u, (d["at_f"] * d["g_last"]).astype(BF16))
                     + _dot_tn(d["v"][0], (d["kt_f"] * d["g_last"]).astype(BF16)))
            ln_g, ln_b, bonus, gate = d["tail"]
            mean = jnp.mean(o, axis=-1, keepdims=True)
            cen = o - mean
            var = jnp.mean(cen * cen, axis=-1, keepdims=True)
            y = cen * lax.rsqrt(var + RW_GN_EPS) * ln_g + ln_b
            outs.append((s_new, (y + bonus) * gate))
        for h, hs in enumerate(heads):
            s_scr[h] = outs[h][0]
            o_ref[0, rows, hs] = outs[h][1]
        return carry

    lax.fori_loop(0, tt // c, chunk, 0)

    @pl.when(ti == n_t - 1)
    def _():
        st_ref[0] = s_scr[...]


def rw_scan(r, dec, k, v, nkk, kka, g, bonus, ln_g, ln_b, s0):
    b, t, _ = r.shape
    tt = min(t, 256)
    n_t = t // tt
    tile = pl.BlockSpec((1, tt, RW_DIM), lambda i, j: (i, j, 0))
    vec = pl.BlockSpec((1, RW_DIM), lambda i, j: (0, 0))
    st = pl.BlockSpec((1, RW_HEADS, RW_HD, RW_HD), lambda i, j: (i, 0, 0, 0))
    return pl.pallas_call(
        functools.partial(_rw_scan_kernel, tt=tt, n_t=n_t),
        grid=(b, n_t),
        in_specs=[tile] * 8 + [vec, vec, st],
        out_specs=[tile, st],
        out_shape=[jax.ShapeDtypeStruct((b, t, RW_DIM), F32), jax.ShapeDtypeStruct(s0.shape, F32)],
        scratch_shapes=[pltpu.VMEM((RW_HEADS, RW_HD, RW_HD), F32)],
        compiler_params=_params(("parallel", "arbitrary")),
        name="rw_scan",
    )(r, dec, k, v, nkk, kka, g, bonus, ln_g, ln_b, s0)


def _pad_last(w, width):
    return jnp.pad(w, [(0, 0)] * (w.ndim - 1) + [(0, width - w.shape[-1])])


def _pad_rows(w, rows):
    return jnp.pad(w, [(0, rows - w.shape[0])] + [(0, 0)] * (w.ndim - 1))


def _split_last(z, sizes):
    return jnp.split(z, [int(c) for c in np.cumsum(np.array(sizes))[:-1]], axis=-1)


def _rw_padded(cols):
    r, k, v, wl, al, gl = _split_last(cols, RW_SIZES)
    return jnp.concatenate([r, k, v, _pad_last(wl, RW_LR_W), _pad_last(al, RW_LR_W), _pad_last(gl, RW_LR_W)], axis=-1)


def _rw_unpadded(cols):
    return jnp.concatenate([cols[..., :RW_WL0], cols[..., RW_WL0:RW_WL0 + RW_DECAY_RANK],
                            cols[..., RW_AL0:RW_AL0 + RW_A_RANK], cols[..., RW_GL0:RW_GL0 + RW_G_RANK]], axis=-1)


def _layer_params(l, p):
    q, cmp_, sel, win, gate, gq, gk, gv, ga, gg, rw, mg = _split_last(p["w_in"][l], IN_SIZES)
    seg = np.kron(np.eye(RW_HEADS, dtype=np.float32), np.ones((RW_HD, RW_HD), np.float32))
    row = lambda v: v.reshape(1, -1)
    eye2 = jnp.eye(BLOCKS_PER_PAGE, dtype=F32)
    w1 = p["cmp_w1"][l].reshape(2, NSA_BLOCK, NSA_HD, NSA_CMP_HID)
    wd = jnp.einsum("ab,kldj->dalkbj", eye2, w1).reshape(NSA_HD * BLOCKS_PER_PAGE * NSA_BLOCK,
                                                          2 * BLOCKS_PER_PAGE * NSA_CMP_HID)
    w2bd = jnp.einsum("ab,kje->kajbe", eye2, p["cmp_w2"][l]).reshape(2, BLOCKS_PER_PAGE * NSA_CMP_HID,
                                                                      BLOCKS_PER_PAGE * NSA_HD)
    pos_dl = jnp.tile(jnp.swapaxes(p["cmp_pos"][l], 1, 2), (1, 1, BLOCKS_PER_PAGE))
    pos_t = jnp.swapaxes(jnp.repeat(pos_dl, NSA_KV_HEADS, axis=0), 0, 1)
    return {
        "norm1_g": p["norm1_g"][l], "norm2_g": p["norm2_g"][l],
        "w_nsa": jnp.concatenate([q, cmp_, sel, win, _pad_last(gate, GATE_W)], axis=1).astype(BF16),
        "w_gla": jnp.concatenate([gq, gk, gv, _pad_last(ga, GLA_A_W), gg], axis=1).astype(BF16),
        "w_rw": _rw_padded(rw).astype(BF16),
        "w_mg": mg.astype(BF16),
        "qk_g": p["nsa_qk_g"][l],
        "cw": {"pos2": jnp.concatenate([p["cmp_pos"][l]] * 2, axis=-1),
               "w1": p["cmp_w1"][l].reshape(2, NSA_BLOCK, NSA_HD, NSA_CMP_HID).astype(BF16),
               "w2": p["cmp_w2"][l].astype(BF16),
               "kg": row(p["nsa_qk_g"][l, 1]),
               "pos_t": pos_t, "wd": wd.astype(BF16), "w2bd": w2bd.astype(BF16)},
        "gla_a2": _pad_rows(p["gla_a2"][l], GLA_A_W).astype(BF16),
        "gla_ab": row(p["gla_a_b"][l]), "gla_ng": row(p["gla_norm_g"][l]),
        "rp": {"mu": row(_rw_padded(p["rw_mu"][l])), "w0": row(p["rw_w0"][l]),
               "w2": _pad_rows(p["rw_w2"][l], RW_LR_W).astype(BF16), "a0": row(p["rw_a0"][l]),
               "a2": _pad_rows(p["rw_a2"][l], RW_LR_W).astype(BF16),
               "g2": _pad_rows(p["rw_g2"][l], RW_LR_W).astype(BF16),
               "kk": row(p["rw_kk"][l]), "ka": row(p["rw_ka"][l]), "rk": row(p["rw_rk"][l]), "seg": jnp.asarray(seg)},
        "ln_g": row(p["rw_ln_g"][l]), "ln_b": row(p["rw_ln_b"][l]),
        "nsa_up": p["nsa_up"][l].astype(BF16), "gla_up": p["gla_up"][l].astype(BF16), "rw_up": p["rw_up"][l].astype(BF16),
        "w_out": p["w_out"][l].astype(BF16), "mlp_w1": p["mlp_w1"][l].astype(BF16), "mlp_w2": p["mlp_w2"][l].astype(BF16),
    }


def _project_in(x2, lp, tm):
    hn = rmsnorm_rows(x2, lp["norm1_g"], tm)
    return tuple(matmul(hn, lp[w], tm=tm) for w in ("w_nsa", "w_gla", "w_rw", "w_mg"))


def _project_out(x2, o_nsa, o_gla, o_rw, z_mg, lp, tm):
    merged = merge_branches(o_nsa, o_gla, o_rw, lp["nsa_up"], lp["gla_up"], lp["rw_up"], z_mg, tm=tm)
    x1 = matmul(merged, lp["w_out"], "residual", x2, tm=tm)
    h2 = rmsnorm_rows(x1, lp["norm2_g"], tm)
    u = matmul(h2, lp["mlp_w1"], "relu2", tm=tm)
    return matmul(u, lp["mlp_w2"], "residual", x1, tm=tm)


def _kv_rows(rows, b, t):
    return rows.reshape(b, t, 2, NSA_KV_HEADS, NSA_HD)


def _prompt_layer(x, lp):
    b, t, d = x.shape
    m = b * t
    tm = 512
    x2 = x.reshape(m, d)
    z_nsa, z_gla, z_rw, z_mg = _project_in(x2, lp, tm)
    qn, cmp_rows, sel_rows, win_rows, sel_b, win_b, gates = nsa_prep(z_nsa, lp["qk_g"])
    nblk = t // NSA_BLOCK
    k_cmp, v_cmp = compress_rows(cmp_rows, lp["cw"], nb=32)
    per_b = lambda a: a.reshape(b, t, a.shape[-1])
    o_nsa = nsa_prompt_attention(per_b(qn), per_b(gates), per_b(sel_b), per_b(win_b),
                                 k_cmp.reshape(b, nblk, NSA_KV_DIM), v_cmp.reshape(b, nblk, NSA_KV_DIM))
    o_gla, gla_st = gla_mixer(per_b(z_gla), jnp.zeros((b, GLA_HEADS, GLA_DK, GLA_DV), F32),
                              lp["gla_a2"], lp["gla_ab"], lp["gla_ng"], t_valid=t)
    z_rw3 = per_b(z_rw)
    shifted = jnp.concatenate([jnp.zeros((b, 1, RW_W), F32), z_rw3[:, :-1]], axis=1).reshape(m, RW_W)
    prep = rw_prep(z_rw, shifted, lp["rp"])
    o_rw, rw_st = rw_scan(*(per_b(a) for a in prep), lp["ln_g"], lp["ln_b"],
                          jnp.zeros((b, RW_HEADS, RW_HD, RW_HD), F32))
    y = _project_out(x2, o_nsa.reshape(m, -1), o_gla.reshape(m, -1), o_rw.reshape(m, -1), z_mg, lp, tm)
    w_keep = min(NSA_WINDOW, t)
    return (y.reshape(b, t, d), _kv_rows(cmp_rows, b, t), _kv_rows(sel_rows, b, t), _kv_rows(win_rows, b, t)[:, t - w_keep:],
            gla_st, rw_st, _rw_unpadded(z_rw3[:, -1]))


SAMPLE_ROWS = SUBLANES_BF16


def _sample_layer(x2, lp, layer, caches, page_table, nbatch):
    cache_cmp, cache_sel, cache_win, state_gla, state_rwkv, state_shift = caches
    tm = SAMPLE_ROWS
    past = page_table.shape[1] * PAGE_SIZE
    nblk = -(-(past + 1) // NSA_BLOCK)
    z_nsa, z_gla, z_rw, z_mg = _project_in(x2, lp, tm)
    qn, cmp_rows, sel_rows, win_rows, _, _, gates = nsa_prep(z_nsa, lp["qk_g"])
    kc_past, vc_past = compress_pages(cache_cmp, layer, page_table, lp["cw"])
    last_blk = jnp.pad(cmp_rows[:nbatch, None, :], ((0, 0), (0, NSA_BLOCK - 1), (0, 0))).reshape(nbatch * NSA_BLOCK, -1)
    kc_last, vc_last = compress_rows(last_blk, lp["cw"], nb=nbatch)
    nblk_pad = -(-nblk // 8) * 8
    with_last = lambda past_s, last_s: jnp.pad(jnp.concatenate([past_s, last_s[:, None, :]], axis=1),
                                               ((0, 0), (0, nblk_pad - nblk), (0, 0)))
    q16 = qn[:nbatch].astype(F32).reshape(nbatch, NSA_HEADS, NSA_HD)
    o_cmp, o_win, idx = nsa_sample_attention1(
        q16, with_last(kc_past, kc_last), with_last(vc_past, vc_last),
        cache_win.reshape(cache_win.shape[0], nbatch, cache_win.shape[2], 2 * NSA_KV_DIM), layer,
        win_rows[:nbatch, None, :], nblk, past)
    grp8 = lambda a: jnp.pad(a.reshape(nbatch, NSA_KV_HEADS, NSA_GROUP, a.shape[-1]),
                             ((0, 0), (0, 0), (0, 8 - NSA_GROUP), (0, 0)))
    gates3 = gates[:nbatch, :3 * NSA_HEADS].reshape(nbatch, 3, NSA_HEADS).transpose(0, 2, 1)
    o8 = nsa_sample_attention2(idx.reshape(-1), page_table, grp8(q16), cache_sel, layer, sel_rows[:nbatch, None, :],
                               grp8(o_cmp), grp8(o_win), grp8(_pad_last(gates3, LANES)), past)
    o_nsa = _pad_rows(o8[:, :, :NSA_GROUP].reshape(nbatch, NSA_Q_DIM), tm)
    z_gla_c = jnp.pad(z_gla[:nbatch, None, :], ((0, 0), (0, GLA_CHUNK - 1), (0, 0)))
    o_gla, gla_st = gla_mixer(z_gla_c, state_gla[layer], lp["gla_a2"], lp["gla_ab"], lp["gla_ng"], t_valid=1)
    o_gla = _pad_rows(o_gla[:, 0], tm)
    shifted = _pad_rows(_rw_padded(state_shift[layer]), tm)
    prep = rw_prep(z_rw, shifted, lp["rp"])
    prep = [jnp.pad(a[:nbatch, None, :], ((0, 0), (0, RW_CHUNK - 1), (0, 0))) for a in prep]
    o_rw, rw_st = rw_scan(*prep, lp["ln_g"], lp["ln_b"], state_rwkv[layer])
    o_rw = _pad_rows(o_rw[:, 0], tm)
    y = _project_out(x2, o_nsa, o_gla, o_rw, z_mg, lp, tm)
    return (y, _kv_rows(cmp_rows[:nbatch], nbatch, 1), _kv_rows(sel_rows[:nbatch], nbatch, 1),
            _kv_rows(win_rows[:nbatch], nbatch, 1), gla_st, rw_st, _rw_unpadded(z_rw[:nbatch]))


def kernel(x_prompt, x_sample, cache_cmp_kv, cache_sel_kv, cache_win_kv, state_gla, state_rwkv, state_rwkv_shift,
           page_table, norm1_g, w_in, nsa_qk_g, cmp_pos, cmp_w1, cmp_w2, nsa_up, gla_a2, gla_a_b, gla_norm_g, gla_up,
           rw_mu, rw_w0, rw_w2, rw_a0, rw_a2, rw_g2, rw_kk, rw_ka, rw_rk, rw_ln_g, rw_ln_b, rw_up, w_out, norm2_g,
           mlp_w1, mlp_w2):
    p = dict(norm1_g=norm1_g, w_in=w_in, nsa_qk_g=nsa_qk_g, cmp_pos=cmp_pos, cmp_w1=cmp_w1, cmp_w2=cmp_w2, nsa_up=nsa_up,
             gla_a2=gla_a2, gla_a_b=gla_a_b, gla_norm_g=gla_norm_g, gla_up=gla_up, rw_mu=rw_mu, rw_w0=rw_w0, rw_w2=rw_w2,
             rw_a0=rw_a0, rw_a2=rw_a2, rw_g2=rw_g2, rw_kk=rw_kk, rw_ka=rw_ka, rw_rk=rw_rk, rw_ln_g=rw_ln_g,
             rw_ln_b=rw_ln_b, rw_up=rw_up, w_out=w_out, norm2_g=norm2_g, mlp_w1=mlp_w1, mlp_w2=mlp_w2)
    nbatch = x_sample.shape[0]
    caches = (cache_cmp_kv, cache_sel_kv, cache_win_kv, state_gla, state_rwkv, state_rwkv_shift)
    xp = x_prompt
    xs = _pad_rows(x_sample.reshape(nbatch, D_MODEL), SAMPLE_ROWS)
    outs_p, outs_s = [], []
    for l in range(DEPTH):
        lp = _layer_params(l, p)
        xp, *rest_p = _prompt_layer(xp, lp)
        xs, *rest_s = _sample_layer(xs, lp, l, caches, page_table, nbatch)
        outs_p.append(rest_p)
        outs_s.append(rest_s)
    stack = lambda outs, i: jnp.stack([o[i] for o in outs])
    return ((xp, xs[:nbatch].reshape(x_sample.shape))
            + tuple(stack(outs_p, i) for i in range(6)) + tuple(stack(outs_s, i) for i in range(6)))
```
